```python
import functools
import jax, jax.numpy as jnp
from jax import lax
import numpy as np

D_MODEL = 2048
BATCH = 2
SEQ = 4096
DEPTH = 2
DEC_BATCH = 32
DEC_SEQ = 1
PAST_LEN = 8192
PAGE_SIZE = 128

D_MIX = D_MODEL
D_CONF = D_MIX // 4
D_SB = D_MIX // 2
D_SC = D_MIX - D_CONF - D_SB
SB_HEAD_DIM = 128
SB_HEADS = D_SB // SB_HEAD_DIM
SB_BIAS_INIT = -6.0
CONF_WIDTH = 31
SC_WIDTH = 3
Q_BLOCK = 128
N_GROUPS = 4
EXPERTS_PER_GROUP = 4
N_EXPERTS = N_GROUPS * EXPERTS_PER_GROUP
TOP_K_IN_GROUP = 2
D_EXPERT = D_MODEL // 4
LN_EPS = 1e-5
DEEPNORM_ALPHA = (2 * DEPTH) ** 0.25
DEEPNORM_BETA = (8 * DEPTH) ** -0.25
P_IN = 2 * D_CONF + 3 * D_SB + 3 * D_SC

kernel_name = 'hymba_conformer_stickbreak_shortconv_hmoe_step'


def layer_norm(x, g, b):
    xf = x.astype(jnp.float32)
    mu = jnp.mean(xf, axis=-1, keepdims=True)
    var = jnp.mean(jnp.square(xf - mu), axis=-1, keepdims=True)
    return ((xf - mu) * lax.rsqrt(var + LN_EPS) * g + b).astype(x.dtype)


def causal_depthwise_conv(x_ext, w):
    return lax.conv_general_dilated(
        x_ext, w[:, None, :].astype(x_ext.dtype), window_strides=(1,), padding='VALID',
        dimension_numbers=('NWC', 'WIO', 'NWC'), feature_group_count=x_ext.shape[-1])


def stick_breaking(q, k, v, bias, q_pos, k_pos):
    z = jnp.einsum('bqhd,bkhd->bhqk', q, k, preferred_element_type=jnp.float32) * (SB_HEAD_DIM ** -0.5)
    z = z + bias.astype(jnp.float32)[None, :, None, None]
    mask = k_pos[None, :] < q_pos[:, None]
    neg_log_one_minus_beta = jnp.where(mask, jax.nn.softplus(z), 0.0)
    between = lax.cumsum(neg_log_one_minus_beta, axis=3, reverse=True) - neg_log_one_minus_beta
    a = jnp.where(mask, jnp.exp(jax.nn.log_sigmoid(z) - between), 0.0)
    return jnp.einsum('bhqk,bkhd->bqhd', a, v.astype(jnp.float32)).astype(v.dtype)


def prompt_attention(q, k, v, bias):
    b, s, h, d = q.shape
    nb = s // Q_BLOCK
    pos = jnp.arange(s, dtype=jnp.int32)
    q_blocks = q.reshape(b, nb, Q_BLOCK, h, d).swapaxes(0, 1)
    pos_blocks = pos.reshape(nb, Q_BLOCK)
    out = lax.map(lambda a: stick_breaking(a[0], k, v, bias, a[1], pos), (q_blocks, pos_blocks))
    return out.swapaxes(0, 1).reshape(b, s, h, d)


def sample_attention(q, k, v, bias, cache_k, cache_v, page_table):
    nb, t = q.shape[0], q.shape[1]
    past = page_table.shape[1] * cache_k.shape[1]
    k_past = cache_k[page_table].reshape(nb, past, SB_HEADS, SB_HEAD_DIM)
    v_past = cache_v[page_table].reshape(nb, past, SB_HEADS, SB_HEAD_DIM)
    k_all = jnp.concatenate([k_past.astype(k.dtype), k], axis=1)
    v_all = jnp.concatenate([v_past.astype(v.dtype), v], axis=1)
    k_pos = jnp.arange(past + t, dtype=jnp.int32)
    q_pos = past + jnp.arange(t, dtype=jnp.int32)
    return stick_breaking(q, k_all, v_all, bias, q_pos, k_pos)


def split_in(u):
    sizes = (D_CONF, D_CONF, D_SB, D_SB, D_SB, D_SC, D_SC, D_SC)
    idx = np.cumsum(sizes)[:-1].tolist()
    return jnp.split(u, idx, axis=-1)


def mixer(x, conv_buf, sc_buf, attend, w_in, conv_w, conv_b, conv_ln_g, conv_ln_b, sconv_w, sb_bias, w_out):
    b, t, _ = x.shape
    c_val, c_gate, q, k, v, g_b, g_c, h = split_in(x @ w_in)
    glu = c_val * jax.nn.sigmoid(c_gate)
    conf_ext = jnp.concatenate([conv_buf.astype(glu.dtype), glu], axis=1)
    conf = jax.nn.silu(layer_norm(causal_depthwise_conv(conf_ext, conv_w) + conv_b, conv_ln_g, conv_ln_b))
    sc_ext = jnp.concatenate([sc_buf.astype(h.dtype), g_c * h], axis=1)
    sc = g_b * causal_depthwise_conv(sc_ext, sconv_w)
    heads = (b, t, SB_HEADS, SB_HEAD_DIM)
    q, k, v = q.reshape(heads), k.reshape(heads), v.reshape(heads)
    att = attend(q, k, v, sb_bias).reshape(b, t, D_SB)
    y = jnp.concatenate([conf, att, sc], axis=-1) @ w_out
    return y, k, v, conf_ext[:, -(CONF_WIDTH - 1):], sc_ext[:, -(SC_WIDTH - 1):]


def hierarchical_moe(x, rg_w, rg_b, re_w, re_b, w_gate, w_up, w_down):
    lead = x.shape[:-1]
    xt = x.reshape(-1, x.shape[-1])
    lg = (xt @ rg_w + rg_b).astype(jnp.float32)
    pg = jax.nn.softmax(lg, axis=-1)
    g_idx = jnp.argmax(lg, axis=-1)
    g_gate = jnp.take_along_axis(pg, g_idx[:, None], axis=-1)
    le = (jnp.einsum('nd,gde->nge', xt, re_w) + re_b).astype(jnp.float32)
    le_sel = jnp.take_along_axis(le, g_idx[:, None, None], axis=1)[:, 0]
    top_v, top_i = lax.top_k(le_sel, TOP_K_IN_GROUP)
    gates = g_gate * jax.nn.softmax(top_v, axis=-1)
    eid = g_idx[:, None] * EXPERTS_PER_GROUP + top_i
    comb = jnp.einsum('nk,nke->ne', gates, jax.nn.one_hot(eid, N_EXPERTS, dtype=jnp.float32))
    hid = jax.nn.silu(jnp.einsum('nd,edf->nef', xt, w_gate)) * jnp.einsum('nd,edf->nef', xt, w_up)
    y = jnp.einsum('nef,efd->nd', hid * comb[:, :, None].astype(hid.dtype), w_down)
    return y.reshape(*lead, -1).astype(x.dtype)


def trunk_layer(x, conv_buf, sc_buf, attend, w_in, conv_w, conv_b, conv_ln_g, conv_ln_b, sconv_w, sb_bias,
                w_out, ln1_g, ln1_b, rg_w, rg_b, re_w, re_b, w_gate, w_up, w_down, ln2_g, ln2_b):
    mix, k, v, conv_state, sc_state = mixer(x, conv_buf, sc_buf, attend, w_in, conv_w, conv_b,
                                            conv_ln_g, conv_ln_b, sconv_w, sb_bias, w_out)
    x = layer_norm(DEEPNORM_ALPHA * x + mix, ln1_g, ln1_b)
    x = layer_norm(DEEPNORM_ALPHA * x + hierarchical_moe(x, rg_w, rg_b, re_w, re_b, w_gate, w_up, w_down),
                   ln2_g, ln2_b)
    return x, k, v, conv_state, sc_state


def setup_inputs(seed: int = 0) -> dict:
    key = jax.random.key(seed)
    ks = jax.random.split(key, 32)
    f32 = jnp.float32
    nrm = jax.random.normal
    n_pages = PAST_LEN // PAGE_SIZE
    n_pool = (DEC_BATCH * n_pages * 5) // 4
    beta = DEEPNORM_BETA
    col_scale = jnp.concatenate([
        jnp.full((D_CONF,), beta, f32), jnp.ones((D_CONF,), f32),
        jnp.ones((2 * D_SB,), f32), jnp.full((D_SB,), beta, f32),
        jnp.ones((2 * D_SC,), f32), jnp.full((D_SC,), beta, f32)])
    page_table = jax.random.permutation(ks[6], n_pool)[: DEC_BATCH * n_pages]
    page_table = page_table.reshape(DEC_BATCH, n_pages).astype(jnp.int32)
    return {
        'x_prompt': nrm(ks[0], (BATCH, SEQ, D_MODEL), f32),
        'x_sample': nrm(ks[1], (DEC_BATCH, DEC_SEQ, D_MODEL), f32),
        'cache_k': nrm(ks[2], (DEPTH, n_pool, PAGE_SIZE, SB_HEADS, SB_HEAD_DIM), f32),
        'cache_v': beta * nrm(ks[3], (DEPTH, n_pool, PAGE_SIZE, SB_HEADS, SB_HEAD_DIM), f32),
        'state_conv': 0.5 * nrm(ks[4], (DEPTH, DEC_BATCH, CONF_WIDTH - 1, D_CONF), f32),
        'state_sconv': 0.5 * nrm(ks[5], (DEPTH, DEC_BATCH, SC_WIDTH - 1, D_SC), f32),
        'page_table': page_table,
        'w_in': nrm(ks[7], (DEPTH, D_MODEL, P_IN), f32) * (D_MODEL ** -0.5) * col_scale,
        'conv_w': nrm(ks[8], (DEPTH, CONF_WIDTH, D_CONF), f32) * (CONF_WIDTH ** -0.5),
        'conv_b': 0.01 * nrm(ks[9], (DEPTH, D_CONF), f32),
        'conv_ln_g': 1.0 + 0.05 * nrm(ks[10], (DEPTH, D_CONF), f32),
        'conv_ln_b': 0.01 * nrm(ks[11], (DEPTH, D_CONF), f32),
        'sconv_w': nrm(ks[12], (DEPTH, SC_WIDTH, D_SC), f32) * (SC_WIDTH ** -0.5),
        'sb_bias': SB_BIAS_INIT + 0.1 * nrm(ks[25], (DEPTH, SB_HEADS), f32),
        'w_out': nrm(ks[13], (DEPTH, D_MIX, D_MODEL), f32) * (D_MIX ** -0.5) * beta,
        'ln1_g': 1.0 + 0.05 * nrm(ks[14], (DEPTH, D_MODEL), f32),
        'ln1_b': 0.01 * nrm(ks[15], (DEPTH, D_MODEL), f32),
        'router_group_w': nrm(ks[16], (DEPTH, D_MODEL, N_GROUPS), f32) * (D_MODEL ** -0.5),
        'router_group_b': 0.01 * nrm(ks[17], (DEPTH, N_GROUPS), f32),
        'router_expert_w': nrm(ks[18], (DEPTH, N_GROUPS, D_MODEL, EXPERTS_PER_GROUP), f32) * (D_MODEL ** -0.5),
        'router_expert_b': 0.01 * nrm(ks[19], (DEPTH, N_GROUPS, EXPERTS_PER_GROUP), f32),
        'expert_w_gate': nrm(ks[20], (DEPTH, N_EXPERTS, D_MODEL, D_EXPERT), f32) * (D_MODEL ** -0.5),
        'expert_w_up': nrm(ks[21], (DEPTH, N_EXPERTS, D_MODEL, D_EXPERT), f32) * (D_MODEL ** -0.5) * beta,
        'expert_w_down': nrm(ks[22], (DEPTH, N_EXPERTS, D_EXPERT, D_MODEL), f32) * (D_EXPERT ** -0.5) * beta,
        'ln2_g': 1.0 + 0.05 * nrm(ks[23], (DEPTH, D_MODEL), f32),
        'ln2_b': 0.01 * nrm(ks[24], (DEPTH, D_MODEL), f32),
    }


def reference(x_prompt, x_sample, cache_k, cache_v, state_conv, state_sconv, page_table,
              w_in, conv_w, conv_b, conv_ln_g, conv_ln_b, sconv_w, sb_bias, w_out, ln1_g, ln1_b,
              router_group_w, router_group_b, router_expert_w, router_expert_b,
              expert_w_gate, expert_w_up, expert_w_down, ln2_g, ln2_b):
    yp, ys = x_prompt, x_sample
    kp, vp, cp, sp = [], [], [], []
    kd, vd, cd, sd = [], [], [], []
    for l in range(DEPTH):
        lw = (w_in[l], conv_w[l], conv_b[l], conv_ln_g[l], conv_ln_b[l], sconv_w[l], sb_bias[l], w_out[l],
              ln1_g[l], ln1_b[l], router_group_w[l], router_group_b[l], router_expert_w[l],
              router_expert_b[l], expert_w_gate[l], expert_w_up[l], expert_w_down[l], ln2_g[l], ln2_b[l])
        zc = jnp.zeros((yp.shape[0], CONF_WIDTH - 1, D_CONF), yp.dtype)
        zs = jnp.zeros((yp.shape[0], SC_WIDTH - 1, D_SC), yp.dtype)
        yp, k, v, c, s = trunk_layer(yp, zc, zs, prompt_attention, *lw)
        kp.append(k); vp.append(v); cp.append(c); sp.append(s)
        attend = functools.partial(sample_attention, cache_k=cache_k[l], cache_v=cache_v[l],
                                   page_table=page_table)
        ys, k, v, c, s = trunk_layer(ys, state_conv[l], state_sconv[l], attend, *lw)
        kd.append(k); vd.append(v); cd.append(c); sd.append(s)
    return (yp, ys, jnp.stack(kp), jnp.stack(vp), jnp.stack(cp), jnp.stack(sp),
            jnp.stack(kd), jnp.stack(vd), jnp.stack(cd), jnp.stack(sd))
```

```python
import functools
import math

import jax
import jax.numpy as jnp
from jax import lax
from jax.experimental import pallas as pl
from jax.experimental.pallas import tpu as pltpu

F32 = jnp.float32
BF16 = jnp.bfloat16

SB_HEAD_DIM = 128
CONF_WIDTH = 31
SC_WIDTH = 3
N_GROUPS = 4
EXPERTS_PER_GROUP = 4
N_EXPERTS = N_GROUPS * EXPERTS_PER_GROUP
LN_EPS = 1e-5

LANES = 128
SUBLANES = 8
VMEM_LIMIT_MB = 56

INPROJ_TM = 512
INPROJ_TN = 512
MIX_T = 256
MIX_RC = 32
CONV_HIST = 32
SC_HIST = 8
ATT_TQ = 512
ATT_TK = 256
PAGES_PER_STEP = 8
OUT_TM = 256
MOE_TM = 512


def _cparams(sem):
    return pltpu.CompilerParams(dimension_semantics=sem,
                                vmem_limit_bytes=VMEM_LIMIT_MB * 1024 * 1024)


def _layer_norm(y, g, b):
    mu = jnp.mean(y, axis=-1, keepdims=True)
    d = y - mu
    var = jnp.mean(d * d, axis=-1, keepdims=True)
    return d * lax.rsqrt(var + LN_EPS) * g + b


def _softplus(z):
    return jnp.maximum(z, 0.0) + jnp.log(1.0 + jnp.exp(-jnp.abs(z)))


def _split_bf16(x):
    hi = x.astype(BF16)
    lo = (x - hi.astype(F32)).astype(BF16)
    return hi, lo


def _inproj_kernel(bounds, scale, x_ref, w_ref, cg_ref, q_ref, k_ref, v_ref, gch_ref, xb_ref):
    j = pl.program_id(1)

    @pl.when(j == 0)
    def _():
        xb_ref[...] = x_ref[...].astype(BF16)

    acc = jnp.dot(xb_ref[...], w_ref[...].astype(BF16), preferred_element_type=F32)
    b_cg, b_q, b_k, b_v = bounds

    @pl.when(j < b_cg)
    def _():
        cg_ref[...] = acc

    @pl.when((j >= b_cg) & (j < b_q))
    def _():
        q_ref[...] = (acc * scale).astype(BF16)

    @pl.when((j >= b_q) & (j < b_k))
    def _():
        k_ref[...] = acc

    @pl.when((j >= b_k) & (j < b_v))
    def _():
        v_ref[...] = acc

    @pl.when(j >= b_v)
    def _():
        gch_ref[...] = acc


def _inproj(x, w_in, layer, d_conf, d_sb, d_sc):
    m, d_model = x.shape
    p_in = w_in.shape[-1]
    tm = min(INPROJ_TM, m)
    tn = INPROJ_TN
    n_cg, n_sb, n_gch = 2 * d_conf // tn, d_sb // tn, 3 * d_sc // tn
    bounds = (n_cg, n_cg + n_sb, n_cg + 2 * n_sb, n_cg + 3 * n_sb)
    assert bounds[-1] + n_gch == p_in // tn and m % tm == 0

    def out_map(lo, n):
        return lambda i, j: (i, jnp.clip(j - lo, 0, n - 1))

    kernel = functools.partial(_inproj_kernel, bounds, SB_HEAD_DIM ** -0.5)
    return pl.pallas_call(
        kernel,
        grid=(m // tm, p_in // tn),
        in_specs=[
            pl.BlockSpec((tm, d_model), lambda i, j: (i, 0)),
            pl.BlockSpec((None, d_model, tn), lambda i, j: (layer, 0, j)),
        ],
        out_specs=[
            pl.BlockSpec((tm, tn), out_map(0, n_cg)),
            pl.BlockSpec((tm, tn), out_map(bounds[0], n_sb)),
            pl.BlockSpec((tm, tn), out_map(bounds[1], n_sb)),
            pl.BlockSpec((tm, tn), out_map(bounds[2], n_sb)),
            pl.BlockSpec((tm, tn), out_map(bounds[3], n_gch)),
        ],
        out_shape=[
            jax.ShapeDtypeStruct((m, 2 * d_conf), F32),
            jax.ShapeDtypeStruct((m, d_sb), BF16),
            jax.ShapeDtypeStruct((m, d_sb), F32),
            jax.ShapeDtypeStruct((m, d_sb), F32),
            jax.ShapeDtypeStruct((m, 3 * d_sc), F32),
        ],
        scratch_shapes=[pltpu.VMEM((tm, d_model), BF16)],
        compiler_params=_cparams(("arbitrary", "arbitrary")),
        name="inproj",
    )(x, w_in)


def _mixer_prompt_kernel(val_ref, gate_ref, gb_ref, gc_ref, h_ref, cw_ref, cb_ref, lg_ref, lb_ref, sw_ref,
                         conf_ref, sc_ref, cst_ref, sst_ref, ext_ref, ext2_ref):
    t = pl.program_id(1)
    tt, c = val_ref.shape

    @pl.when(t == 0)
    def _():
        ext_ref[0:CONV_HIST, :] = jnp.zeros((CONV_HIST, c), F32)
        ext2_ref[0:SC_HIST, :] = jnp.zeros((SC_HIST, ext2_ref.shape[1]), F32)

    ext_ref[CONV_HIST:CONV_HIST + tt, :] = val_ref[...] * jax.nn.sigmoid(gate_ref[...])
    ext2_ref[SC_HIST:SC_HIST + tt, :] = gc_ref[...] * h_ref[...]

    cb = cb_ref[...]
    lg = lg_ref[...]
    lb = lb_ref[...]
    conv_off = CONV_HIST - (CONF_WIDTH - 1)
    sc_off = SC_HIST - (SC_WIDTH - 1)
    for ch in range(tt // MIX_RC):
        r0 = ch * MIX_RC
        acc = jnp.zeros((MIX_RC, c), F32)
        for w in range(CONF_WIDTH):
            acc = acc + ext_ref[r0 + conv_off + w:r0 + conv_off + w + MIX_RC, :] * cw_ref[w:w + 1, :]
        yn = _layer_norm(acc + cb, lg, lb)
        conf_ref[r0:r0 + MIX_RC, :] = (yn * jax.nn.sigmoid(yn)).astype(conf_ref.dtype)
        s = jnp.zeros((MIX_RC, ext2_ref.shape[1]), F32)
        for w in range(SC_WIDTH):
            s = s + ext2_ref[r0 + sc_off + w:r0 + sc_off + w + MIX_RC, :] * sw_ref[w:w + 1, :]
        sc_ref[r0:r0 + MIX_RC, :] = (gb_ref[r0:r0 + MIX_RC, :] * s).astype(sc_ref.dtype)

    ext_ref[0:CONV_HIST, :] = ext_ref[tt:tt + CONV_HIST, :]
    ext2_ref[0:SC_HIST, :] = ext2_ref[tt:tt + SC_HIST, :]

    @pl.when(t == pl.num_programs(1) - 1)
    def _():
        cst_ref[...] = ext_ref[tt:tt + CONV_HIST, :]
        sst_ref[...] = ext2_ref[tt:tt + SC_HIST, :]


def _mixer_prompt(cg, gch, batch, seq, layer, conv_w, conv_b, conv_ln_g, conv_ln_b, sconv_w):
    d_conf = cg.shape[1] // 2
    d_sc = gch.shape[1] // 3
    tt = MIX_T
    nt = seq // tt
    row = lambda b, t: (b * nt + t, 0)
    col = lambda cidx: (lambda b, t: (b * nt + t, cidx))
    par3 = lambda b, t: (layer, 0, 0)
    return pl.pallas_call(
        _mixer_prompt_kernel,
        grid=(batch, nt),
        in_specs=[
            pl.BlockSpec((tt, d_conf), col(0)),
            pl.BlockSpec((tt, d_conf), col(1)),
            pl.BlockSpec((tt, d_sc), col(0)),
            pl.BlockSpec((tt, d_sc), col(1)),
            pl.BlockSpec((tt, d_sc), col(2)),
            pl.BlockSpec((None, CONF_WIDTH, d_conf), par3),
            pl.BlockSpec((None, 1, d_conf), par3),
            pl.BlockSpec((None, 1, d_conf), par3),
            pl.BlockSpec((None, 1, d_conf), par3),
            pl.BlockSpec((None, SC_WIDTH, d_sc), par3),
        ],
        out_specs=[
            pl.BlockSpec((tt, d_conf), row),
            pl.BlockSpec((tt, d_sc), row),
            pl.BlockSpec((None, CONV_HIST, d_conf), lambda b, t: (b, 0, 0)),
            pl.BlockSpec((None, SC_HIST, d_sc), lambda b, t: (b, 0, 0)),
        ],
        out_shape=[
            jax.ShapeDtypeStruct((batch * seq, d_conf), BF16),
            jax.ShapeDtypeStruct((batch * seq, d_sc), BF16),
            jax.ShapeDtypeStruct((batch, CONV_HIST, d_conf), F32),
            jax.ShapeDtypeStruct((batch, SC_HIST, d_sc), F32),
        ],
        scratch_shapes=[pltpu.VMEM((tt + CONV_HIST, d_conf), F32),
                        pltpu.VMEM((tt + SC_HIST, d_sc), F32)],
        compiler_params=_cparams(("arbitrary", "arbitrary")),
        name="mixer_prompt",
    )(cg, cg, gch, gch, gch, conv_w, conv_b, conv_ln_g, conv_ln_b, sconv_w)


def _mixer_sample_kernel(cg_ref, gch_ref, cs_ref, ss_ref, cw_ref, cb_ref, lg_ref, lb_ref, sw_ref,
                         conf_ref, sc_ref, ncs_ref, nss_ref):
    c = cs_ref.shape[-1]
    c2 = ss_ref.shape[-1]
    hist = CONF_WIDTH - 1
    glu = cg_ref[:, 0:c] * jax.nn.sigmoid(cg_ref[:, c:2 * c])
    acc = glu * cw_ref[hist:hist + 1, :]
    for w in range(hist):
        acc = acc + cs_ref[w] * cw_ref[w:w + 1, :]
    yn = _layer_norm(acc + cb_ref[...], lg_ref[...], lb_ref[...])
    conf_ref[...] = (yn * jax.nn.sigmoid(yn)).astype(conf_ref.dtype)

    p = gch_ref[:, c2:2 * c2] * gch_ref[:, 2 * c2:3 * c2]
    s = p * sw_ref[SC_WIDTH - 1:SC_WIDTH, :]
    for w in range(SC_WIDTH - 1):
        s = s + ss_ref[w] * sw_ref[w:w + 1, :]
    sc_ref[...] = (gch_ref[:, 0:c2] * s).astype(sc_ref.dtype)

    for w in range(hist - 1):
        ncs_ref[w] = cs_ref[w + 1]
    ncs_ref[hist - 1] = glu
    for w in range(SC_WIDTH - 2):
        nss_ref[w] = ss_ref[w + 1]
    nss_ref[SC_WIDTH - 2] = p


def _mixer_sample(cg, gch, cs_t, ss_t, layer, conv_w, conv_b, conv_ln_g, conv_ln_b, sconv_w):
    nb = cg.shape[0]
    d_conf = cg.shape[1] // 2
    d_sc = gch.shape[1] // 3
    full2 = lambda i: (0, 0)
    par3 = lambda i: (layer, 0, 0)
    st4 = lambda i: (layer, 0, 0, 0)
    return pl.pallas_call(
        _mixer_sample_kernel,
        grid=(1,),
        in_specs=[
            pl.BlockSpec(cg.shape, full2),
            pl.BlockSpec(gch.shape, full2),
            pl.BlockSpec((None,) + cs_t.shape[1:], st4),
            pl.BlockSpec((None,) + ss_t.shape[1:], st4),
            pl.BlockSpec((None, CONF_WIDTH, d_conf), par3),
            pl.BlockSpec((None, 1, d_conf), par3),
            pl.BlockSpec((None, 1, d_conf), par3),
            pl.BlockSpec((None, 1, d_conf), par3),
            pl.BlockSpec((None, SC_WIDTH, d_sc), par3),
        ],
        out_specs=[
            pl.BlockSpec((nb, d_conf), full2),
            pl.BlockSpec((nb, d_sc), full2),
            pl.BlockSpec(cs_t.shape[1:], lambda i: (0, 0, 0)),
            pl.BlockSpec(ss_t.shape[1:], lambda i: (0, 0, 0)),
        ],
        out_shape=[
            jax.ShapeDtypeStruct((nb, d_conf), BF16),
            jax.ShapeDtypeStruct((nb, d_sc), BF16),
            jax.ShapeDtypeStruct(cs_t.shape[1:], F32),
            jax.ShapeDtypeStruct(ss_t.shape[1:], F32),
        ],
        compiler_params=_cparams(("arbitrary",)),
        name="mixer_sample",
    )(cg, gch, cs_t, ss_t, conv_w, conv_b, conv_ln_g, conv_ln_b, sconv_w)


def _attn_prompt_kernel(bias_ref, q_ref, k_ref, v_ref, tri_ref, o_ref, kb_ref, vb_ref):
    h = pl.program_id(1)
    i = pl.program_id(2)
    tq = q_ref.shape[0]
    tk = tri_ref.shape[0]
    ratio = tq // tk

    @pl.when(i == 0)
    def _():
        kb_ref[...] = k_ref[...].astype(BF16)
        vb_ref[...] = v_ref[...].astype(BF16)

    bias = bias_ref[h]
    q = q_ref[...]
    tri = tri_ref[...]

    def block(kb, acc, run, masked):
        start = pl.multiple_of(kb * tk, tk)
        kblk = kb_ref[pl.ds(start, tk), :]
        vblk = vb_ref[pl.ds(start, tk), :]
        z = lax.dot_general(q, kblk, (((1,), (1,)), ((), ())), preferred_element_type=F32) + bias
        sp = _softplus(z)
        if masked:
            row = i * tq + lax.broadcasted_iota(jnp.int32, (tq, tk), 0)
            colp = kb * tk + lax.broadcasted_iota(jnp.int32, (tq, tk), 1)
            keep = colp < row
            sp = jnp.where(keep, sp, 0.0)
        hi, lo = _split_bf16(sp)
        csum = (jnp.dot(hi, tri, preferred_element_type=F32)
                + jnp.dot(lo, tri, preferred_element_type=F32))
        a = jnp.exp(z - csum - run)
        if masked:
            a = jnp.where(keep, a, 0.0)
        acc = acc + jnp.dot(a.astype(BF16), vblk, preferred_element_type=F32)
        run = run + jnp.sum(sp, axis=1, keepdims=True)
        return acc, run

    acc = jnp.zeros((tq, q_ref.shape[1]), F32)
    run = jnp.zeros((tq, 1), F32)
    for d in range(ratio):
        acc, run = block((i + 1) * ratio - 1 - d, acc, run, True)

    def body(jj, carry):
        return block(i * ratio - 1 - jj, carry[0], carry[1], False)

    acc, run = lax.fori_loop(0, i * ratio, body, (acc, run))
    o_ref[...] = acc.astype(o_ref.dtype)


def _attn_prompt(q, k, v, bias, batch, seq):
    d_sb = q.shape[1]
    heads = d_sb // SB_HEAD_DIM
    tq, tk = ATT_TQ, ATT_TK
    nq = seq // tq
    r = lax.broadcasted_iota(jnp.int32, (tk, tk), 0)
    c = lax.broadcasted_iota(jnp.int32, (tk, tk), 1)
    tri = (r >= c).astype(BF16)
    grid_spec = pltpu.PrefetchScalarGridSpec(
        num_scalar_prefetch=0,
        grid=(batch, heads, nq),
        in_specs=[
            pl.BlockSpec(memory_space=pltpu.SMEM),
            pl.BlockSpec((tq, SB_HEAD_DIM), lambda b, h, i: (b * nq + i, h)),
            pl.BlockSpec((seq, SB_HEAD_DIM), lambda b, h, i: (b, h)),
            pl.BlockSpec((seq, SB_HEAD_DIM), lambda b, h, i: (b, h)),
            pl.BlockSpec((tk, tk), lambda b, h, i: (0, 0)),
        ],
        out_specs=pl.BlockSpec((tq, SB_HEAD_DIM), lambda b, h, i: (b * nq + i, h)),
        scratch_shapes=[pltpu.VMEM((seq, SB_HEAD_DIM), BF16),
                        pltpu.VMEM((seq, SB_HEAD_DIM), BF16)],
    )
    return pl.pallas_call(
        _attn_prompt_kernel,
        grid_spec=grid_spec,
        out_shape=jax.ShapeDtypeStruct((batch * seq, d_sb), BF16),
        compiler_params=_cparams(("arbitrary", "arbitrary", "arbitrary")),
        name="attn_prompt",
    )(bias, q, k, v, tri)


def _attn_sample_kernel(pps, pt_ref, qbd_ref, bias_ref, tri_ref, *refs):
    k_refs = refs[:pps]
    v_refs = refs[pps:2 * pps]
    o_ref = refs[2 * pps]
    acc_ref, run_ref = refs[2 * pps + 1:]
    s = pl.program_id(1)
    page = k_refs[0].shape[0]

    @pl.when(s == 0)
    def _():
        acc_ref[...] = jnp.zeros(acc_ref.shape, F32)
        run_ref[...] = jnp.zeros(run_ref.shape, F32)

    kc = jnp.concatenate([r[...].astype(BF16) for r in k_refs], axis=0)
    vc = jnp.concatenate([r[...].astype(BF16) for r in v_refs], axis=0)
    z_all = jnp.dot(kc, qbd_ref[...], preferred_element_type=F32) + bias_ref[...]
    tri = tri_ref[...]
    run = run_ref[...]
    a_parts = []
    for r in range(pps):
        z = z_all[r * page:(r + 1) * page, :]
        hi, lo = _split_bf16(_softplus(z))
        csum = (jnp.dot(tri, hi, preferred_element_type=F32)
                + jnp.dot(tri, lo, preferred_element_type=F32)) + run
        a_parts.append(jnp.exp(z - csum).astype(BF16))
        run = csum[0:1, :]
    run_ref[...] = run
    a_all = jnp.concatenate(a_parts, axis=0)
    acc_ref[...] += lax.dot_general(a_all, vc, (((0,), (0,)), ((), ())), preferred_element_type=F32)

    @pl.when(s == pl.num_programs(1) - 1)
    def _():
        heads = acc_ref.shape[1] // SB_HEAD_DIM
        top = acc_ref[0:heads, :]
        row = lax.broadcasted_iota(jnp.int32, top.shape, 0)
        col = lax.broadcasted_iota(jnp.int32, top.shape, 1)
        lo_c = row * SB_HEAD_DIM
        keep = (col >= lo_c) & (col < lo_c + SB_HEAD_DIM)
        o_ref[...] = jnp.sum(jnp.where(keep, top, 0.0), axis=0, keepdims=True).astype(o_ref.dtype)


def _attn_sample(q, bias, cache_k, cache_v, page_table, layer):
    nb, d_sb = q.shape
    heads = d_sb // SB_HEAD_DIM
    depth, n_pool, page = cache_k.shape[:3]
    n_pages = page_table.shape[1]
    pps = PAGES_PER_STEP
    assert n_pages % pps == 0 and heads <= LANES
    ck = cache_k.reshape(depth * n_pool, page, d_sb)
    cv = cache_v.reshape(depth * n_pool, page, d_sb)
    pt = page_table.reshape(-1)
    r = lax.broadcasted_iota(jnp.int32, (d_sb, LANES), 0) // SB_HEAD_DIM
    c = lax.broadcasted_iota(jnp.int32, (d_sb, LANES), 1)
    qbd = jnp.where((r == c)[None], q[:, :, None], jnp.zeros((), q.dtype))
    bias_row = jnp.zeros((1, LANES), F32).at[0, :heads].set(bias)
    rr = lax.broadcasted_iota(jnp.int32, (page, page), 0)
    cc = lax.broadcasted_iota(jnp.int32, (page, page), 1)
    tri = (cc >= rr).astype(BF16)

    def page_map(rk):
        def index_map(b, s, pt_ref):
            pg = pt_ref[b * n_pages + (n_pages - 1 - (s * pps + rk))]
            return (layer * n_pool + pg, 0, 0)
        return index_map

    page_specs = [pl.BlockSpec((None, page, d_sb), page_map(rk)) for rk in range(pps)]
    grid_spec = pltpu.PrefetchScalarGridSpec(
        num_scalar_prefetch=1,
        grid=(nb, n_pages // pps),
        in_specs=[
            pl.BlockSpec((None, d_sb, LANES), lambda b, s, pt_ref: (b, 0, 0)),
            pl.BlockSpec((1, LANES), lambda b, s, pt_ref: (0, 0)),
            pl.BlockSpec((page, page), lambda b, s, pt_ref: (0, 0)),
        ] + page_specs + page_specs,
        out_specs=pl.BlockSpec((None, 1, d_sb), lambda b, s, pt_ref: (b, 0, 0)),
        scratch_shapes=[pltpu.VMEM((LANES, d_sb), F32), pltpu.VMEM((1, LANES), F32)],
    )
    out = pl.pallas_call(
        functools.partial(_attn_sample_kernel, pps),
        grid_spec=grid_spec,
        out_shape=jax.ShapeDtypeStruct((nb, 1, d_sb), BF16),
        compiler_params=_cparams(("arbitrary", "arbitrary")),
        name="attn_sample",
    )(pt, qbd, bias_row, tri, *([ck] * pps), *([cv] * pps))
    return out.reshape(nb, d_sb)


def _route(logits):
    lane = lax.broadcasted_iota(jnp.int32, logits.shape, 1).astype(F32)
    neg = -jnp.inf
    far = float(LANES)
    is_group = lane < N_GROUPS
    lg = jnp.where(is_group, logits, neg)
    gmax = jnp.max(lg, axis=1, keepdims=True)
    gidx = jnp.min(jnp.where(lg == gmax, lane, far), axis=1, keepdims=True)
    den = jnp.sum(jnp.where(is_group, jnp.exp(logits - gmax), 0.0), axis=1, keepdims=True)
    ggate = 1.0 / den
    first = N_GROUPS + gidx * EXPERTS_PER_GROUP
    le = jnp.where((lane >= first) & (lane < first + EXPERTS_PER_GROUP), logits, neg)
    v1 = jnp.max(le, axis=1, keepdims=True)
    i1 = jnp.min(jnp.where(le == v1, lane, far), axis=1, keepdims=True)
    le2 = jnp.where(lane == i1, neg, le)
    v2 = jnp.max(le2, axis=1, keepdims=True)
    i2 = jnp.min(jnp.where(le2 == v2, lane, far), axis=1, keepdims=True)
    e21 = jnp.exp(v2 - v1)
    p1 = 1.0 / (1.0 + e21)
    p2 = e21 * p1
    return jnp.where(lane == i1 - N_GROUPS, ggate * p1,
                     jnp.where(lane == i2 - N_GROUPS, ggate * p2, 0.0))


def _outproj_kernel(alpha, conf_ref, att_ref, sc_ref, x_ref, w_ref, g_ref, b_ref, wrh_ref, wrl_ref, br_ref,
                    x1_ref, x1b_ref, comb_ref):
    cat = jnp.concatenate([conf_ref[...], att_ref[...], sc_ref[...]], axis=1)
    mix = jnp.dot(cat, w_ref[...], preferred_element_type=F32)
    x1 = _layer_norm(alpha * x_ref[...] + mix, g_ref[...], b_ref[...])
    x1_ref[...] = x1
    hi, lo = _split_bf16(x1)
    x1b_ref[...] = hi
    wrh = wrh_ref[...]
    logits = (jnp.dot(hi, wrh, preferred_element_type=F32)
              + jnp.dot(lo, wrh, preferred_element_type=F32)
              + jnp.dot(hi, wrl_ref[...], preferred_element_type=F32)) + br_ref[...]
    comb_ref[...] = _route(logits)


def _outproj(conf, att, sc, x, w_out_b, layer, alpha, ln_g, ln_b, wr_hi, wr_lo, br):
    m, d_model = x.shape
    tm = min(OUT_TM, m)
    d_mix = w_out_b.shape[1]
    row = lambda i: (i, 0)
    par3 = lambda i: (layer, 0, 0)
    return pl.pallas_call(
        functools.partial(_outproj_kernel, alpha),
        grid=(m // tm,),
        in_specs=[
            pl.BlockSpec((tm, conf.shape[1]), row),
            pl.BlockSpec((tm, att.shape[1]), row),
            pl.BlockSpec((tm, sc.shape[1]), row),
            pl.BlockSpec((tm, d_model), row),
            pl.BlockSpec((None, d_mix, d_model), par3),
            pl.BlockSpec((None, 1, d_model), par3),
            pl.BlockSpec((None, 1, d_model), par3),
            pl.BlockSpec((None, d_model, LANES), par3),
            pl.BlockSpec((None, d_model, LANES), par3),
            pl.BlockSpec((None, 1, LANES), par3),
        ],
        out_specs=[
            pl.BlockSpec((tm, d_model), row),
            pl.BlockSpec((tm, d_model), row),
            pl.BlockSpec((tm, LANES), row),
        ],
        out_shape=[
            jax.ShapeDtypeStruct((m, d_model), F32),
            jax.ShapeDtypeStruct((m, d_model), BF16),
            jax.ShapeDtypeStruct((m, LANES), F32),
        ],
        compiler_params=_cparams(("arbitrary",)),
        name="outproj_router",
    )(conf, att, sc, x, w_out_b, ln_g, ln_b, wr_hi, wr_lo, br)


def _moe_kernel(alpha, xb_ref, x1_ref, comb_ref, wg_ref, wu_ref, wd_ref, g_ref, b_ref, o_ref, acc_ref):
    e = pl.program_id(1)

    @pl.when(e == 0)
    def _():
        acc_ref[...] = jnp.zeros(acc_ref.shape, F32)

    xb = xb_ref[...]
    gate = jnp.dot(xb, wg_ref[...], preferred_element_type=F32)
    up = jnp.dot(xb, wu_ref[...], preferred_element_type=F32)
    comb = comb_ref[...]
    lane = lax.broadcasted_iota(jnp.int32, comb.shape, 1)
    ce = jnp.sum(jnp.where(lane == e, comb, 0.0), axis=1, keepdims=True)
    hid = (gate * jax.nn.sigmoid(gate)) * up * ce
    acc_ref[...] += jnp.dot(hid.astype(BF16), wd_ref[...], preferred_element_type=F32)

    @pl.when(e == pl.num_programs(1) - 1)
    def _():
        o_ref[...] = _layer_norm(alpha * x1_ref[...] + acc_ref[...], g_ref[...], b_ref[...])


def _moe(x1b, x1, comb, wg_b, wu_b, wd_b, layer, alpha, ln_g, ln_b):
    m, d_model = x1.shape
    n_exp, _, d_exp = wg_b.shape[1:]
    tm = min(MOE_TM, m)
    row = lambda i, e: (i, 0)
    par3 = lambda i, e: (layer, 0, 0)
    wmap = lambda i, e: (layer, e, 0, 0)
    return pl.pallas_call(
        functools.partial(_moe_kernel, alpha),
        grid=(m // tm, n_exp),
        in_specs=[
            pl.BlockSpec((tm, d_model), row),
            pl.BlockSpec((tm, d_model), row),
            pl.BlockSpec((tm, LANES), row),
            pl.BlockSpec((None, None, d_model, d_exp), wmap),
            pl.BlockSpec((None, None, d_model, d_exp), wmap),
            pl.BlockSpec((None, None, d_exp, d_model), wmap),
            pl.BlockSpec((None, 1, d_model), par3),
            pl.BlockSpec((None, 1, d_model), par3),
        ],
        out_specs=pl.BlockSpec((tm, d_model), row),
        out_shape=jax.ShapeDtypeStruct((m, d_model), F32),
        scratch_shapes=[pltpu.VMEM((tm, d_model), F32)],
        compiler_params=_cparams(("arbitrary", "arbitrary")),
        name="moe_dense",
    )(x1b, x1, comb, wg_b, wu_b, wd_b, ln_g, ln_b)


def kernel(x_prompt, x_sample, cache_k, cache_v, state_conv, state_sconv, page_table, w_in, conv_w, conv_b, conv_ln_g, conv_ln_b, sconv_w, sb_bias, w_out, ln1_g, ln1_b, router_group_w, router_group_b, router_expert_w, router_expert_b, expert_w_gate, expert_w_up, expert_w_down, ln2_g, ln2_b):
    batch, seq, d_model = x_prompt.shape
    nb = x_sample.shape[0]
    depth = w_in.shape[0]
    d_conf = conv_w.shape[-1]
    d_sc = sconv_w.shape[-1]
    heads = sb_bias.shape[-1]
    d_sb = heads * SB_HEAD_DIM
    alpha = (2 * depth) ** 0.25

    row3 = lambda a: a.reshape(depth, 1, a.shape[-1])
    conv_b3, conv_ln_g3, conv_ln_b3 = row3(conv_b), row3(conv_ln_g), row3(conv_ln_b)
    ln1_g3, ln1_b3, ln2_g3, ln2_b3 = row3(ln1_g), row3(ln1_b), row3(ln2_g), row3(ln2_b)

    n_log = N_GROUPS + N_EXPERTS
    wr = jnp.concatenate(
        [router_group_w, router_expert_w.transpose(0, 2, 1, 3).reshape(depth, d_model, N_EXPERTS)], axis=-1)
    wr = jnp.pad(wr, ((0, 0), (0, 0), (0, LANES - n_log)))
    wr_hi = wr.astype(BF16)
    wr_lo = (wr - wr_hi.astype(F32)).astype(BF16)
    br = jnp.concatenate([router_group_b, router_expert_b.reshape(depth, N_EXPERTS)], axis=-1)
    br = jnp.pad(br, ((0, 0), (0, LANES - n_log))).reshape(depth, 1, LANES)

    w_out_b = w_out.astype(BF16)
    wg_b = expert_w_gate.astype(BF16)
    wu_b = expert_w_up.astype(BF16)
    wd_b = expert_w_down.astype(BF16)

    cs_t = state_conv.transpose(0, 2, 1, 3)
    ss_t = state_sconv.transpose(0, 2, 1, 3)

    yp = x_prompt.reshape(batch * seq, d_model)
    ys = x_sample.reshape(nb, d_model)
    kp, vp, cp, sp, kd, vd, cd, sd = [], [], [], [], [], [], [], []
    for l in range(depth):
        cg, q, k, v, gch = _inproj(yp, w_in, l, d_conf, d_sb, d_sc)
        conf, sc, cst, sst = _mixer_prompt(cg, gch, batch, seq, l, conv_w, conv_b3, conv_ln_g3, conv_ln_b3,
                                           sconv_w)
        att = _attn_prompt(q, k, v, sb_bias[l], batch, seq)
        x1, x1b, comb = _outproj(conf, att, sc, yp, w_out_b, l, alpha, ln1_g3, ln1_b3, wr_hi, wr_lo, br)
        yp = _moe(x1b, x1, comb, wg_b, wu_b, wd_b, l, alpha, ln2_g3, ln2_b3)
        kp.append(k.reshape(batch, seq, heads, SB_HEAD_DIM))
        vp.append(v.reshape(batch, seq, heads, SB_HEAD_DIM))
        cp.append(cst[:, CONV_HIST - (CONF_WIDTH - 1):, :])
        sp.append(sst[:, SC_HIST - (SC_WIDTH - 1):, :])

        cg, q, k, v, gch = _inproj(ys, w_in, l, d_conf, d_sb, d_sc)
        conf, sc, ncs, nss = _mixer_sample(cg, gch, cs_t, ss_t, l, conv_w, conv_b3, conv_ln_g3, conv_ln_b3,
                                           sconv_w)
        att = _attn_sample(q, sb_bias[l], cache_k, cache_v, page_table, l)
        x1, x1b, comb = _outproj(conf, att, sc, ys, w_out_b, l, alpha, ln1_g3, ln1_b3, wr_hi, wr_lo, br)
        ys = _moe(x1b, x1, comb, wg_b, wu_b, wd_b, l, alpha, ln2_g3, ln2_b3)
        kd.append(k.reshape(nb, 1, heads, SB_HEAD_DIM))
        vd.append(v.reshape(nb, 1, heads, SB_HEAD_DIM))
        cd.append(ncs.transpose(1, 0, 2))
        sd.append(nss.transpose(1, 0, 2))

    return (yp.reshape(batch, seq, d_model), ys.reshape(nb, 1, d_model),
            jnp.stack(kp), jnp.stack(vp), jnp.stack(cp), jnp.stack(sp),
            jnp.stack(kd), jnp.stack(vd), jnp.stack(cd), jnp.stack(sd))
```

```python
import functools
import math

import jax
import jax.numpy as jnp
from jax import lax
from jax.experimental import pallas as pl
from jax.experimental.pallas import tpu as pltpu

F32 = jnp.float32
BF16 = jnp.bfloat16

SB_HEAD_DIM = 128
CONF_WIDTH = 31
SC_WIDTH = 3
N_GROUPS = 4
EXPERTS_PER_GROUP = 4
N_EXPERTS = N_GROUPS * EXPERTS_PER_GROUP
LN_EPS = 1e-5
LOG2_E = math.log2(math.e)

LANES = 128
SUBLANES = 8
VMEM_LIMIT_MB = 56

INPROJ_TM = 512
INPROJ_TN = 512
MIX_T = 256
MIX_RC = 32
CONV_HIST = 32
SC_HIST = 8
ATT_TQ = 1024
ATT_TK = 256
ATT_UNROLL = 4
ATT_FLAGS = None
PAGES_PER_STEP = 8
OUT_TM = 256
MOE_TM = 512


def _cparams(sem, flags=None):
    return pltpu.CompilerParams(dimension_semantics=sem, flags=flags,
                                vmem_limit_bytes=VMEM_LIMIT_MB * 1024 * 1024)


def _layer_norm(y, g, b):
    mu = jnp.mean(y, axis=-1, keepdims=True)
    d = y - mu
    var = jnp.mean(d * d, axis=-1, keepdims=True)
    return d * lax.rsqrt(var + LN_EPS) * g + b


def _softplus2(z):
    return jnp.maximum(z, 0.0) + jnp.log2(1.0 + jnp.exp2(-jnp.abs(z)))


def _split_bf16(x):
    hi = x.astype(BF16)
    lo = (x - hi.astype(F32)).astype(BF16)
    return hi, lo


def _inproj_kernel(bounds, scale, x_ref, w_ref, cg_ref, q_ref, k_ref, v_ref, gch_ref, xb_ref):
    j = pl.program_id(1)

    @pl.when(j == 0)
    def _():
        xb_ref[...] = x_ref[...].astype(BF16)

    acc = jnp.dot(xb_ref[...], w_ref[...], preferred_element_type=F32)
    b_cg, b_q, b_k, b_v = bounds

    @pl.when(j < b_cg)
    def _():
        cg_ref[...] = acc

    @pl.when((j >= b_cg) & (j < b_q))
    def _():
        q_ref[...] = (acc * scale).astype(BF16)

    @pl.when((j >= b_q) & (j < b_k))
    def _():
        k_ref[...] = acc

    @pl.when((j >= b_k) & (j < b_v))
    def _():
        v_ref[...] = acc

    @pl.when(j >= b_v)
    def _():
        gch_ref[...] = acc


def _inproj(x, w_in, layer, d_conf, d_sb, d_sc):
    m, d_model = x.shape
    p_in = w_in.shape[-1]
    tm = min(INPROJ_TM, m)
    tn = INPROJ_TN
    n_cg, n_sb, n_gch = 2 * d_conf // tn, d_sb // tn, 3 * d_sc // tn
    bounds = (n_cg, n_cg + n_sb, n_cg + 2 * n_sb, n_cg + 3 * n_sb)
    assert bounds[-1] + n_gch == p_in // tn and m % tm == 0

    def out_map(lo, n):
        return lambda i, j: (i, jnp.clip(j - lo, 0, n - 1))

    kernel = functools.partial(_inproj_kernel, bounds, SB_HEAD_DIM ** -0.5 * LOG2_E)
    return pl.pallas_call(
        kernel,
        grid=(m // tm, p_in // tn),
        in_specs=[
            pl.BlockSpec((tm, d_model), lambda i, j: (i, 0)),
            pl.BlockSpec((None, d_model, tn), lambda i, j: (layer, 0, j)),
        ],
        out_specs=[
            pl.BlockSpec((tm, tn), out_map(0, n_cg)),
            pl.BlockSpec((tm, tn), out_map(bounds[0], n_sb)),
            pl.BlockSpec((tm, tn), out_map(bounds[1], n_sb)),
            pl.BlockSpec((tm, tn), out_map(bounds[2], n_sb)),
            pl.BlockSpec((tm, tn), out_map(bounds[3], n_gch)),
        ],
        out_shape=[
            jax.ShapeDtypeStruct((m, 2 * d_conf), F32),
            jax.ShapeDtypeStruct((m, d_sb), BF16),
            jax.ShapeDtypeStruct((m, d_sb), F32),
            jax.ShapeDtypeStruct((m, d_sb), F32),
            jax.ShapeDtypeStruct((m, 3 * d_sc), F32),
        ],
        scratch_shapes=[pltpu.VMEM((tm, d_model), BF16)],
        compiler_params=_cparams(("arbitrary", "arbitrary")),
        name="inproj",
    )(x, w_in)


def _mixer_prompt_kernel(val_ref, gate_ref, gb_ref, gc_ref, h_ref, cw_ref, cb_ref, lg_ref, lb_ref, sw_ref,
                         conf_ref, sc_ref, cst_ref, sst_ref, ext_ref, ext2_ref):
    t = pl.program_id(1)
    tt, c = val_ref.shape

    @pl.when(t == 0)
    def _():
        ext_ref[0:CONV_HIST, :] = jnp.zeros((CONV_HIST, c), F32)
        ext2_ref[0:SC_HIST, :] = jnp.zeros((SC_HIST, ext2_ref.shape[1]), F32)

    ext_ref[CONV_HIST:CONV_HIST + tt, :] = val_ref[...] * jax.nn.sigmoid(gate_ref[...])
    ext2_ref[SC_HIST:SC_HIST + tt, :] = gc_ref[...] * h_ref[...]

    cb = cb_ref[...]
    lg = lg_ref[...]
    lb = lb_ref[...]
    conv_off = CONV_HIST - (CONF_WIDTH - 1)
    sc_off = SC_HIST - (SC_WIDTH - 1)
    for ch in range(tt // MIX_RC):
        r0 = ch * MIX_RC
        acc = jnp.zeros((MIX_RC, c), F32)
        for w in range(CONF_WIDTH):
            acc = acc + ext_ref[r0 + conv_off + w:r0 + conv_off + w + MIX_RC, :] * cw_ref[w:w + 1, :]
        yn = _layer_norm(acc + cb, lg, lb)
        conf_ref[r0:r0 + MIX_RC, :] = (yn * jax.nn.sigmoid(yn)).astype(conf_ref.dtype)
        s = jnp.zeros((MIX_RC, ext2_ref.shape[1]), F32)
        for w in range(SC_WIDTH):
            s = s + ext2_ref[r0 + sc_off + w:r0 + sc_off + w + MIX_RC, :] * sw_ref[w:w + 1, :]
        sc_ref[r0:r0 + MIX_RC, :] = (gb_ref[r0:r0 + MIX_RC, :] * s).astype(sc_ref.dtype)

    ext_ref[0:CONV_HIST, :] = ext_ref[tt:tt + CONV_HIST, :]
    ext2_ref[0:SC_HIST, :] = ext2_ref[tt:tt + SC_HIST, :]

    @pl.when(t == pl.num_programs(1) - 1)
    def _():
        cst_ref[...] = ext_ref[tt:tt + CONV_HIST, :]
        sst_ref[...] = ext2_ref[tt:tt + SC_HIST, :]


def _mixer_prompt(cg, gch, batch, seq, layer, conv_w, conv_b, conv_ln_g, conv_ln_b, sconv_w):
    d_conf = cg.shape[1] // 2
    d_sc = gch.shape[1] // 3
    tt = MIX_T
    nt = seq // tt
    row = lambda b, t: (b * nt + t, 0)
    col = lambda cidx: (lambda b, t: (b * nt + t, cidx))
    par3 = lambda b, t: (layer, 0, 0)
    return pl.pallas_call(
        _mixer_prompt_kernel,
        grid=(batch, nt),
        in_specs=[
            pl.BlockSpec((tt, d_conf), col(0)),
            pl.BlockSpec((tt, d_conf), col(1)),
            pl.BlockSpec((tt, d_sc), col(0)),
            pl.BlockSpec((tt, d_sc), col(1)),
            pl.BlockSpec((tt, d_sc), col(2)),
            pl.BlockSpec((None, CONF_WIDTH, d_conf), par3),
            pl.BlockSpec((None, 1, d_conf), par3),
            pl.BlockSpec((None, 1, d_conf), par3),
            pl.BlockSpec((None, 1, d_conf), par3),
            pl.BlockSpec((None, SC_WIDTH, d_sc), par3),
        ],
        out_specs=[
            pl.BlockSpec((tt, d_conf), row),
            pl.BlockSpec((tt, d_sc), row),
            pl.BlockSpec((None, CONV_HIST, d_conf), lambda b, t: (b, 0, 0)),
            pl.BlockSpec((None, SC_HIST, d_sc), lambda b, t: (b, 0, 0)),
        ],
        out_shape=[
            jax.ShapeDtypeStruct((batch * seq, d_conf), BF16),
            jax.ShapeDtypeStruct((batch * seq, d_sc), BF16),
            jax.ShapeDtypeStruct((batch, CONV_HIST, d_conf), F32),
            jax.ShapeDtypeStruct((batch, SC_HIST, d_sc), F32),
        ],
        scratch_shapes=[pltpu.VMEM((tt + CONV_HIST, d_conf), F32),
                        pltpu.VMEM((tt + SC_HIST, d_sc), F32)],
        compiler_params=_cparams(("arbitrary", "arbitrary")),
        name="mixer_prompt",
    )(cg, cg, gch, gch, gch, conv_w, conv_b, conv_ln_g, conv_ln_b, sconv_w)


def _mixer_sample_kernel(cg_ref, gch_ref, cs_ref, ss_ref, cw_ref, cb_ref, lg_ref, lb_ref, sw_ref,
                         conf_ref, sc_ref, ncs_ref, nss_ref):
    c = cs_ref.shape[-1]
    c2 = ss_ref.shape[-1]
    hist = CONF_WIDTH - 1
    glu = cg_ref[:, 0:c] * jax.nn.sigmoid(cg_ref[:, c:2 * c])
    acc = glu * cw_ref[hist:hist + 1, :]
    for w in range(hist):
        acc = acc + cs_ref[w] * cw_ref[w:w + 1, :]
    yn = _layer_norm(acc + cb_ref[...], lg_ref[...], lb_ref[...])
    conf_ref[...] = (yn * jax.nn.sigmoid(yn)).astype(conf_ref.dtype)

    p = gch_ref[:, c2:2 * c2] * gch_ref[:, 2 * c2:3 * c2]
    s = p * sw_ref[SC_WIDTH - 1:SC_WIDTH, :]
    for w in range(SC_WIDTH - 1):
        s = s + ss_ref[w] * sw_ref[w:w + 1, :]
    sc_ref[...] = (gch_ref[:, 0:c2] * s).astype(sc_ref.dtype)

    for w in range(hist - 1):
        ncs_ref[w] = cs_ref[w + 1]
    ncs_ref[hist - 1] = glu
    for w in range(SC_WIDTH - 2):
        nss_ref[w] = ss_ref[w + 1]
    nss_ref[SC_WIDTH - 2] = p


def _mixer_sample(cg, gch, cs_t, ss_t, layer, conv_w, conv_b, conv_ln_g, conv_ln_b, sconv_w):
    nb = cg.shape[0]
    d_conf = cg.shape[1] // 2
    d_sc = gch.shape[1] // 3
    full2 = lambda i: (0, 0)
    par3 = lambda i: (layer, 0, 0)
    st4 = lambda i: (layer, 0, 0, 0)
    return pl.pallas_call(
        _mixer_sample_kernel,
        grid=(1,),
        in_specs=[
            pl.BlockSpec(cg.shape, full2),
            pl.BlockSpec(gch.shape, full2),
            pl.BlockSpec((None,) + cs_t.shape[1:], st4),
            pl.BlockSpec((None,) + ss_t.shape[1:], st4),
            pl.BlockSpec((None, CONF_WIDTH, d_conf), par3),
            pl.BlockSpec((None, 1, d_conf), par3),
            pl.BlockSpec((None, 1, d_conf), par3),
            pl.BlockSpec((None, 1, d_conf), par3),
            pl.BlockSpec((None, SC_WIDTH, d_sc), par3),
        ],
        out_specs=[
            pl.BlockSpec((nb, d_conf), full2),
            pl.BlockSpec((nb, d_sc), full2),
            pl.BlockSpec(cs_t.shape[1:], lambda i: (0, 0, 0)),
            pl.BlockSpec(ss_t.shape[1:], lambda i: (0, 0, 0)),
        ],
        out_shape=[
            jax.ShapeDtypeStruct((nb, d_conf), BF16),
            jax.ShapeDtypeStruct((nb, d_sc), BF16),
            jax.ShapeDtypeStruct(cs_t.shape[1:], F32),
            jax.ShapeDtypeStruct(ss_t.shape[1:], F32),
        ],
        compiler_params=_cparams(("arbitrary",)),
        name="mixer_sample",
    )(cg, gch, cs_t, ss_t, conv_w, conv_b, conv_ln_g, conv_ln_b, sconv_w)


def _attn_prompt_kernel(bias_ref, q_ref, k_ref, v_ref, tri_ref, o_ref, kb_ref, vt_ref):
    h = pl.program_id(1)
    i = pl.program_id(2)
    tq = q_ref.shape[0]
    tk = tri_ref.shape[0]
    ratio = tq // tk

    @pl.when(i == 0)
    def _():
        kb_ref[...] = k_ref[...].astype(BF16)
        for kb in range(vt_ref.shape[0]):
            vt_ref[kb] = v_ref[kb * tk:(kb + 1) * tk, :].T.astype(BF16)

    bias = bias_ref[h]
    q = q_ref[...]
    tri2 = tri_ref[...]

    def blocks(kbs, q0s, acc, run, masked):
        zs, keeps, splits, csums, avs = [], [], [], [], []
        for kb, q0 in zip(kbs, q0s):
            start = pl.multiple_of(kb * tk, tk)
            kblk = kb_ref[pl.ds(start, tk), :]
            zs.append(lax.dot_general(kblk, q[q0:, :], (((1,), (1,)), ((), ())),
                                      preferred_element_type=F32) + bias)
        for kb, q0, z in zip(kbs, q0s, zs):
            sp = _softplus2(z)
            if masked:
                keyp = kb * tk + lax.broadcasted_iota(jnp.int32, z.shape, 0)
                qryp = i * tq + q0 + lax.broadcasted_iota(jnp.int32, z.shape, 1)
                keeps.append(keyp < qryp)
                sp = jnp.where(keeps[-1], sp, 0.0)
            hi, lo = _split_bf16(sp)
            splits.append(jnp.concatenate([hi, lo], axis=0))
        for w in splits:
            csums.append(jnp.dot(tri2, w, preferred_element_type=F32))
        def from_q0(full, q0, tail):
            return tail if q0 == 0 else jnp.concatenate([full[:, :q0], tail], axis=1)

        for j, q0 in enumerate(q0s):
            csum = csums[j] + run[:, q0:]
            run = from_q0(run, q0, csum[0:1, :])
            a = jnp.exp2(zs[j] - csum)
            if masked:
                a = jnp.where(keeps[j], a, 0.0)
            avs.append(jnp.dot(vt_ref[kbs[j]], a.astype(BF16), preferred_element_type=F32))
        for q0, av in zip(q0s, avs):
            acc = from_q0(acc, q0, acc[:, q0:] + av)
        return acc, run

    acc = jnp.zeros((q_ref.shape[1], tq), F32)
    run = jnp.zeros((1, tq), F32)
    diag = list(reversed(range(ratio)))
    acc, run = blocks([i * ratio + c for c in diag], [c * tk for c in diag], acc, run, True)

    def body(jj, carry):
        first = i * ratio - 1 - jj * ATT_UNROLL
        return blocks([first - d for d in range(ATT_UNROLL)], [0] * ATT_UNROLL, carry[0], carry[1], False)

    acc, run = lax.fori_loop(0, i * ratio // ATT_UNROLL, body, (acc, run))
    o_ref[...] = acc.T.astype(o_ref.dtype)


def _attn_prompt(q, k, v, bias, batch, seq):
    d_sb = q.shape[1]
    heads = d_sb // SB_HEAD_DIM
    tq, tk = ATT_TQ, ATT_TK
    nq = seq // tq
    assert seq % tq == 0 and (tq // tk) % ATT_UNROLL == 0
    r = lax.broadcasted_iota(jnp.int32, (tk, tk), 0)
    c = lax.broadcasted_iota(jnp.int32, (tk, tk), 1)
    upper = (c >= r).astype(BF16)
    tri = jnp.concatenate([upper, upper], axis=1)
    grid_spec = pltpu.PrefetchScalarGridSpec(
        num_scalar_prefetch=0,
        grid=(batch, heads, nq),
        in_specs=[
            pl.BlockSpec(memory_space=pltpu.SMEM),
            pl.BlockSpec((tq, SB_HEAD_DIM), lambda b, h, i: (b * nq + i, h)),
            pl.BlockSpec((seq, SB_HEAD_DIM), lambda b, h, i: (b, h)),
            pl.BlockSpec((seq, SB_HEAD_DIM), lambda b, h, i: (b, h)),
            pl.BlockSpec((tk, 2 * tk), lambda b, h, i: (0, 0)),
        ],
        out_specs=pl.BlockSpec((tq, SB_HEAD_DIM), lambda b, h, i: (b * nq + i, h)),
        scratch_shapes=[pltpu.VMEM((seq, SB_HEAD_DIM), BF16),
                        pltpu.VMEM((seq // tk, SB_HEAD_DIM, tk), BF16)],
    )
    return pl.pallas_call(
        _attn_prompt_kernel,
        grid_spec=grid_spec,
        out_shape=jax.ShapeDtypeStruct((batch * seq, d_sb), BF16),
        compiler_params=_cparams(("arbitrary", "arbitrary", "arbitrary"), ATT_FLAGS),
        name="attn_prompt",
    )(bias, q, k, v, tri)


def _attn_sample_kernel(pps, pt_ref, qbd_ref, bias_ref, tri_ref, *refs):
    k_refs = refs[:pps]
    v_refs = refs[pps:2 * pps]
    o_ref = refs[2 * pps]
    acc_ref, run_ref = refs[2 * pps + 1:]
    s = pl.program_id(1)
    heads = acc_ref.shape[1] // k_refs[0].shape[1]
    page = k_refs[0].shape[0] // heads

    @pl.when(s == 0)
    def _():
        acc_ref[...] = jnp.zeros(acc_ref.shape, F32)
        run_ref[...] = jnp.zeros(run_ref.shape, F32)

    def flat_page(ref):
        return jnp.concatenate([ref[pl.ds(h, page, stride=heads), :] for h in range(heads)],
                               axis=1).astype(BF16)

    kc = jnp.concatenate([flat_page(r) for r in k_refs], axis=0)
    vc = jnp.concatenate([flat_page(r) for r in v_refs], axis=0)
    z_all = jnp.dot(kc, qbd_ref[...], preferred_element_type=F32) + bias_ref[...]
    tri = tri_ref[...]
    run = run_ref[...]
    a_parts = []
    for r in range(pps):
        z = z_all[r * page:(r + 1) * page, :]
        hi, lo = _split_bf16(_softplus2(z))
        csum = (jnp.dot(tri, hi, preferred_element_type=F32)
                + jnp.dot(tri, lo, preferred_element_type=F32)) + run
        a_parts.append(jnp.exp2(z - csum).astype(BF16))
        run = csum[0:1, :]
    run_ref[...] = run
    a_all = jnp.concatenate(a_parts, axis=0)
    acc_ref[...] += lax.dot_general(a_all, vc, (((0,), (0,)), ((), ())), preferred_element_type=F32)

    @pl.when(s == pl.num_programs(1) - 1)
    def _():
        top = acc_ref[0:heads, :]
        row = lax.broadcasted_iota(jnp.int32, top.shape, 0)
        col = lax.broadcasted_iota(jnp.int32, top.shape, 1)
        lo_c = row * SB_HEAD_DIM
        keep = (col >= lo_c) & (col < lo_c + SB_HEAD_DIM)
        o_ref[...] = jnp.sum(jnp.where(keep, top, 0.0), axis=0, keepdims=True).astype(o_ref.dtype)


def _attn_sample(q, bias, cache_k, cache_v, page_table, layer):
    nb, d_sb = q.shape
    heads = d_sb // SB_HEAD_DIM
    depth, n_pool, page = cache_k.shape[:3]
    n_pages = page_table.shape[1]
    pps = PAGES_PER_STEP
    assert n_pages % pps == 0 and heads == SUBLANES
    pt = page_table.reshape(-1)
    ck = cache_k.reshape(depth, n_pool, page * heads, SB_HEAD_DIM)
    cv = cache_v.reshape(depth, n_pool, page * heads, SB_HEAD_DIM)
    r = lax.broadcasted_iota(jnp.int32, (d_sb, LANES), 0) // SB_HEAD_DIM
    c = lax.broadcasted_iota(jnp.int32, (d_sb, LANES), 1)
    qbd = jnp.where((r == c)[None], q[:, :, None], jnp.zeros((), q.dtype))
    bias_row = jnp.zeros((1, LANES), F32).at[0, :heads].set(bias)
    rr = lax.broadcasted_iota(jnp.int32, (page, page), 0)
    cc = lax.broadcasted_iota(jnp.int32, (page, page), 1)
    tri = (cc >= rr).astype(BF16)

    def page_map(rk):
        def index_map(b, s, pt_ref):
            pg = pt_ref[b * n_pages + (n_pages - 1 - (s * pps + rk))]
            return (layer, pg, 0, 0)
        return index_map

    page_specs = [pl.BlockSpec((None, None, page * heads, SB_HEAD_DIM), page_map(rk)) for rk in range(pps)]
    grid_spec = pltpu.PrefetchScalarGridSpec(
        num_scalar_prefetch=1,
        grid=(nb, n_pages // pps),
        in_specs=[
            pl.BlockSpec((None, d_sb, LANES), lambda b, s, pt_ref: (b, 0, 0)),
            pl.BlockSpec((1, LANES), lambda b, s, pt_ref: (0, 0)),
            pl.BlockSpec((page, page), lambda b, s, pt_ref: (0, 0)),
        ] + page_specs + page_specs,
        out_specs=pl.BlockSpec((None, 1, d_sb), lambda b, s, pt_ref: (b, 0, 0)),
        scratch_shapes=[pltpu.VMEM((LANES, d_sb), F32), pltpu.VMEM((1, LANES), F32)],
    )
    out = pl.pallas_call(
        functools.partial(_attn_sample_kernel, pps),
        grid_spec=grid_spec,
        out_shape=jax.ShapeDtypeStruct((nb, 1, d_sb), BF16),
        compiler_params=_cparams(("arbitrary", "arbitrary")),
        name="attn_sample",
    )(pt, qbd, bias_row, tri, *([ck] * pps), *([cv] * pps))
    return out.reshape(nb, d_sb)


def _route(logits):
    lane = lax.broadcasted_iota(jnp.int32, logits.shape, 1).astype(F32)
    neg = -jnp.inf
    far = float(LANES)
    is_group = lane < N_GROUPS
    lg = jnp.where(is_group, logits, neg)
    gmax = jnp.max(lg, axis=1, keepdims=True)
    gidx = jnp.min(jnp.where(lg == gmax, lane, far), axis=1, keepdims=True)
    den = jnp.sum(jnp.where(is_group, jnp.exp(logits - gmax), 0.0), axis=1, keepdims=True)
    ggate = 1.0 / den
    first = N_GROUPS + gidx * EXPERTS_PER_GROUP
    le = jnp.where((lane >= first) & (lane < first + EXPERTS_PER_GROUP), logits, neg)
    v1 = jnp.max(le, axis=1, keepdims=True)
    i1 = jnp.min(jnp.where(le == v1, lane, far), axis=1, keepdims=True)
    le2 = jnp.where(lane == i1, neg, le)
    v2 = jnp.max(le2, axis=1, keepdims=True)
    i2 = jnp.min(jnp.where(le2 == v2, lane, far), axis=1, keepdims=True)
    e21 = jnp.exp(v2 - v1)
    p1 = 1.0 / (1.0 + e21)
    p2 = e21 * p1
    return jnp.where(lane == i1 - N_GROUPS, ggate * p1,
                     jnp.where(lane == i2 - N_GROUPS, ggate * p2, 0.0))


def _outproj_kernel(alpha, conf_ref, att_ref, sc_ref, x_ref, w_ref, g_ref, b_ref, wrh_ref, wrl_ref, br_ref,
                    x1_ref, x1b_ref, comb_ref):
    cat = jnp.concatenate([conf_ref[...], att_ref[...], sc_ref[...]], axis=1)
    mix = jnp.dot(cat, w_ref[...], preferred_element_type=F32)
    x1 = _layer_norm(alpha * x_ref[...] + mix, g_ref[...], b_ref[...])
    x1_ref[...] = x1
    hi, lo = _split_bf16(x1)
    x1b_ref[...] = hi
    wrh = wrh_ref[...]
    logits = (jnp.dot(hi, wrh, preferred_element_type=F32)
              + jnp.dot(lo, wrh, preferred_element_type=F32)
              + jnp.dot(hi, wrl_ref[...], preferred_element_type=F32)) + br_ref[...]
    comb_ref[...] = _route(logits)


def _outproj(conf, att, sc, x, w_out_b, layer, alpha, ln_g, ln_b, wr_hi, wr_lo, br):
    m, d_model = x.shape
    tm = min(OUT_TM, m)
    d_mix = w_out_b.shape[1]
    row = lambda i: (i, 0)
    par3 = lambda i: (layer, 0, 0)
    return pl.pallas_call(
        functools.partial(_outproj_kernel, alpha),
        grid=(m // tm,),
        in_specs=[
            pl.BlockSpec((tm, conf.shape[1]), row),
            pl.BlockSpec((tm, att.shape[1]), row),
            pl.BlockSpec((tm, sc.shape[1]), row),
            pl.BlockSpec((tm, d_model), row),
            pl.BlockSpec((None, d_mix, d_model), par3),
            pl.BlockSpec((None, 1, d_model), par3),
            pl.BlockSpec((None, 1, d_model), par3),
            pl.BlockSpec((None, d_model, LANES), par3),
            pl.BlockSpec((None, d_model, LANES), par3),
            pl.BlockSpec((None, 1, LANES), par3),
        ],
        out_specs=[
            pl.BlockSpec((tm, d_model), row),
            pl.BlockSpec((tm, d_model), row),
            pl.BlockSpec((tm, LANES), row),
        ],
        out_shape=[
            jax.ShapeDtypeStruct((m, d_model), F32),
            jax.ShapeDtypeStruct((m, d_model), BF16),
            jax.ShapeDtypeStruct((m, LANES), F32),
        ],
        compiler_params=_cparams(("arbitrary",)),
        name="outproj_router",
    )(conf, att, sc, x, w_out_b, ln_g, ln_b, wr_hi, wr_lo, br)


def _moe_kernel(alpha, xb_ref, x1_ref, comb_ref, wg_ref, wu_ref, wd_ref, g_ref, b_ref, o_ref, acc_ref):
    e = pl.program_id(1)

    @pl.when(e == 0)
    def _():
        acc_ref[...] = jnp.zeros(acc_ref.shape, F32)

    xb = xb_ref[...]
    gate = jnp.dot(xb, wg_ref[...], preferred_element_type=F32)
    up = jnp.dot(xb, wu_ref[...], preferred_element_type=F32)
    comb = comb_ref[...]
    lane = lax.broadcasted_iota(jnp.int32, comb.shape, 1)
    ce = jnp.sum(jnp.where(lane == e, comb, 0.0), axis=1, keepdims=True)
    hid = (gate * jax.nn.sigmoid(gate)) * up * ce
    acc_ref[...] += jnp.dot(hid.astype(BF16), wd_ref[...], preferred_element_type=F32)

    @pl.when(e == pl.num_programs(1) - 1)
    def _():
        o_ref[...] = _layer_norm(alpha * x1_ref[...] + acc_ref[...], g_ref[...], b_ref[...])


def _moe(x1b, x1, comb, wg_b, wu_b, wd_b, layer, alpha, ln_g, ln_b):
    m, d_model = x1.shape
    n_exp, _, d_exp = wg_b.shape[1:]
    tm = min(MOE_TM, m)
    row = lambda i, e: (i, 0)
    par3 = lambda i, e: (layer, 0, 0)
    wmap = lambda i, e: (layer, e, 0, 0)
    return pl.pallas_call(
        functools.partial(_moe_kernel, alpha),
        grid=(m // tm, n_exp),
        in_specs=[
            pl.BlockSpec((tm, d_model), row),
            pl.BlockSpec((tm, d_model), row),
            pl.BlockSpec((tm, LANES), row),
            pl.BlockSpec((None, None, d_model, d_exp), wmap),
            pl.BlockSpec((None, None, d_model, d_exp), wmap),
            pl.BlockSpec((None, None, d_exp, d_model), wmap),
            pl.BlockSpec((None, 1, d_model), par3),
            pl.BlockSpec((None, 1, d_model), par3),
        ],
        out_specs=pl.BlockSpec((tm, d_model), row),
        out_shape=jax.ShapeDtypeStruct((m, d_model), F32),
        scratch_shapes=[pltpu.VMEM((tm, d_model), F32)],
        compiler_params=_cparams(("arbitrary", "arbitrary")),
        name="moe_dense",
    )(x1b, x1, comb, wg_b, wu_b, wd_b, ln_g, ln_b)


def kernel(x_prompt, x_sample, cache_k, cache_v, state_conv, state_sconv, page_table, w_in, conv_w, conv_b, conv_ln_g, conv_ln_b, sconv_w, sb_bias, w_out, ln1_g, ln1_b, router_group_w, router_group_b, router_expert_w, router_expert_b, expert_w_gate, expert_w_up, expert_w_down, ln2_g, ln2_b):
    batch, seq, d_model = x_prompt.shape
    nb = x_sample.shape[0]
    depth = w_in.shape[0]
    d_conf = conv_w.shape[-1]
    d_sc = sconv_w.shape[-1]
    heads = sb_bias.shape[-1]
    d_sb = heads * SB_HEAD_DIM
    alpha = (2 * depth) ** 0.25

    row3 = lambda a: a.reshape(depth, 1, a.shape[-1])
    conv_b3, conv_ln_g3, conv_ln_b3 = row3(conv_b), row3(conv_ln_g), row3(conv_ln_b)
    ln1_g3, ln1_b3, ln2_g3, ln2_b3 = row3(ln1_g), row3(ln1_b), row3(ln2_g), row3(ln2_b)

    n_log = N_GROUPS + N_EXPERTS
    wr = jnp.concatenate(
        [router_group_w, router_expert_w.transpose(0, 2, 1, 3).reshape(depth, d_model, N_EXPERTS)], axis=-1)
    wr = jnp.pad(wr, ((0, 0), (0, 0), (0, LANES - n_log)))
    wr_hi = wr.astype(BF16)
    wr_lo = (wr - wr_hi.astype(F32)).astype(BF16)
    br = jnp.concatenate([router_group_b, router_expert_b.reshape(depth, N_EXPERTS)], axis=-1)
    br = jnp.pad(br, ((0, 0), (0, LANES - n_log))).reshape(depth, 1, LANES)

    w_in_b = w_in.astype(BF16)
    w_out_b = w_out.astype(BF16)
    wg_b = expert_w_gate.astype(BF16)
    wu_b = expert_w_up.astype(BF16)
    wd_b = expert_w_down.astype(BF16)

    cs_t = state_conv.transpose(0, 2, 1, 3)
    ss_t = state_sconv.transpose(0, 2, 1, 3)

    yp = x_prompt.reshape(batch * seq, d_model)
    ys = x_sample.reshape(nb, d_model)
    kp, vp, cp, sp, kd, vd, cd, sd = [], [], [], [], [], [], [], []
    for l in range(depth):
        cg, q, k, v, gch = _inproj(yp, w_in_b, l, d_conf, d_sb, d_sc)
        conf, sc, cst, sst = _mixer_prompt(cg, gch, batch, seq, l, conv_w, conv_b3, conv_ln_g3, conv_ln_b3,
                                           sconv_w)
        att = _attn_prompt(q, k, v, sb_bias[l] * LOG2_E, batch, seq)
        x1, x1b, comb = _outproj(conf, att, sc, yp, w_out_b, l, alpha, ln1_g3, ln1_b3, wr_hi, wr_lo, br)
        yp = _moe(x1b, x1, comb, wg_b, wu_b, wd_b, l, alpha, ln2_g3, ln2_b3)
        kp.append(k.reshape(batch, seq, heads, SB_HEAD_DIM))
        vp.append(v.reshape(batch, seq, heads, SB_HEAD_DIM))
        cp.append(cst[:, CONV_HIST - (CONF_WIDTH - 1):, :])
        sp.append(sst[:, SC_HIST - (SC_WIDTH - 1):, :])

        cg, q, k, v, gch = _inproj(ys, w_in_b, l, d_conf, d_sb, d_sc)
        conf, sc, ncs, nss = _mixer_sample(cg, gch, cs_t, ss_t, l, conv_w, conv_b3, conv_ln_g3, conv_ln_b3,
                                           sconv_w)
        att = _attn_sample(q, sb_bias[l] * LOG2_E, cache_k, cache_v, page_table, l)
        x1, x1b, comb = _outproj(conf, att, sc, ys, w_out_b, l, alpha, ln1_g3, ln1_b3, wr_hi, wr_lo, br)
        ys = _moe(x1b, x1, comb, wg_b, wu_b, wd_b, l, alpha, ln2_g3, ln2_b3)
        kd.append(k.reshape(nb, 1, heads, SB_HEAD_DIM))
        vd.append(v.reshape(nb, 1, heads, SB_HEAD_DIM))
        cd.append(ncs.transpose(1, 0, 2))
        sd.append(nss.transpose(1, 0, 2))

    return (yp.reshape(batch, seq, d_model), ys.reshape(nb, 1, d_model),
            jnp.stack(kp), jnp.stack(vp), jnp.stack(cp), jnp.stack(sp),
            jnp.stack(kd), jnp.stack(vd), jnp.stack(cd), jnp.stack(sd))
```

```python
import functools
import math

import jax
import jax.numpy as jnp
from jax import lax
from jax.experimental import pallas as pl
from jax.experimental.pallas import tpu as pltpu

F32 = jnp.float32
BF16 = jnp.bfloat16

SB_HEAD_DIM = 128
CONF_WIDTH = 31
SC_WIDTH = 3
N_GROUPS = 4
EXPERTS_PER_GROUP = 4
N_EXPERTS = N_GROUPS * EXPERTS_PER_GROUP
LN_EPS = 1e-5
LOG2_E = math.log2(math.e)

LANES = 128
SUBLANES = 8
VMEM_LIMIT_MB = 56

INPROJ_TM = 512
INPROJ_TN = 512
MIX_T = 256
MIX_RC = 32
CONV_HIST = 32
SC_HIST = 8
ATT_TQ = 1024
ATT_TK = 256
ATT_UNROLL = 4
ATT_FLAGS = None
PAGES_PER_STEP = 8
OUT_TM = 256
MOE_TM = 512
DISP_TM = 256
ROW_ALIGN = 16
SORT_ROWS = 384
EXP_TM = 1024
EXP_SUB = 256
EXP_HC = 256


def _cparams(sem, flags=None):
    return pltpu.CompilerParams(dimension_semantics=sem, flags=flags,
                                vmem_limit_bytes=VMEM_LIMIT_MB * 1024 * 1024)


def _layer_norm(y, g, b):
    mu = jnp.mean(y, axis=-1, keepdims=True)
    d = y - mu
    var = jnp.mean(d * d, axis=-1, keepdims=True)
    return d * lax.rsqrt(var + LN_EPS) * g + b


def _softplus2(z):
    return jnp.maximum(z, 0.0) + jnp.log2(1.0 + jnp.exp2(-jnp.abs(z)))


def _split_bf16(x):
    hi = x.astype(BF16)
    lo = (x - hi.astype(F32)).astype(BF16)
    return hi, lo


def _inproj_kernel(bounds, scale, x_ref, w_ref, cg_ref, q_ref, k_ref, v_ref, gch_ref, xb_ref):
    j = pl.program_id(1)

    @pl.when(j == 0)
    def _():
        xb_ref[...] = x_ref[...].astype(BF16)

    acc = jnp.dot(xb_ref[...], w_ref[...], preferred_element_type=F32)
    b_cg, b_q, b_k, b_v = bounds

    @pl.when(j < b_cg)
    def _():
        cg_ref[...] = acc

    @pl.when((j >= b_cg) & (j < b_q))
    def _():
        q_ref[...] = (acc * scale).astype(BF16)

    @pl.when((j >= b_q) & (j < b_k))
    def _():
        k_ref[...] = acc

    @pl.when((j >= b_k) & (j < b_v))
    def _():
        v_ref[...] = acc

    @pl.when(j >= b_v)
    def _():
        gch_ref[...] = acc


def _inproj(x, w_in, layer, d_conf, d_sb, d_sc):
    m, d_model = x.shape
    p_in = w_in.shape[-1]
    tm = min(INPROJ_TM, m)
    tn = INPROJ_TN
    n_cg, n_sb, n_gch = 2 * d_conf // tn, d_sb // tn, 3 * d_sc // tn
    bounds = (n_cg, n_cg + n_sb, n_cg + 2 * n_sb, n_cg + 3 * n_sb)
    assert bounds[-1] + n_gch == p_in // tn and m % tm == 0

    def out_map(lo, n):
        return lambda i, j: (i, jnp.clip(j - lo, 0, n - 1))

    kernel = functools.partial(_inproj_kernel, bounds, SB_HEAD_DIM ** -0.5 * LOG2_E)
    return pl.pallas_call(
        kernel,
        grid=(m // tm, p_in // tn),
        in_specs=[
            pl.BlockSpec((tm, d_model), lambda i, j: (i, 0)),
            pl.BlockSpec((None, d_model, tn), lambda i, j: (layer, 0, j)),
        ],
        out_specs=[
            pl.BlockSpec((tm, tn), out_map(0, n_cg)),
            pl.BlockSpec((tm, tn), out_map(bounds[0], n_sb)),
            pl.BlockSpec((tm, tn), out_map(bounds[1], n_sb)),
            pl.BlockSpec((tm, tn), out_map(bounds[2], n_sb)),
            pl.BlockSpec((tm, tn), out_map(bounds[3], n_gch)),
        ],
        out_shape=[
            jax.ShapeDtypeStruct((m, 2 * d_conf), F32),
            jax.ShapeDtypeStruct((m, d_sb), BF16),
            jax.ShapeDtypeStruct((m, d_sb), F32),
            jax.ShapeDtypeStruct((m, d_sb), F32),
            jax.ShapeDtypeStruct((m, 3 * d_sc), F32),
        ],
        scratch_shapes=[pltpu.VMEM((tm, d_model), BF16)],
        compiler_params=_cparams(("arbitrary", "arbitrary")),
        name="inproj",
    )(x, w_in)


def _mixer_prompt_kernel(val_ref, gate_ref, gb_ref, gc_ref, h_ref, cw_ref, cb_ref, lg_ref, lb_ref, sw_ref,
                         conf_ref, sc_ref, cst_ref, sst_ref, ext_ref, ext2_ref):
    t = pl.program_id(1)
    tt, c = val_ref.shape

    @pl.when(t == 0)
    def _():
        ext_ref[0:CONV_HIST, :] = jnp.zeros((CONV_HIST, c), F32)
        ext2_ref[0:SC_HIST, :] = jnp.zeros((SC_HIST, ext2_ref.shape[1]), F32)

    ext_ref[CONV_HIST:CONV_HIST + tt, :] = val_ref[...] * jax.nn.sigmoid(gate_ref[...])
    ext2_ref[SC_HIST:SC_HIST + tt, :] = gc_ref[...] * h_ref[...]

    cb = cb_ref[...]
    lg = lg_ref[...]
    lb = lb_ref[...]
    conv_off = CONV_HIST - (CONF_WIDTH - 1)
    sc_off = SC_HIST - (SC_WIDTH - 1)
    for ch in range(tt // MIX_RC):
        r0 = ch * MIX_RC
        acc = jnp.zeros((MIX_RC, c), F32)
        for w in range(CONF_WIDTH):
            acc = acc + ext_ref[r0 + conv_off + w:r0 + conv_off + w + MIX_RC, :] * cw_ref[w:w + 1, :]
        yn = _layer_norm(acc + cb, lg, lb)
        conf_ref[r0:r0 + MIX_RC, :] = (yn * jax.nn.sigmoid(yn)).astype(conf_ref.dtype)
        s = jnp.zeros((MIX_RC, ext2_ref.shape[1]), F32)
        for w in range(SC_WIDTH):
            s = s + ext2_ref[r0 + sc_off + w:r0 + sc_off + w + MIX_RC, :] * sw_ref[w:w + 1, :]
        sc_ref[r0:r0 + MIX_RC, :] = (gb_ref[r0:r0 + MIX_RC, :] * s).astype(sc_ref.dtype)

    ext_ref[0:CONV_HIST, :] = ext_ref[tt:tt + CONV_HIST, :]
    ext2_ref[0:SC_HIST, :] = ext2_ref[tt:tt + SC_HIST, :]

    @pl.when(t == pl.num_programs(1) - 1)
    def _():
        cst_ref[...] = ext_ref[tt:tt + CONV_HIST, :]
        sst_ref[...] = ext2_ref[tt:tt + SC_HIST, :]


def _mixer_prompt(cg, gch, batch, seq, layer, conv_w, conv_b, conv_ln_g, conv_ln_b, sconv_w):
    d_conf = cg.shape[1] // 2
    d_sc = gch.shape[1] // 3
    tt = MIX_T
    nt = seq // tt
    row = lambda b, t: (b * nt + t, 0)
    col = lambda cidx: (lambda b, t: (b * nt + t, cidx))
    par3 = lambda b, t: (layer, 0, 0)
    return pl.pallas_call(
        _mixer_prompt_kernel,
        grid=(batch, nt),
        in_specs=[
            pl.BlockSpec((tt, d_conf), col(0)),
            pl.BlockSpec((tt, d_conf), col(1)),
            pl.BlockSpec((tt, d_sc), col(0)),
            pl.BlockSpec((tt, d_sc), col(1)),
            pl.BlockSpec((tt, d_sc), col(2)),
            pl.BlockSpec((None, CONF_WIDTH, d_conf), par3),
            pl.BlockSpec((None, 1, d_conf), par3),
            pl.BlockSpec((None, 1, d_conf), par3),
            pl.BlockSpec((None, 1, d_conf), par3),
            pl.BlockSpec((None, SC_WIDTH, d_sc), par3),
        ],
        out_specs=[
            pl.BlockSpec((tt, d_conf), row),
            pl.BlockSpec((tt, d_sc), row),
            pl.BlockSpec((None, CONV_HIST, d_conf), lambda b, t: (b, 0, 0)),
            pl.BlockSpec((None, SC_HIST, d_sc), lambda b, t: (b, 0, 0)),
        ],
        out_shape=[
            jax.ShapeDtypeStruct((batch * seq, d_conf), BF16),
            jax.ShapeDtypeStruct((batch * seq, d_sc), BF16),
            jax.ShapeDtypeStruct((batch, CONV_HIST, d_conf), F32),
            jax.ShapeDtypeStruct((batch, SC_HIST, d_sc), F32),
        ],
        scratch_shapes=[pltpu.VMEM((tt + CONV_HIST, d_conf), F32),
                        pltpu.VMEM((tt + SC_HIST, d_sc), F32)],
        compiler_params=_cparams(("arbitrary", "arbitrary")),
        name="mixer_prompt",
    )(cg, cg, gch, gch, gch, conv_w, conv_b, conv_ln_g, conv_ln_b, sconv_w)


def _mixer_sample_kernel(cg_ref, gch_ref, cs_ref, ss_ref, cw_ref, cb_ref, lg_ref, lb_ref, sw_ref,
                         conf_ref, sc_ref, ncs_ref, nss_ref):
    c = cs_ref.shape[-1]
    c2 = ss_ref.shape[-1]
    hist = CONF_WIDTH - 1
    glu = cg_ref[:, 0:c] * jax.nn.sigmoid(cg_ref[:, c:2 * c])
    acc = glu * cw_ref[hist:hist + 1, :]
    for w in range(hist):
        acc = acc + cs_ref[w] * cw_ref[w:w + 1, :]
    yn = _layer_norm(acc + cb_ref[...], lg_ref[...], lb_ref[...])
    conf_ref[...] = (yn * jax.nn.sigmoid(yn)).astype(conf_ref.dtype)

    p = gch_ref[:, c2:2 * c2] * gch_ref[:, 2 * c2:3 * c2]
    s = p * sw_ref[SC_WIDTH - 1:SC_WIDTH, :]
    for w in range(SC_WIDTH - 1):
        s = s + ss_ref[w] * sw_ref[w:w + 1, :]
    sc_ref[...] = (gch_ref[:, 0:c2] * s).astype(sc_ref.dtype)

    for w in range(hist - 1):
        ncs_ref[w] = cs_ref[w + 1]
    ncs_ref[hist - 1] = glu
    for w in range(SC_WIDTH - 2):
        nss_ref[w] = ss_ref[w + 1]
    nss_ref[SC_WIDTH - 2] = p


def _mixer_sample(cg, gch, cs_t, ss_t, layer, conv_w, conv_b, conv_ln_g, conv_ln_b, sconv_w):
    nb = cg.shape[0]
    d_conf = cg.shape[1] // 2
    d_sc = gch.shape[1] // 3
    full2 = lambda i: (0, 0)
    par3 = lambda i: (layer, 0, 0)
    st4 = lambda i: (layer, 0, 0, 0)
    return pl.pallas_call(
        _mixer_sample_kernel,
        grid=(1,),
        in_specs=[
            pl.BlockSpec(cg.shape, full2),
            pl.BlockSpec(gch.shape, full2),
            pl.BlockSpec((None,) + cs_t.shape[1:], st4),
            pl.BlockSpec((None,) + ss_t.shape[1:], st4),
            pl.BlockSpec((None, CONF_WIDTH, d_conf), par3),
            pl.BlockSpec((None, 1, d_conf), par3),
            pl.BlockSpec((None, 1, d_conf), par3),
            pl.BlockSpec((None, 1, d_conf), par3),
            pl.BlockSpec((None, SC_WIDTH, d_sc), par3),
        ],
        out_specs=[
            pl.BlockSpec((nb, d_conf), full2),
            pl.BlockSpec((nb, d_sc), full2),
            pl.BlockSpec(cs_t.shape[1:], lambda i: (0, 0, 0)),
            pl.BlockSpec(ss_t.shape[1:], lambda i: (0, 0, 0)),
        ],
        out_shape=[
            jax.ShapeDtypeStruct((nb, d_conf), BF16),
            jax.ShapeDtypeStruct((nb, d_sc), BF16),
            jax.ShapeDtypeStruct(cs_t.shape[1:], F32),
            jax.ShapeDtypeStruct(ss_t.shape[1:], F32),
        ],
        compiler_params=_cparams(("arbitrary",)),
        name="mixer_sample",
    )(cg, gch, cs_t, ss_t, conv_w, conv_b, conv_ln_g, conv_ln_b, sconv_w)


def _attn_prompt_kernel(bias_ref, q_ref, k_ref, v_ref, tri_ref, o_ref, kb_ref, vt_ref):
    h = pl.program_id(1)
    i = pl.program_id(2)
    tq = q_ref.shape[0]
    tk = tri_ref.shape[0]
    ratio = tq // tk

    @pl.when(i == 0)
    def _():
        kb_ref[...] = k_ref[...].astype(BF16)
        for kb in range(vt_ref.shape[0]):
            vt_ref[kb] = v_ref[kb * tk:(kb + 1) * tk, :].T.astype(BF16)

    bias = bias_ref[h]
    q = q_ref[...]
    tri2 = tri_ref[...]

    def blocks(kbs, q0s, acc, run, masked):
        zs, keeps, splits, csums, avs = [], [], [], [], []
        for kb, q0 in zip(kbs, q0s):
            start = pl.multiple_of(kb * tk, tk)
            kblk = kb_ref[pl.ds(start, tk), :]
            zs.append(lax.dot_general(kblk, q[q0:, :], (((1,), (1,)), ((), ())),
                                      preferred_element_type=F32) + bias)
        for kb, q0, z in zip(kbs, q0s, zs):
            sp = _softplus2(z)
            if masked:
                keyp = kb * tk + lax.broadcasted_iota(jnp.int32, z.shape, 0)
                qryp = i * tq + q0 + lax.broadcasted_iota(jnp.int32, z.shape, 1)
                keeps.append(keyp < qryp)
                sp = jnp.where(keeps[-1], sp, 0.0)
            hi, lo = _split_bf16(sp)
            splits.append(jnp.concatenate([hi, lo], axis=0))
        for w in splits:
            csums.append(jnp.dot(tri2, w, preferred_element_type=F32))
        def from_q0(full, q0, tail):
            return tail if q0 == 0 else jnp.concatenate([full[:, :q0], tail], axis=1)

        for j, q0 in enumerate(q0s):
            csum = csums[j] + run[:, q0:]
            run = from_q0(run, q0, csum[0:1, :])
            a = jnp.exp2(zs[j] - csum)
            if masked:
                a = jnp.where(keeps[j], a, 0.0)
            avs.append(jnp.dot(vt_ref[kbs[j]], a.astype(BF16), preferred_element_type=F32))
        for q0, av in zip(q0s, avs):
            acc = from_q0(acc, q0, acc[:, q0:] + av)
        return acc, run

    acc = jnp.zeros((q_ref.shape[1], tq), F32)
    run = jnp.zeros((1, tq), F32)
    diag = list(reversed(range(ratio)))
    acc, run = blocks([i * ratio + c for c in diag], [c * tk for c in diag], acc, run, True)

    def body(jj, carry):
        first = i * ratio - 1 - jj * ATT_UNROLL
        return blocks([first - d for d in range(ATT_UNROLL)], [0] * ATT_UNROLL, carry[0], carry[1], False)

    acc, run = lax.fori_loop(0, i * ratio // ATT_UNROLL, body, (acc, run))
    o_ref[...] = acc.T.astype(o_ref.dtype)


def _attn_prompt(q, k, v, bias, batch, seq):
    d_sb = q.shape[1]
    heads = d_sb // SB_HEAD_DIM
    tq, tk = ATT_TQ, ATT_TK
    nq = seq // tq
    assert seq % tq == 0 and (tq // tk) % ATT_UNROLL == 0
    r = lax.broadcasted_iota(jnp.int32, (tk, tk), 0)
    c = lax.broadcasted_iota(jnp.int32, (tk, tk), 1)
    upper = (c >= r).astype(BF16)
    tri = jnp.concatenate([upper, upper], axis=1)
    grid_spec = pltpu.PrefetchScalarGridSpec(
        num_scalar_prefetch=0,
        grid=(batch, heads, nq),
        in_specs=[
            pl.BlockSpec(memory_space=pltpu.SMEM),
            pl.BlockSpec((tq, SB_HEAD_DIM), lambda b, h, i: (b * nq + i, h)),
            pl.BlockSpec((seq, SB_HEAD_DIM), lambda b, h, i: (b, h)),
            pl.BlockSpec((seq, SB_HEAD_DIM), lambda b, h, i: (b, h)),
            pl.BlockSpec((tk, 2 * tk), lambda b, h, i: (0, 0)),
        ],
        out_specs=pl.BlockSpec((tq, SB_HEAD_DIM), lambda b, h, i: (b * nq + i, h)),
        scratch_shapes=[pltpu.VMEM((seq, SB_HEAD_DIM), BF16),
                        pltpu.VMEM((seq // tk, SB_HEAD_DIM, tk), BF16)],
    )
    return pl.pallas_call(
        _attn_prompt_kernel,
        grid_spec=grid_spec,
        out_shape=jax.ShapeDtypeStruct((batch * seq, d_sb), BF16),
        compiler_params=_cparams(("arbitrary", "arbitrary", "arbitrary"), ATT_FLAGS),
        name="attn_prompt",
    )(bias, q, k, v, tri)


def _attn_sample_kernel(pps, pt_ref, qbd_ref, bias_ref, tri_ref, *refs):
    k_refs = refs[:pps]
    v_refs = refs[pps:2 * pps]
    o_ref = refs[2 * pps]
    acc_ref, run_ref = refs[2 * pps + 1:]
    s = pl.program_id(1)
    heads = acc_ref.shape[1] // k_refs[0].shape[1]
    page = k_refs[0].shape[0] // heads

    @pl.when(s == 0)
    def _():
        acc_ref[...] = jnp.zeros(acc_ref.shape, F32)
        run_ref[...] = jnp.zeros(run_ref.shape, F32)

    def flat_page(ref):
        return jnp.concatenate([ref[pl.ds(h, page, stride=heads), :] for h in range(heads)],
                               axis=1).astype(BF16)

    kc = jnp.concatenate([flat_page(r) for r in k_refs], axis=0)
    vc = jnp.concatenate([flat_page(r) for r in v_refs], axis=0)
    z_all = jnp.dot(kc, qbd_ref[...], preferred_element_type=F32) + bias_ref[...]
    tri = tri_ref[...]
    run = run_ref[...]
    a_parts = []
    for r in range(pps):
        z = z_all[r * page:(r + 1) * page, :]
        hi, lo = _split_bf16(_softplus2(z))
        csum = (jnp.dot(tri, hi, preferred_element_type=F32)
                + jnp.dot(tri, lo, preferred_element_type=F32)) + run
        a_parts.append(jnp.exp2(z - csum).astype(BF16))
        run = csum[0:1, :]
    run_ref[...] = run
    a_all = jnp.concatenate(a_parts, axis=0)
    acc_ref[...] += lax.dot_general(a_all, vc, (((0,), (0,)), ((), ())), preferred_element_type=F32)

    @pl.when(s == pl.num_programs(1) - 1)
    def _():
        top = acc_ref[0:heads, :]
        row = lax.broadcasted_iota(jnp.int32, top.shape, 0)
        col = lax.broadcasted_iota(jnp.int32, top.shape, 1)
        lo_c = row * SB_HEAD_DIM
        keep = (col >= lo_c) & (col < lo_c + SB_HEAD_DIM)
        o_ref[...] = jnp.sum(jnp.where(keep, top, 0.0), axis=0, keepdims=True).astype(o_ref.dtype)


def _attn_sample(q, bias, cache_k, cache_v, page_table, layer):
    nb, d_sb = q.shape
    heads = d_sb // SB_HEAD_DIM
    depth, n_pool, page = cache_k.shape[:3]
    n_pages = page_table.shape[1]
    pps = PAGES_PER_STEP
    assert n_pages % pps == 0 and heads == SUBLANES
    pt = page_table.reshape(-1)
    ck = cache_k.reshape(depth, n_pool, page * heads, SB_HEAD_DIM)
    cv = cache_v.reshape(depth, n_pool, page * heads, SB_HEAD_DIM)
    r = lax.broadcasted_iota(jnp.int32, (d_sb, LANES), 0) // SB_HEAD_DIM
    c = lax.broadcasted_iota(jnp.int32, (d_sb, LANES), 1)
    qbd = jnp.where((r == c)[None], q[:, :, None], jnp.zeros((), q.dtype))
    bias_row = jnp.zeros((1, LANES), F32).at[0, :heads].set(bias)
    rr = lax.broadcasted_iota(jnp.int32, (page, page), 0)
    cc = lax.broadcasted_iota(jnp.int32, (page, page), 1)
    tri = (cc >= rr).astype(BF16)

    def page_map(rk):
        def index_map(b, s, pt_ref):
            pg = pt_ref[b * n_pages + (n_pages - 1 - (s * pps + rk))]
            return (layer, pg, 0, 0)
        return index_map

    page_specs = [pl.BlockSpec((None, None, page * heads, SB_HEAD_DIM), page_map(rk)) for rk in range(pps)]
    grid_spec = pltpu.PrefetchScalarGridSpec(
        num_scalar_prefetch=1,
        grid=(nb, n_pages // pps),
        in_specs=[
            pl.BlockSpec((None, d_sb, LANES), lambda b, s, pt_ref: (b, 0, 0)),
            pl.BlockSpec((1, LANES), lambda b, s, pt_ref: (0, 0)),
            pl.BlockSpec((page, page), lambda b, s, pt_ref: (0, 0)),
        ] + page_specs + page_specs,
        out_specs=pl.BlockSpec((None, 1, d_sb), lambda b, s, pt_ref: (b, 0, 0)),
        scratch_shapes=[pltpu.VMEM((LANES, d_sb), F32), pltpu.VMEM((1, LANES), F32)],
    )
    out = pl.pallas_call(
        functools.partial(_attn_sample_kernel, pps),
        grid_spec=grid_spec,
        out_shape=jax.ShapeDtypeStruct((nb, 1, d_sb), BF16),
        compiler_params=_cparams(("arbitrary", "arbitrary")),
        name="attn_sample",
    )(pt, qbd, bias_row, tri, *([ck] * pps), *([cv] * pps))
    return out.reshape(nb, d_sb)


def _route(logits):
    lane = lax.broadcasted_iota(jnp.int32, logits.shape, 1).astype(F32)
    neg = -jnp.inf
    far = float(LANES)
    is_group = lane < N_GROUPS
    lg = jnp.where(is_group, logits, neg)
    gmax = jnp.max(lg, axis=1, keepdims=True)
    gidx = jnp.min(jnp.where(lg == gmax, lane, far), axis=1, keepdims=True)
    den = jnp.sum(jnp.where(is_group, jnp.exp(logits - gmax), 0.0), axis=1, keepdims=True)
    ggate = 1.0 / den
    first = N_GROUPS + gidx * EXPERTS_PER_GROUP
    le = jnp.where((lane >= first) & (lane < first + EXPERTS_PER_GROUP), logits, neg)
    v1 = jnp.max(le, axis=1, keepdims=True)
    i1 = jnp.min(jnp.where(le == v1, lane, far), axis=1, keepdims=True)
    le2 = jnp.where(lane == i1, neg, le)
    v2 = jnp.max(le2, axis=1, keepdims=True)
    i2 = jnp.min(jnp.where(le2 == v2, lane, far), axis=1, keepdims=True)
    e21 = jnp.exp(v2 - v1)
    p1 = 1.0 / (1.0 + e21)
    p2 = e21 * p1
    return lane, gidx, (i1 - N_GROUPS, ggate * p1), (i2 - N_GROUPS, ggate * p2)


def _comb(route):
    lane, _, (e1, w1), (e2, w2) = route
    return jnp.where(lane == e1, w1, jnp.where(lane == e2, w2, 0.0))


def _router_logits(x1, wrh_ref, wrl_ref, br_ref):
    hi, lo = _split_bf16(x1)
    wrh = wrh_ref[...]
    logits = (jnp.dot(hi, wrh, preferred_element_type=F32)
              + jnp.dot(lo, wrh, preferred_element_type=F32)
              + jnp.dot(hi, wrl_ref[...], preferred_element_type=F32)) + br_ref[...]
    return hi, logits


def _outproj_kernel(alpha, conf_ref, att_ref, sc_ref, x_ref, w_ref, g_ref, b_ref, wrh_ref, wrl_ref, br_ref,
                    x1_ref, x1b_ref, comb_ref):
    cat = jnp.concatenate([conf_ref[...], att_ref[...], sc_ref[...]], axis=1)
    mix = jnp.dot(cat, w_ref[...], preferred_element_type=F32)
    x1 = _layer_norm(alpha * x_ref[...] + mix, g_ref[...], b_ref[...])
    x1_ref[...] = x1
    hi, logits = _router_logits(x1, wrh_ref, wrl_ref, br_ref)
    x1b_ref[...] = hi
    comb_ref[...] = _comb(_route(logits))


def _outproj(conf, att, sc, x, w_out_b, layer, alpha, ln_g, ln_b, wr_hi, wr_lo, br):
    m, d_model = x.shape
    tm = min(OUT_TM, m)
    d_mix = w_out_b.shape[1]
    row = lambda i: (i, 0)
    par3 = lambda i: (layer, 0, 0)
    return pl.pallas_call(
        functools.partial(_outproj_kernel, alpha),
        grid=(m // tm,),
        in_specs=[
            pl.BlockSpec((tm, conf.shape[1]), row),
            pl.BlockSpec((tm, att.shape[1]), row),
            pl.BlockSpec((tm, sc.shape[1]), row),
            pl.BlockSpec((tm, d_model), row),
            pl.BlockSpec((None, d_mix, d_model), par3),
            pl.BlockSpec((None, 1, d_model), par3),
            pl.BlockSpec((None, 1, d_model), par3),
            pl.BlockSpec((None, d_model, LANES), par3),
            pl.BlockSpec((None, d_model, LANES), par3),
            pl.BlockSpec((None, 1, LANES), par3),
        ],
        out_specs=[
            pl.BlockSpec((tm, d_model), row),
            pl.BlockSpec((tm, d_model), row),
            pl.BlockSpec((tm, LANES), row),
        ],
        out_shape=[
            jax.ShapeDtypeStruct((m, d_model), F32),
            jax.ShapeDtypeStruct((m, d_model), BF16),
            jax.ShapeDtypeStruct((m, LANES), F32),
        ],
        compiler_params=_cparams(("arbitrary",)),
        name="outproj_router",
    )(conf, att, sc, x, w_out_b, ln_g, ln_b, wr_hi, wr_lo, br)


def _moe_kernel(alpha, xb_ref, x1_ref, comb_ref, wg_ref, wu_ref, wd_ref, g_ref, b_ref, o_ref, acc_ref):
    e = pl.program_id(1)

    @pl.when(e == 0)
    def _():
        acc_ref[...] = jnp.zeros(acc_ref.shape, F32)

    xb = xb_ref[...]
    gate = jnp.dot(xb, wg_ref[...].astype(BF16), preferred_element_type=F32)
    up = jnp.dot(xb, wu_ref[...].astype(BF16), preferred_element_type=F32)
    comb = comb_ref[...]
    lane = lax.broadcasted_iota(jnp.int32, comb.shape, 1)
    ce = jnp.sum(jnp.where(lane == e, comb, 0.0), axis=1, keepdims=True)
    hid = (gate * jax.nn.sigmoid(gate)) * up * ce
    acc_ref[...] += jnp.dot(hid.astype(BF16), wd_ref[...].astype(BF16), preferred_element_type=F32)

    @pl.when(e == pl.num_programs(1) - 1)
    def _():
        o_ref[...] = _layer_norm(alpha * x1_ref[...] + acc_ref[...], g_ref[...], b_ref[...])


def _moe(x1b, x1, comb, wg_b, wu_b, wd_b, layer, alpha, ln_g, ln_b):
    m, d_model = x1.shape
    n_exp, _, d_exp = wg_b.shape[1:]
    tm = min(MOE_TM, m)
    row = lambda i, e: (i, 0)
    par3 = lambda i, e: (layer, 0, 0)
    wmap = lambda i, e: (layer, e, 0, 0)
    return pl.pallas_call(
        functools.partial(_moe_kernel, alpha),
        grid=(m // tm, n_exp),
        in_specs=[
            pl.BlockSpec((tm, d_model), row),
            pl.BlockSpec((tm, d_model), row),
            pl.BlockSpec((tm, LANES), row),
            pl.BlockSpec((None, None, d_model, d_exp), wmap),
            pl.BlockSpec((None, None, d_model, d_exp), wmap),
            pl.BlockSpec((None, None, d_exp, d_model), wmap),
            pl.BlockSpec((None, 1, d_model), par3),
            pl.BlockSpec((None, 1, d_model), par3),
        ],
        out_specs=pl.BlockSpec((tm, d_model), row),
        out_shape=jax.ShapeDtypeStruct((m, d_model), F32),
        scratch_shapes=[pltpu.VMEM((tm, d_model), F32)],
        compiler_params=_cparams(("arbitrary", "arbitrary")),
        name="moe_dense",
    )(x1b, x1, comb, wg_b, wu_b, wd_b, ln_g, ln_b)


def _moe_layout(n_tokens):
    nt = n_tokens // DISP_TM
    max_rows = n_tokens + nt * (ROW_ALIGN - 1)
    cap = pl.cdiv(max_rows + DISP_TM + EXP_TM, EXP_TM) * EXP_TM
    t_max = pl.cdiv(max_rows + N_GROUPS * DISP_TM, EXP_TM) + N_GROUPS
    return nt, cap, t_max


def _dispatch_kernel(alpha, cap, conf_ref, att_ref, sc_ref, x_ref, w_ref, g_ref, b_ref, wrh_ref, wrl_ref,
                     br_ref, ltri_ref, rows_in_ref, x1_ref, info_ref, tstart_ref, xs_ref, sbuf, cnt_ref, sem):
    del rows_in_ref
    i = pl.program_id(0)
    nt = pl.num_programs(0)
    slot = i % 2
    tm = x_ref.shape[0]

    @pl.when(i == 0)
    def _():
        for g in range(N_GROUPS):
            cnt_ref[g] = 0
        sbuf[...] = jnp.zeros(sbuf.shape, BF16)

    cat = jnp.concatenate([conf_ref[...], att_ref[...], sc_ref[...]], axis=1)
    mix = jnp.dot(cat, w_ref[...], preferred_element_type=F32)
    x1 = _layer_norm(alpha * x_ref[...] + mix, g_ref[...], b_ref[...])
    x1_ref[...] = x1
    hi, logits = _router_logits(x1, wrh_ref, wrl_ref, br_ref)
    lane, gidx, (e1, w1), (e2, w2) = _route(logits)

    first = gidx * EXPERTS_PER_GROUP
    l1, l2 = e1 - first, e2 - first
    w1h = w1.astype(BF16).astype(F32)
    w2h = w2.astype(BF16).astype(F32)
    gates = jnp.where(lane == l1, w1h, jnp.where(lane == l2, w2h, 0.0))
    gates = jnp.where(lane == l1 + EXPERTS_PER_GROUP, w1 - w1h,
                      jnp.where(lane == l2 + EXPERTS_PER_GROUP, w2 - w2h, gates))
    xrow = jnp.concatenate([hi, gates.astype(BF16)], axis=1)

    onehot = jnp.where(lane == gidx, 1.0, 0.0)
    before = jnp.dot(ltri_ref[...], onehot.astype(BF16), preferred_element_type=F32)
    rank = jnp.sum(jnp.where(lane == gidx, before, 0.0), axis=1, keepdims=True)
    lens = (before[tm - 1:tm, :] + onehot[tm - 1:tm, :]).astype(jnp.int32)
    info_ref[...] = jnp.where(lane == 0.0, gidx, jnp.where(lane == 1.0, rank, 0.0))

    lbase = jnp.int32(0)
    lbases, pads = [], []
    dest = rank
    for g in range(N_GROUPS):
        pad = ((lens[0, g] + (ROW_ALIGN - 1)) // ROW_ALIGN) * ROW_ALIGN
        lbases.append(lbase)
        pads.append(pad)
        dest = dest + jnp.where(gidx == float(g), lbase.astype(F32), 0.0)
        lbase = lbase + pad
    col = lax.broadcasted_iota(jnp.int32, (tm, SORT_ROWS), 1).astype(F32)
    onehot_dest = jnp.where(col == dest, 1.0, 0.0).astype(BF16)
    srt = lax.dot_general(onehot_dest, xrow, (((0,), (0,)), ((), ())), preferred_element_type=F32)
    sbuf[slot, 0:SORT_ROWS, :] = srt.astype(BF16)

    def copy(sl, g, src, dst):
        return pltpu.make_async_copy(
            sbuf.at[sl, pl.ds(pl.multiple_of(src, ROW_ALIGN), tm)],
            xs_ref.at[pl.ds(pl.multiple_of(dst, ROW_ALIGN), tm)], sem.at[sl, g])

    @pl.when(i > 0)
    def _():
        for g in range(N_GROUPS):
            copy(1 - slot, g, 0, 0).wait()

    for g in range(N_GROUPS):
        start = cnt_ref[g]
        copy(slot, g, lbases[g], g * cap + start).start()
        tstart_ref[i, g] = start
        cnt_ref[g] = start + pads[g]

    @pl.when(i == nt - 1)
    def _():
        for g in range(N_GROUPS):
            copy(slot, g, 0, 0).wait()
            tstart_ref[nt, g] = cnt_ref[g]


def _dispatch(conf, att, sc, x, w_out_b, layer, alpha, ln_g, ln_b, wr_hi, wr_lo, br, rows_buf):
    m, d_model = x.shape
    tm = DISP_TM
    nt, cap, _ = _moe_layout(m)
    d_mix = w_out_b.shape[1]
    width = d_model + LANES
    rr = lax.broadcasted_iota(jnp.int32, (tm, tm), 0)
    cc = lax.broadcasted_iota(jnp.int32, (tm, tm), 1)
    ltri = (cc < rr).astype(BF16)
    row = lambda i: (i, 0)
    par3 = lambda i: (layer, 0, 0)
    return pl.pallas_call(
        functools.partial(_dispatch_kernel, alpha, cap),
        grid=(nt,),
        in_specs=[
            pl.BlockSpec((tm, conf.shape[1]), row),
            pl.BlockSpec((tm, att.shape[1]), row),
            pl.BlockSpec((tm, sc.shape[1]), row),
            pl.BlockSpec((tm, d_model), row),
            pl.BlockSpec((None, d_mix, d_model), par3),
            pl.BlockSpec((None, 1, d_model), par3),
            pl.BlockSpec((None, 1, d_model), par3),
            pl.BlockSpec((None, d_model, LANES), par3),
            pl.BlockSpec((None, d_model, LANES), par3),
            pl.BlockSpec((None, 1, LANES), par3),
            pl.BlockSpec((tm, tm), lambda i: (0, 0)),
            pl.BlockSpec(memory_space=pl.ANY),
        ],
        out_specs=[
            pl.BlockSpec((tm, d_model), row),
            pl.BlockSpec((tm, LANES), row),
            pl.BlockSpec(memory_space=pltpu.SMEM),
            pl.BlockSpec(memory_space=pl.ANY),
        ],
        input_output_aliases={11: 3},
        out_shape=[
            jax.ShapeDtypeStruct((m, d_model), F32),
            jax.ShapeDtypeStruct((m, LANES), F32),
            jax.ShapeDtypeStruct((nt + 1, N_GROUPS), jnp.int32),
            jax.ShapeDtypeStruct((N_GROUPS * cap, width), BF16),
        ],
        scratch_shapes=[
            pltpu.VMEM((2, SORT_ROWS + tm, width), BF16),
            pltpu.SMEM((N_GROUPS,), jnp.int32),
            pltpu.SemaphoreType.DMA((2, N_GROUPS)),
        ],
        compiler_params=_cparams(("arbitrary",)),
        name="dispatch",
    )(conf, att, sc, x, w_out_b, ln_g, ln_b, wr_hi, wr_lo, br, ltri, rows_buf)


def _expert_kernel(cpe, tg_ref, tb_ref, tn_ref, tot_ref, xs_ref, wg_ref, wu_ref, wd_ref, ys_ref,
                   acc_ref, wgb_ref, wub_ref, wdb_ref):
    t = pl.program_id(0)
    c = pl.program_id(1)
    nc = EXPERTS_PER_GROUP * cpe
    d_model = wg_ref.shape[0]

    @pl.when(t < tot_ref[0])
    def _():
        @pl.when(c == 0)
        def _():
            acc_ref[...] = jnp.zeros(acc_ref.shape, F32)

        wgb_ref[...] = wg_ref[...].astype(BF16)
        wub_ref[...] = wu_ref[...].astype(BF16)
        wdb_ref[...] = wd_ref[...].astype(BF16)
        e_local = c // cpe

        def sub(s, carry):
            r0 = pl.multiple_of(s * EXP_SUB, EXP_SUB)
            xb = xs_ref[pl.ds(r0, EXP_SUB), 0:d_model]
            gates = xs_ref[pl.ds(r0, EXP_SUB), d_model:d_model + LANES].astype(F32)
            lane = lax.broadcasted_iota(jnp.int32, gates.shape, 1)
            mine = (lane == e_local) | (lane == e_local + EXPERTS_PER_GROUP)
            ce = jnp.sum(jnp.where(mine, gates, 0.0), axis=1, keepdims=True)
            gate = jnp.dot(xb, wgb_ref[...], preferred_element_type=F32)
            up = jnp.dot(xb, wub_ref[...], preferred_element_type=F32)
            hid = (gate * jax.nn.sigmoid(gate)) * up * ce
            acc_ref[pl.ds(r0, EXP_SUB), :] += jnp.dot(hid.astype(BF16), wdb_ref[...],
                                                      preferred_element_type=F32)
            return carry

        lax.fori_loop(0, tn_ref[t], sub, 0)

        @pl.when(c == nc - 1)
        def _():
            ys_ref[:, 0:d_model] = acc_ref[...].astype(ys_ref.dtype)
            ys_ref[:, d_model:d_model + LANES] = xs_ref[:, d_model:d_model + LANES]


def _experts(xs, tile_g, tile_blk, tile_nsub, total, wg, wu, wd, layer, cap):
    width = xs.shape[1]
    d_model = width - LANES
    d_exp = wg.shape[-1]
    t_max = tile_g.shape[0]
    cpe = d_exp // EXP_HC
    nc = EXPERTS_PER_GROUP * cpe
    blocks_per_group = cap // EXP_TM

    def chunk(t, c, tot):
        return jnp.where(t < tot[0], c, nc - 1)

    def rows(t, c, tg, tb, tn, tot):
        return (tg[t] * blocks_per_group + tb[t], 0)

    def w_in_map(t, c, tg, tb, tn, tot):
        ce = chunk(t, c, tot)
        return (layer, tg[t] * EXPERTS_PER_GROUP + ce // cpe, 0, ce % cpe)

    def w_out_map(t, c, tg, tb, tn, tot):
        ce = chunk(t, c, tot)
        return (layer, tg[t] * EXPERTS_PER_GROUP + ce // cpe, ce % cpe, 0)

    grid_spec = pltpu.PrefetchScalarGridSpec(
        num_scalar_prefetch=4,
        grid=(t_max, nc),
        in_specs=[
            pl.BlockSpec((EXP_TM, width), rows),
            pl.BlockSpec((None, None, d_model, EXP_HC), w_in_map),
            pl.BlockSpec((None, None, d_model, EXP_HC), w_in_map),
            pl.BlockSpec((None, None, EXP_HC, d_model), w_out_map),
        ],
        out_specs=pl.BlockSpec((EXP_TM, width), rows),
        scratch_shapes=[
            pltpu.VMEM((EXP_TM, d_model), F32),
            pltpu.VMEM((d_model, EXP_HC), BF16),
            pltpu.VMEM((d_model, EXP_HC), BF16),
            pltpu.VMEM((EXP_HC, d_model), BF16),
        ],
    )
    return pl.pallas_call(
        functools.partial(_expert_kernel, cpe),
        grid_spec=grid_spec,
        out_shape=jax.ShapeDtypeStruct(xs.shape, xs.dtype),
        input_output_aliases={4: 0},
        compiler_params=_cparams(("arbitrary", "arbitrary")),
        name="experts",
    )(tile_g, tile_blk, tile_nsub, total, xs, wg, wu, wd)


def _combine_kernel(alpha, ts_ref, x1_ref, info_ref, *refs):
    wins = refs[:N_GROUPS]
    g_ref, b_ref, o_ref = refs[N_GROUPS:]
    tm = x1_ref.shape[0]
    info = info_ref[...]
    gidx = info[:, 0:1]
    rank = info[:, 1:2]
    col = lax.broadcasted_iota(jnp.int32, (tm, tm), 1).astype(F32)
    y = jnp.zeros(x1_ref.shape, F32)
    for g in range(N_GROUPS):
        pick = jnp.where((gidx == float(g)) & (col == rank), 1.0, 0.0).astype(BF16)
        y = y + jnp.dot(pick, wins[g][...], preferred_element_type=F32)
    o_ref[...] = _layer_norm(alpha * x1_ref[...] + y, g_ref[...], b_ref[...])


def _combine(x1, info, tstart, ys, layer, alpha, ln_g, ln_b, cap):
    m, d_model = x1.shape
    tm = DISP_TM
    ts = tstart.reshape(-1)

    def window(g):
        def index_map(i, ts_ref):
            return (pl.multiple_of(g * cap + ts_ref[i * N_GROUPS + g], ROW_ALIGN), 0)
        return pl.BlockSpec((pl.Element(tm), pl.Element(d_model)), index_map)

    row = lambda i, ts_ref: (i, 0)
    par3 = lambda i, ts_ref: (layer, 0, 0)
    grid_spec = pltpu.PrefetchScalarGridSpec(
        num_scalar_prefetch=1,
        grid=(m // tm,),
        in_specs=[pl.BlockSpec((tm, d_model), row), pl.BlockSpec((tm, LANES), row)]
        + [window(g) for g in range(N_GROUPS)]
        + [pl.BlockSpec((None, 1, d_model), par3), pl.BlockSpec((None, 1, d_model), par3)],
        out_specs=pl.BlockSpec((tm, d_model), row),
    )
    return pl.pallas_call(
        functools.partial(_combine_kernel, alpha),
        grid_spec=grid_spec,
        out_shape=jax.ShapeDtypeStruct((m, d_model), F32),
        compiler_params=_cparams(("arbitrary",)),
        name="combine",
    )(ts, x1, info, *([ys] * N_GROUPS), ln_g, ln_b)


def _expert_tiles(tstart, t_max):
    ends = tstart[-1]
    per_group = (ends + DISP_TM + EXP_TM - 1) // EXP_TM
    cum = jnp.cumsum(per_group)
    total = cum[-1:]
    t = jnp.minimum(jnp.arange(t_max, dtype=jnp.int32), total[0] - 1)
    tile_g = jnp.sum((t[:, None] >= cum[None, :]).astype(jnp.int32), axis=1)
    tile_blk = t - (cum - per_group)[tile_g]
    valid = jnp.clip(ends[tile_g] - tile_blk * EXP_TM, 0, EXP_TM)
    tile_nsub = (valid + EXP_SUB - 1) // EXP_SUB
    return tile_g, tile_blk, tile_nsub.astype(jnp.int32), total.astype(jnp.int32)


def kernel(x_prompt, x_sample, cache_k, cache_v, state_conv, state_sconv, page_table, w_in, conv_w, conv_b, conv_ln_g, conv_ln_b, sconv_w, sb_bias, w_out, ln1_g, ln1_b, router_group_w, router_group_b, router_expert_w, router_expert_b, expert_w_gate, expert_w_up, expert_w_down, ln2_g, ln2_b):
    batch, seq, d_model = x_prompt.shape
    nb = x_sample.shape[0]
    depth = w_in.shape[0]
    d_conf = conv_w.shape[-1]
    d_sc = sconv_w.shape[-1]
    heads = sb_bias.shape[-1]
    d_sb = heads * SB_HEAD_DIM
    alpha = (2 * depth) ** 0.25

    row3 = lambda a: a.reshape(depth, 1, a.shape[-1])
    conv_b3, conv_ln_g3, conv_ln_b3 = row3(conv_b), row3(conv_ln_g), row3(conv_ln_b)
    ln1_g3, ln1_b3, ln2_g3, ln2_b3 = row3(ln1_g), row3(ln1_b), row3(ln2_g), row3(ln2_b)

    n_log = N_GROUPS + N_EXPERTS
    wr = jnp.concatenate(
        [router_group_w, router_expert_w.transpose(0, 2, 1, 3).reshape(depth, d_model, N_EXPERTS)], axis=-1)
    wr = jnp.pad(wr, ((0, 0), (0, 0), (0, LANES - n_log)))
    wr_hi = wr.astype(BF16)
    wr_lo = (wr - wr_hi.astype(F32)).astype(BF16)
    br = jnp.concatenate([router_group_b, router_expert_b.reshape(depth, N_EXPERTS)], axis=-1)
    br = jnp.pad(br, ((0, 0), (0, LANES - n_log))).reshape(depth, 1, LANES)

    w_in_b = w_in.astype(BF16)
    w_out_b = w_out.astype(BF16)
    experts = (expert_w_gate, expert_w_up, expert_w_down)
    _, cap, t_max = _moe_layout(batch * seq)
    rows = jnp.zeros((N_GROUPS * cap, d_model + LANES), BF16)

    cs_t = state_conv.transpose(0, 2, 1, 3)
    ss_t = state_sconv.transpose(0, 2, 1, 3)

    yp = x_prompt.reshape(batch * seq, d_model)
    ys = x_sample.reshape(nb, d_model)
    kp, vp, cp, sp, kd, vd, cd, sd = [], [], [], [], [], [], [], []
    for l in range(depth):
        cg, q, k, v, gch = _inproj(yp, w_in_b, l, d_conf, d_sb, d_sc)
        conf, sc, cst, sst = _mixer_prompt(cg, gch, batch, seq, l, conv_w, conv_b3, conv_ln_g3, conv_ln_b3,
                                           sconv_w)
        att = _attn_prompt(q, k, v, sb_bias[l] * LOG2_E, batch, seq)
        x1, info, tstart, rows = _dispatch(conf, att, sc, yp, w_out_b, l, alpha, ln1_g3, ln1_b3, wr_hi, wr_lo, br,
                                           rows)
        rows = _experts(rows, *_expert_tiles(tstart, t_max), *experts, l, cap)
        yp = _combine(x1, info, tstart, rows, l, alpha, ln2_g3, ln2_b3, cap)
        kp.append(k.reshape(batch, seq, heads, SB_HEAD_DIM))
        vp.append(v.reshape(batch, seq, heads, SB_HEAD_DIM))
        cp.append(cst[:, CONV_HIST - (CONF_WIDTH - 1):, :])
        sp.append(sst[:, SC_HIST - (SC_WIDTH - 1):, :])

        cg, q, k, v, gch = _inproj(ys, w_in_b, l, d_conf, d_sb, d_sc)
        conf, sc, ncs, nss = _mixer_sample(cg, gch, cs_t, ss_t, l, conv_w, conv_b3, conv_ln_g3, conv_ln_b3,
                                           sconv_w)
        att = _attn_sample(q, sb_bias[l] * LOG2_E, cache_k, cache_v, page_table, l)
        x1, x1b, comb = _outproj(conf, att, sc, ys, w_out_b, l, alpha, ln1_g3, ln1_b3, wr_hi, wr_lo, br)
        ys = _moe(x1b, x1, comb, *experts, l, alpha, ln2_g3, ln2_b3)
        kd.append(k.reshape(nb, 1, heads, SB_HEAD_DIM))
        vd.append(v.reshape(nb, 1, heads, SB_HEAD_DIM))
        cd.append(ncs.transpose(1, 0, 2))
        sd.append(nss.transpose(1, 0, 2))

    return (yp.reshape(batch, seq, d_model), ys.reshape(nb, 1, d_model),
            jnp.stack(kp), jnp.stack(vp), jnp.stack(cp), jnp.stack(sp),
            jnp.stack(kd), jnp.stack(vd), jnp.stack(cd), jnp.stack(sd))
```

```python
import functools
import math

import jax
import jax.numpy as jnp
from jax import lax
from jax.experimental import pallas as pl
from jax.experimental.pallas import tpu as pltpu

F32 = jnp.float32
BF16 = jnp.bfloat16

SB_HEAD_DIM = 128
CONF_WIDTH = 31
SC_WIDTH = 3
N_GROUPS = 4
EXPERTS_PER_GROUP = 4
N_EXPERTS = N_GROUPS * EXPERTS_PER_GROUP
LN_EPS = 1e-5
LOG2_E = math.log2(math.e)

LANES = 128
SUBLANES = 8
VMEM_LIMIT_MB = 56

INPROJ_TM = 1024
INPROJ_TN = 512
MIX_T = 256
MIX_RC = 32
CONV_HIST = 32
SC_HIST = 8
ATT_TQ = 1024
ATT_TK = 256
ATT_UNROLL = 4
PAGES_PER_STEP = 8
OUT_TM = 256
MOE_TM = 512
DISP_TM = 256
ROW_ALIGN = 16
SORT_ROWS = 384
EXP_TM = 1280
EXP_SUB = 256
EXP_HC = 256


def _cparams(sem):
    return pltpu.CompilerParams(dimension_semantics=sem,
                                vmem_limit_bytes=VMEM_LIMIT_MB * 1024 * 1024)


def _layer_norm(y, g, b):
    mu = jnp.mean(y, axis=-1, keepdims=True)
    d = y - mu
    var = jnp.mean(d * d, axis=-1, keepdims=True)
    return d * lax.rsqrt(var + LN_EPS) * g + b


def _softplus2(z):
    return jnp.maximum(z, 0.0) + jnp.log2(1.0 + jnp.exp2(-jnp.abs(z)))


def _split_bf16(x):
    hi = x.astype(BF16)
    lo = (x - hi.astype(F32)).astype(BF16)
    return hi, lo


def _inproj_kernel(bounds, scale, x_ref, w_ref, cg_ref, q_ref, k_ref, v_ref, gch_ref, xb_ref):
    j = pl.program_id(1)

    @pl.when(j == 0)
    def _():
        xb_ref[...] = x_ref[...].astype(BF16)

    acc = jnp.dot(xb_ref[...], w_ref[...], preferred_element_type=F32)
    b_cg, b_q, b_k, b_v = bounds

    @pl.when(j < b_cg)
    def _():
        cg_ref[...] = acc

    @pl.when((j >= b_cg) & (j < b_q))
    def _():
        q_ref[...] = (acc * scale).astype(BF16)

    @pl.when((j >= b_q) & (j < b_k))
    def _():
        k_ref[...] = acc

    @pl.when((j >= b_k) & (j < b_v))
    def _():
        v_ref[...] = acc

    @pl.when(j >= b_v)
    def _():
        gch_ref[...] = acc


def _inproj(x, w_in, layer, d_conf, d_sb, d_sc):
    m, d_model = x.shape
    p_in = w_in.shape[-1]
    tm = min(INPROJ_TM, m)
    tn = INPROJ_TN
    n_cg, n_sb, n_gch = 2 * d_conf // tn, d_sb // tn, 3 * d_sc // tn
    bounds = (n_cg, n_cg + n_sb, n_cg + 2 * n_sb, n_cg + 3 * n_sb)
    assert bounds[-1] + n_gch == p_in // tn and m % tm == 0

    def out_map(lo, n):
        return lambda i, j: (i, jnp.clip(j - lo, 0, n - 1))

    kernel = functools.partial(_inproj_kernel, bounds, SB_HEAD_DIM ** -0.5 * LOG2_E)
    return pl.pallas_call(
        kernel,
        grid=(m // tm, p_in // tn),
        in_specs=[
            pl.BlockSpec((tm, d_model), lambda i, j: (i, 0)),
            pl.BlockSpec((None, d_model, tn), lambda i, j: (layer, 0, j)),
        ],
        out_specs=[
            pl.BlockSpec((tm, tn), out_map(0, n_cg)),
            pl.BlockSpec((tm, tn), out_map(bounds[0], n_sb)),
            pl.BlockSpec((tm, tn), out_map(bounds[1], n_sb)),
            pl.BlockSpec((tm, tn), out_map(bounds[2], n_sb)),
            pl.BlockSpec((tm, tn), out_map(bounds[3], n_gch)),
        ],
        out_shape=[
            jax.ShapeDtypeStruct((m, 2 * d_conf), F32),
            jax.ShapeDtypeStruct((m, d_sb), BF16),
            jax.ShapeDtypeStruct((m, d_sb), F32),
            jax.ShapeDtypeStruct((m, d_sb), F32),
            jax.ShapeDtypeStruct((m, 3 * d_sc), F32),
        ],
        scratch_shapes=[pltpu.VMEM((tm, d_model), BF16)],
        compiler_params=_cparams(("arbitrary", "arbitrary")),
        name="inproj",
    )(x, w_in)


def _mixer_prompt_kernel(val_ref, gate_ref, gb_ref, gc_ref, h_ref, cw_ref, cb_ref, lg_ref, lb_ref, sw_ref,
                         conf_ref, sc_ref, cst_ref, sst_ref, ext_ref, ext2_ref, shift_ref):
    t = pl.program_id(1)
    tt, c = val_ref.shape

    @pl.when(t == 0)
    def _():
        ext_ref[0:CONV_HIST, :] = jnp.zeros((CONV_HIST, c), F32)
        ext2_ref[0:SC_HIST, :] = jnp.zeros((SC_HIST, ext2_ref.shape[1]), F32)

    ext_ref[CONV_HIST:CONV_HIST + tt, :] = val_ref[...] * jax.nn.sigmoid(gate_ref[...])
    ext2_ref[SC_HIST:SC_HIST + tt, :] = gc_ref[...] * h_ref[...]

    shifted_rows = shift_ref.shape[1]
    for b in range(1, SUBLANES):
        shift_ref[b - 1] = ext_ref[b:b + shifted_rows, :]

    def tap_rows(start, rows):
        b, base = start % SUBLANES, start - start % SUBLANES
        if b == 0:
            return ext_ref[base:base + rows, :]
        return shift_ref[b - 1, base:base + rows, :]

    cb = cb_ref[...]
    lg = lg_ref[...]
    lb = lb_ref[...]
    conv_off = CONV_HIST - (CONF_WIDTH - 1)
    sc_off = SC_HIST - (SC_WIDTH - 1)
    for ch in range(tt // MIX_RC):
        r0 = ch * MIX_RC
        acc = jnp.zeros((MIX_RC, c), F32)
        for w in range(CONF_WIDTH):
            acc = acc + tap_rows(r0 + conv_off + w, MIX_RC) * cw_ref[w:w + 1, :]
        yn = _layer_norm(acc + cb, lg, lb)
        conf_ref[r0:r0 + MIX_RC, :] = (yn * jax.nn.sigmoid(yn)).astype(conf_ref.dtype)
        s = jnp.zeros((MIX_RC, ext2_ref.shape[1]), F32)
        for w in range(SC_WIDTH):
            s = s + ext2_ref[r0 + sc_off + w:r0 + sc_off + w + MIX_RC, :] * sw_ref[w:w + 1, :]
        sc_ref[r0:r0 + MIX_RC, :] = (gb_ref[r0:r0 + MIX_RC, :] * s).astype(sc_ref.dtype)

    ext_ref[0:CONV_HIST, :] = ext_ref[tt:tt + CONV_HIST, :]
    ext2_ref[0:SC_HIST, :] = ext2_ref[tt:tt + SC_HIST, :]

    @pl.when(t == pl.num_programs(1) - 1)
    def _():
        cst_ref[...] = ext_ref[tt:tt + CONV_HIST, :]
        sst_ref[...] = ext2_ref[tt:tt + SC_HIST, :]


def _mixer_prompt(cg, gch, batch, seq, layer, conv_w, conv_b, conv_ln_g, conv_ln_b, sconv_w):
    d_conf = cg.shape[1] // 2
    d_sc = gch.shape[1] // 3
    tt = MIX_T
    nt = seq // tt
    row = lambda b, t: (b * nt + t, 0)
    col = lambda cidx: (lambda b, t: (b * nt + t, cidx))
    par3 = lambda b, t: (layer, 0, 0)
    return pl.pallas_call(
        _mixer_prompt_kernel,
        grid=(batch, nt),
        in_specs=[
            pl.BlockSpec((tt, d_conf), col(0)),
            pl.BlockSpec((tt, d_conf), col(1)),
            pl.BlockSpec((tt, d_sc), col(0)),
            pl.BlockSpec((tt, d_sc), col(1)),
            pl.BlockSpec((tt, d_sc), col(2)),
            pl.BlockSpec((None, CONF_WIDTH, d_conf), par3),
            pl.BlockSpec((None, 1, d_conf), par3),
            pl.BlockSpec((None, 1, d_conf), par3),
            pl.BlockSpec((None, 1, d_conf), par3),
            pl.BlockSpec((None, SC_WIDTH, d_sc), par3),
        ],
        out_specs=[
            pl.BlockSpec((tt, d_conf), row),
            pl.BlockSpec((tt, d_sc), row),
            pl.BlockSpec((None, CONV_HIST, d_conf), lambda b, t: (b, 0, 0)),
            pl.BlockSpec((None, SC_HIST, d_sc), lambda b, t: (b, 0, 0)),
        ],
        out_shape=[
            jax.ShapeDtypeStruct((batch * seq, d_conf), BF16),
            jax.ShapeDtypeStruct((batch * seq, d_sc), BF16),
            jax.ShapeDtypeStruct((batch, CONV_HIST, d_conf), F32),
            jax.ShapeDtypeStruct((batch, SC_HIST, d_sc), F32),
        ],
        scratch_shapes=[pltpu.VMEM((tt + CONV_HIST, d_conf), F32),
                        pltpu.VMEM((tt + SC_HIST, d_sc), F32),
                        pltpu.VMEM((SUBLANES - 1, tt + CONV_HIST - SUBLANES, d_conf), F32)],
        compiler_params=_cparams(("arbitrary", "arbitrary")),
        name="mixer_prompt",
    )(cg, cg, gch, gch, gch, conv_w, conv_b, conv_ln_g, conv_ln_b, sconv_w)


def _mixer_sample_kernel(cg_ref, gch_ref, cs_ref, ss_ref, cw_ref, cb_ref, lg_ref, lb_ref, sw_ref,
                         conf_ref, sc_ref, ncs_ref, nss_ref):
    c = cs_ref.shape[-1]
    c2 = ss_ref.shape[-1]
    hist = CONF_WIDTH - 1
    glu = cg_ref[:, 0:c] * jax.nn.sigmoid(cg_ref[:, c:2 * c])
    acc = glu * cw_ref[hist:hist + 1, :]
    for w in range(hist):
        acc = acc + cs_ref[w] * cw_ref[w:w + 1, :]
    yn = _layer_norm(acc + cb_ref[...], lg_ref[...], lb_ref[...])
    conf_ref[...] = (yn * jax.nn.sigmoid(yn)).astype(conf_ref.dtype)

    p = gch_ref[:, c2:2 * c2] * gch_ref[:, 2 * c2:3 * c2]
    s = p * sw_ref[SC_WIDTH - 1:SC_WIDTH, :]
    for w in range(SC_WIDTH - 1):
        s = s + ss_ref[w] * sw_ref[w:w + 1, :]
    sc_ref[...] = (gch_ref[:, 0:c2] * s).astype(sc_ref.dtype)

    for w in range(hist - 1):
        ncs_ref[w] = cs_ref[w + 1]
    ncs_ref[hist - 1] = glu
    for w in range(SC_WIDTH - 2):
        nss_ref[w] = ss_ref[w + 1]
    nss_ref[SC_WIDTH - 2] = p


def _mixer_sample(cg, gch, cs_t, ss_t, layer, conv_w, conv_b, conv_ln_g, conv_ln_b, sconv_w):
    nb = cg.shape[0]
    d_conf = cg.shape[1] // 2
    d_sc = gch.shape[1] // 3
    full2 = lambda i: (0, 0)
    par3 = lambda i: (layer, 0, 0)
    st4 = lambda i: (layer, 0, 0, 0)
    return pl.pallas_call(
        _mixer_sample_kernel,
        grid=(1,),
        in_specs=[
            pl.BlockSpec(cg.shape, full2),
            pl.BlockSpec(gch.shape, full2),
            pl.BlockSpec((None,) + cs_t.shape[1:], st4),
            pl.BlockSpec((None,) + ss_t.shape[1:], st4),
            pl.BlockSpec((None, CONF_WIDTH, d_conf), par3),
            pl.BlockSpec((None, 1, d_conf), par3),
            pl.BlockSpec((None, 1, d_conf), par3),
            pl.BlockSpec((None, 1, d_conf), par3),
            pl.BlockSpec((None, SC_WIDTH, d_sc), par3),
        ],
        out_specs=[
            pl.BlockSpec((nb, d_conf), full2),
            pl.BlockSpec((nb, d_sc), full2),
            pl.BlockSpec(cs_t.shape[1:], lambda i: (0, 0, 0)),
            pl.BlockSpec(ss_t.shape[1:], lambda i: (0, 0, 0)),
        ],
        out_shape=[
            jax.ShapeDtypeStruct((nb, d_conf), BF16),
            jax.ShapeDtypeStruct((nb, d_sc), BF16),
            jax.ShapeDtypeStruct(cs_t.shape[1:], F32),
            jax.ShapeDtypeStruct(ss_t.shape[1:], F32),
        ],
        compiler_params=_cparams(("arbitrary",)),
        name="mixer_sample",
    )(cg, gch, cs_t, ss_t, conv_w, conv_b, conv_ln_g, conv_ln_b, sconv_w)


def _attn_prompt_kernel(bias_ref, q_ref, k_ref, v_ref, tri_ref, o_ref, kb_ref, vt_ref):
    h = pl.program_id(1)
    i = pl.program_id(2)
    tq = q_ref.shape[0]
    tk = tri_ref.shape[0]
    ratio = tq // tk

    @pl.when(i == 0)
    def _():
        kb_ref[...] = k_ref[...].astype(BF16)
        for kb in range(vt_ref.shape[0]):
            vt_ref[kb] = v_ref[kb * tk:(kb + 1) * tk, :].T.astype(BF16)

    bias = bias_ref[h]
    q = q_ref[...]
    tri2 = tri_ref[...]

    def blocks(kbs, q0s, acc, run, masked):
        zs, keeps, splits, csums, avs = [], [], [], [], []
        for kb, q0 in zip(kbs, q0s):
            start = pl.multiple_of(kb * tk, tk)
            kblk = kb_ref[pl.ds(start, tk), :]
            zs.append(lax.dot_general(kblk, q[q0:, :], (((1,), (1,)), ((), ())),
                                      preferred_element_type=F32) + bias)
        for kb, q0, z in zip(kbs, q0s, zs):
            sp = _softplus2(z)
            if masked:
                keyp = kb * tk + lax.broadcasted_iota(jnp.int32, z.shape, 0)
                qryp = i * tq + q0 + lax.broadcasted_iota(jnp.int32, z.shape, 1)
                keeps.append(keyp < qryp)
                sp = jnp.where(keeps[-1], sp, 0.0)
            hi, lo = _split_bf16(sp)
            splits.append(jnp.concatenate([hi, lo], axis=0))
        for w in splits:
            csums.append(jnp.dot(tri2, w, preferred_element_type=F32))
        def from_q0(full, q0, tail):
            return tail if q0 == 0 else jnp.concatenate([full[:, :q0], tail], axis=1)

        for j, q0 in enumerate(q0s):
            csum = csums[j] + run[:, q0:]
            run = from_q0(run, q0, csum[0:1, :])
            a = jnp.exp2(zs[j] - csum)
            if masked:
                a = jnp.where(keeps[j], a, 0.0)
            avs.append(jnp.dot(vt_ref[kbs[j]], a.astype(BF16), preferred_element_type=F32))
        for q0, av in zip(q0s, avs):
            acc = from_q0(acc, q0, acc[:, q0:] + av)
        return acc, run

    acc = jnp.zeros((q_ref.shape[1], tq), F32)
    run = jnp.zeros((1, tq), F32)
    diag = list(reversed(range(ratio)))
    acc, run = blocks([i * ratio + c for c in diag], [c * tk for c in diag], acc, run, True)

    def body(jj, carry):
        first = i * ratio - 1 - jj * ATT_UNROLL
        return blocks([first - d for d in range(ATT_UNROLL)], [0] * ATT_UNROLL, carry[0], carry[1], False)

    acc, run = lax.fori_loop(0, i * ratio // ATT_UNROLL, body, (acc, run))
    o_ref[...] = acc.T.astype(o_ref.dtype)


def _attn_prompt(q, k, v, bias, batch, seq):
    d_sb = q.shape[1]
    heads = d_sb // SB_HEAD_DIM
    tq, tk = ATT_TQ, ATT_TK
    nq = seq // tq
    assert seq % tq == 0 and (tq // tk) % ATT_UNROLL == 0
    r = lax.broadcasted_iota(jnp.int32, (tk, tk), 0)
    c = lax.broadcasted_iota(jnp.int32, (tk, tk), 1)
    upper = (c >= r).astype(BF16)
    tri = jnp.concatenate([upper, upper], axis=1)
    grid_spec = pltpu.PrefetchScalarGridSpec(
        num_scalar_prefetch=0,
        grid=(batch, heads, nq),
        in_specs=[
            pl.BlockSpec(memory_space=pltpu.SMEM),
            pl.BlockSpec((tq, SB_HEAD_DIM), lambda b, h, i: (b * nq + i, h)),
            pl.BlockSpec((seq, SB_HEAD_DIM), lambda b, h, i: (b, h)),
            pl.BlockSpec((seq, SB_HEAD_DIM), lambda b, h, i: (b, h)),
            pl.BlockSpec((tk, 2 * tk), lambda b, h, i: (0, 0)),
        ],
        out_specs=pl.BlockSpec((tq, SB_HEAD_DIM), lambda b, h, i: (b * nq + i, h)),
        scratch_shapes=[pltpu.VMEM((seq, SB_HEAD_DIM), BF16),
                        pltpu.VMEM((seq // tk, SB_HEAD_DIM, tk), BF16)],
    )
    return pl.pallas_call(
        _attn_prompt_kernel,
        grid_spec=grid_spec,
        out_shape=jax.ShapeDtypeStruct((batch * seq, d_sb), BF16),
        compiler_params=_cparams(("arbitrary", "arbitrary", "arbitrary")),
        name="attn_prompt",
    )(bias, q, k, v, tri)


def _attn_sample_kernel(pps, pt_ref, qbd_ref, bias_ref, tri_ref, *refs):
    k_refs = refs[:pps]
    v_refs = refs[pps:2 * pps]
    o_ref = refs[2 * pps]
    acc_ref, run_ref = refs[2 * pps + 1:]
    s = pl.program_id(1)
    heads = acc_ref.shape[1] // k_refs[0].shape[1]
    page = k_refs[0].shape[0] // heads

    @pl.when(s == 0)
    def _():
        acc_ref[...] = jnp.zeros(acc_ref.shape, F32)
        run_ref[...] = jnp.zeros(run_ref.shape, F32)

    def flat_page(ref):
        return jnp.concatenate([ref[pl.ds(h, page, stride=heads), :] for h in range(heads)],
                               axis=1).astype(BF16)

    vc = jnp.concatenate([flat_page(r) for r in v_refs], axis=0)
    half = pps // 2
    z_halves = [jnp.dot(jnp.concatenate([flat_page(r) for r in k_refs[lo:lo + half]], axis=0), qbd_ref[...],
                        preferred_element_type=F32) + bias_ref[...] for lo in (0, half)]
    tri = tri_ref[...]
    run = run_ref[...]
    a_parts = []
    for r in range(pps):
        z = z_halves[r // half][(r % half) * page:(r % half + 1) * page, :]
        hi, lo = _split_bf16(_softplus2(z))
        csum = (jnp.dot(tri, hi, preferred_element_type=F32)
                + jnp.dot(tri, lo, preferred_element_type=F32)) + run
        a_parts.append(jnp.exp2(z - csum).astype(BF16))
        run = csum[0:1, :]
    run_ref[...] = run
    a_all = jnp.concatenate(a_parts, axis=0)
    acc_ref[...] += lax.dot_general(a_all, vc, (((0,), (0,)), ((), ())), preferred_element_type=F32)

    @pl.when(s == pl.num_programs(1) - 1)
    def _():
        top = acc_ref[0:heads, :]
        row = lax.broadcasted_iota(jnp.int32, top.shape, 0)
        col = lax.broadcasted_iota(jnp.int32, top.shape, 1)
        lo_c = row * SB_HEAD_DIM
        keep = (col >= lo_c) & (col < lo_c + SB_HEAD_DIM)
        o_ref[...] = jnp.sum(jnp.where(keep, top, 0.0), axis=0, keepdims=True).astype(o_ref.dtype)


def _attn_sample(q, bias, cache_k, cache_v, page_table, layer):
    nb, d_sb = q.shape
    heads = d_sb // SB_HEAD_DIM
    depth, n_pool, page = cache_k.shape[:3]
    n_pages = page_table.shape[1]
    pps = PAGES_PER_STEP
    assert n_pages % pps == 0 and heads == SUBLANES
    pt = page_table.reshape(-1)
    ck = cache_k.reshape(depth, n_pool, page * heads, SB_HEAD_DIM)
    cv = cache_v.reshape(depth, n_pool, page * heads, SB_HEAD_DIM)
    r = lax.broadcasted_iota(jnp.int32, (d_sb, LANES), 0) // SB_HEAD_DIM
    c = lax.broadcasted_iota(jnp.int32, (d_sb, LANES), 1)
    qbd = jnp.where((r == c)[None], q[:, :, None], jnp.zeros((), q.dtype))
    bias_row = jnp.zeros((1, LANES), F32).at[0, :heads].set(bias)
    rr = lax.broadcasted_iota(jnp.int32, (page, page), 0)
    cc = lax.broadcasted_iota(jnp.int32, (page, page), 1)
    tri = (cc >= rr).astype(BF16)

    def page_map(rk):
        def index_map(b, s, pt_ref):
            pg = pt_ref[b * n_pages + (n_pages - 1 - (s * pps + rk))]
            return (layer, pg, 0, 0)
        return index_map

    page_specs = [pl.BlockSpec((None, None, page * heads, SB_HEAD_DIM), page_map(rk)) for rk in range(pps)]
    grid_spec = pltpu.PrefetchScalarGridSpec(
        num_scalar_prefetch=1,
        grid=(nb, n_pages // pps),
        in_specs=[
            pl.BlockSpec((None, d_sb, LANES), lambda b, s, pt_ref: (b, 0, 0)),
            pl.BlockSpec((1, LANES), lambda b, s, pt_ref: (0, 0)),
            pl.BlockSpec((page, page), lambda b, s, pt_ref: (0, 0)),
        ] + page_specs + page_specs,
        out_specs=pl.BlockSpec((None, 1, d_sb), lambda b, s, pt_ref: (b, 0, 0)),
        scratch_shapes=[pltpu.VMEM((LANES, d_sb), F32), pltpu.VMEM((1, LANES), F32)],
    )
    out = pl.pallas_call(
        functools.partial(_attn_sample_kernel, pps),
        grid_spec=grid_spec,
        out_shape=jax.ShapeDtypeStruct((nb, 1, d_sb), BF16),
        compiler_params=_cparams(("arbitrary", "arbitrary")),
        name="attn_sample",
    )(pt, qbd, bias_row, tri, *([ck] * pps), *([cv] * pps))
    return out.reshape(nb, d_sb)


def _route(logits):
    lane = lax.broadcasted_iota(jnp.int32, logits.shape, 1).astype(F32)
    neg = -jnp.inf
    far = float(LANES)
    is_group = lane < N_GROUPS
    lg = jnp.where(is_group, logits, neg)
    gmax = jnp.max(lg, axis=1, keepdims=True)
    gidx = jnp.min(jnp.where(lg == gmax, lane, far), axis=1, keepdims=True)
    den = jnp.sum(jnp.where(is_group, jnp.exp(logits - gmax), 0.0), axis=1, keepdims=True)
    ggate = 1.0 / den
    first = N_GROUPS + gidx * EXPERTS_PER_GROUP
    le = jnp.where((lane >= first) & (lane < first + EXPERTS_PER_GROUP), logits, neg)
    v1 = jnp.max(le, axis=1, keepdims=True)
    i1 = jnp.min(jnp.where(le == v1, lane, far), axis=1, keepdims=True)
    le2 = jnp.where(lane == i1, neg, le)
    v2 = jnp.max(le2, axis=1, keepdims=True)
    i2 = jnp.min(jnp.where(le2 == v2, lane, far), axis=1, keepdims=True)
    e21 = jnp.exp(v2 - v1)
    p1 = 1.0 / (1.0 + e21)
    p2 = e21 * p1
    return lane, gidx, (i1 - N_GROUPS, ggate * p1), (i2 - N_GROUPS, ggate * p2)


def _comb(route):
    lane, _, (e1, w1), (e2, w2) = route
    return jnp.where(lane == e1, w1, jnp.where(lane == e2, w2, 0.0))


def _router_logits(x1, wrh_ref, wrl_ref, br_ref):
    hi, lo = _split_bf16(x1)
    wrh = wrh_ref[...]
    logits = (jnp.dot(hi, wrh, preferred_element_type=F32)
              + jnp.dot(lo, wrh, preferred_element_type=F32)
              + jnp.dot(hi, wrl_ref[...], preferred_element_type=F32)) + br_ref[...]
    return hi, logits


def _outproj_kernel(alpha, conf_ref, att_ref, sc_ref, x_ref, w_ref, g_ref, b_ref, wrh_ref, wrl_ref, br_ref,
                    x1_ref, x1b_ref, comb_ref):
    cat = jnp.concatenate([conf_ref[...], att_ref[...], sc_ref[...]], axis=1)
    mix = jnp.dot(cat, w_ref[...], preferred_element_type=F32)
    x1 = _layer_norm(alpha * x_ref[...] + mix, g_ref[...], b_ref[...])
    x1_ref[...] = x1
    hi, logits = _router_logits(x1, wrh_ref, wrl_ref, br_ref)
    x1b_ref[...] = hi
    comb_ref[...] = _comb(_route(logits))


def _outproj(conf, att, sc, x, w_out_b, layer, alpha, ln_g, ln_b, wr_hi, wr_lo, br):
    m, d_model = x.shape
    tm = min(OUT_TM, m)
    d_mix = w_out_b.shape[1]
    row = lambda i: (i, 0)
    par3 = lambda i: (layer, 0, 0)
    return pl.pallas_call(
        functools.partial(_outproj_kernel, alpha),
        grid=(m // tm,),
        in_specs=[
            pl.BlockSpec((tm, conf.shape[1]), row),
            pl.BlockSpec((tm, att.shape[1]), row),
            pl.BlockSpec((tm, sc.shape[1]), row),
            pl.BlockSpec((tm, d_model), row),
            pl.BlockSpec((None, d_mix, d_model), par3),
            pl.BlockSpec((None, 1, d_model), par3),
            pl.BlockSpec((None, 1, d_model), par3),
            pl.BlockSpec((None, d_model, LANES), par3),
            pl.BlockSpec((None, d_model, LANES), par3),
            pl.BlockSpec((None, 1, LANES), par3),
        ],
        out_specs=[
            pl.BlockSpec((tm, d_model), row),
            pl.BlockSpec((tm, d_model), row),
            pl.BlockSpec((tm, LANES), row),
        ],
        out_shape=[
            jax.ShapeDtypeStruct((m, d_model), F32),
            jax.ShapeDtypeStruct((m, d_model), BF16),
            jax.ShapeDtypeStruct((m, LANES), F32),
        ],
        compiler_params=_cparams(("arbitrary",)),
        name="outproj_router",
    )(conf, att, sc, x, w_out_b, ln_g, ln_b, wr_hi, wr_lo, br)


def _moe_kernel(alpha, xb_ref, x1_ref, comb_ref, wg_ref, wu_ref, wd_ref, g_ref, b_ref, o_ref, acc_ref):
    e = pl.program_id(1)

    @pl.when(e == 0)
    def _():
        acc_ref[...] = jnp.zeros(acc_ref.shape, F32)

    xb = xb_ref[...]
    gate = jnp.dot(xb, wg_ref[...].astype(BF16), preferred_element_type=F32)
    up = jnp.dot(xb, wu_ref[...].astype(BF16), preferred_element_type=F32)
    comb = comb_ref[...]
    lane = lax.broadcasted_iota(jnp.int32, comb.shape, 1)
    ce = jnp.sum(jnp.where(lane == e, comb, 0.0), axis=1, keepdims=True)
    hid = (gate * jax.nn.sigmoid(gate)) * up * ce
    acc_ref[...] += jnp.dot(hid.astype(BF16), wd_ref[...].astype(BF16), preferred_element_type=F32)

    @pl.when(e == pl.num_programs(1) - 1)
    def _():
        o_ref[...] = _layer_norm(alpha * x1_ref[...] + acc_ref[...], g_ref[...], b_ref[...])


def _moe(x1b, x1, comb, wg_b, wu_b, wd_b, layer, alpha, ln_g, ln_b):
    m, d_model = x1.shape
    n_exp, _, d_exp = wg_b.shape[1:]
    tm = min(MOE_TM, m)
    row = lambda i, e: (i, 0)
    par3 = lambda i, e: (layer, 0, 0)
    wmap = lambda i, e: (layer, e, 0, 0)
    return pl.pallas_call(
        functools.partial(_moe_kernel, alpha),
        grid=(m // tm, n_exp),
        in_specs=[
            pl.BlockSpec((tm, d_model), row),
            pl.BlockSpec((tm, d_model), row),
            pl.BlockSpec((tm, LANES), row),
            pl.BlockSpec((None, None, d_model, d_exp), wmap),
            pl.BlockSpec((None, None, d_model, d_exp), wmap),
            pl.BlockSpec((None, None, d_exp, d_model), wmap),
            pl.BlockSpec((None, 1, d_model), par3),
            pl.BlockSpec((None, 1, d_model), par3),
        ],
        out_specs=pl.BlockSpec((tm, d_model), row),
        out_shape=jax.ShapeDtypeStruct((m, d_model), F32),
        scratch_shapes=[pltpu.VMEM((tm, d_model), F32)],
        compiler_params=_cparams(("arbitrary", "arbitrary")),
        name="moe_dense",
    )(x1b, x1, comb, wg_b, wu_b, wd_b, ln_g, ln_b)


def _moe_layout(n_tokens):
    nt = n_tokens // DISP_TM
    max_rows = n_tokens + nt * (ROW_ALIGN - 1)
    cap = pl.cdiv(max_rows + DISP_TM + EXP_TM, EXP_TM) * EXP_TM
    t_max = pl.cdiv(max_rows, EXP_TM) + N_GROUPS
    return nt, cap, t_max


def _dispatch_kernel(alpha, cap, conf_ref, att_ref, sc_ref, x_ref, w_ref, g_ref, b_ref, wrh_ref, wrl_ref,
                     br_ref, ltri_ref, rows_in_ref, x1_ref, info_ref, tstart_ref, xs_ref, sbuf, cnt_ref, sem):
    del rows_in_ref
    i = pl.program_id(0)
    nt = pl.num_programs(0)
    slot = i % 2
    tm = x_ref.shape[0]

    @pl.when(i == 0)
    def _():
        for g in range(N_GROUPS):
            cnt_ref[g] = 0
        sbuf[...] = jnp.zeros(sbuf.shape, BF16)

    cat = jnp.concatenate([conf_ref[...], att_ref[...], sc_ref[...]], axis=1)
    mix = jnp.dot(cat, w_ref[...], preferred_element_type=F32)
    x1 = _layer_norm(alpha * x_ref[...] + mix, g_ref[...], b_ref[...])
    x1_ref[...] = x1
    hi, logits = _router_logits(x1, wrh_ref, wrl_ref, br_ref)
    lane, gidx, (e1, w1), (e2, w2) = _route(logits)

    first = gidx * EXPERTS_PER_GROUP
    l1, l2 = e1 - first, e2 - first
    w1h = w1.astype(BF16).astype(F32)
    w2h = w2.astype(BF16).astype(F32)
    gates = jnp.where(lane == l1, w1h, jnp.where(lane == l2, w2h, 0.0))
    gates = jnp.where(lane == l1 + EXPERTS_PER_GROUP, w1 - w1h,
                      jnp.where(lane == l2 + EXPERTS_PER_GROUP, w2 - w2h, gates))
    xrow = jnp.concatenate([hi, gates.astype(BF16)], axis=1)

    onehot = jnp.where(lane == gidx, 1.0, 0.0)
    before = jnp.dot(ltri_ref[...], onehot.astype(BF16), preferred_element_type=F32)
    rank = jnp.sum(jnp.where(lane == gidx, before, 0.0), axis=1, keepdims=True)
    lens = (before[tm - 1:tm, :] + onehot[tm - 1:tm, :]).astype(jnp.int32)
    info_ref[...] = jnp.where(lane == 0.0, gidx, jnp.where(lane == 1.0, rank, 0.0))

    lbase = jnp.int32(0)
    lbases, pads = [], []
    dest = rank
    for g in range(N_GROUPS):
        pad = ((lens[0, g] + (ROW_ALIGN - 1)) // ROW_ALIGN) * ROW_ALIGN
        lbases.append(lbase)
        pads.append(pad)
        dest = dest + jnp.where(gidx == float(g), lbase.astype(F32), 0.0)
        lbase = lbase + pad
    col = lax.broadcasted_iota(jnp.int32, (tm, SORT_ROWS), 1).astype(F32)
    onehot_dest = jnp.where(col == dest, 1.0, 0.0).astype(BF16)
    srt = lax.dot_general(onehot_dest, xrow, (((0,), (0,)), ((), ())), preferred_element_type=F32)
    sbuf[slot, 0:SORT_ROWS, :] = srt.astype(BF16)

    def copy(sl, g, src, dst):
        return pltpu.make_async_copy(
            sbuf.at[sl, pl.ds(pl.multiple_of(src, ROW_ALIGN), tm)],
            xs_ref.at[pl.ds(pl.multiple_of(dst, ROW_ALIGN), tm)], sem.at[sl, g])

    @pl.when(i > 0)
    def _():
        for g in range(N_GROUPS):
            copy(1 - slot, g, 0, 0).wait()

    for g in range(N_GROUPS):
        start = cnt_ref[g]
        copy(slot, g, lbases[g], g * cap + start).start()
        tstart_ref[i, g] = start
        cnt_ref[g] = start + pads[g]

    @pl.when(i == nt - 1)
    def _():
        for g in range(N_GROUPS):
            copy(slot, g, 0, 0).wait()
            tstart_ref[nt, g] = cnt_ref[g]


def _dispatch(conf, att, sc, x, w_out_b, layer, alpha, ln_g, ln_b, wr_hi, wr_lo, br, rows_buf):
    m, d_model = x.shape
    tm = DISP_TM
    nt, cap, _ = _moe_layout(m)
    d_mix = w_out_b.shape[1]
    width = d_model + LANES
    rr = lax.broadcasted_iota(jnp.int32, (tm, tm), 0)
    cc = lax.broadcasted_iota(jnp.int32, (tm, tm), 1)
    ltri = (cc < rr).astype(BF16)
    row = lambda i: (i, 0)
    par3 = lambda i: (layer, 0, 0)
    return pl.pallas_call(
        functools.partial(_dispatch_kernel, alpha, cap),
        grid=(nt,),
        in_specs=[
            pl.BlockSpec((tm, conf.shape[1]), row),
            pl.BlockSpec((tm, att.shape[1]), row),
            pl.BlockSpec((tm, sc.shape[1]), row),
            pl.BlockSpec((tm, d_model), row),
            pl.BlockSpec((None, d_mix, d_model), par3),
            pl.BlockSpec((None, 1, d_model), par3),
            pl.BlockSpec((None, 1, d_model), par3),
            pl.BlockSpec((None, d_model, LANES), par3),
            pl.BlockSpec((None, d_model, LANES), par3),
            pl.BlockSpec((None, 1, LANES), par3),
            pl.BlockSpec((tm, tm), lambda i: (0, 0)),
            pl.BlockSpec(memory_space=pl.ANY),
        ],
        out_specs=[
            pl.BlockSpec((tm, d_model), row),
            pl.BlockSpec((tm, LANES), row),
            pl.BlockSpec(memory_space=pltpu.SMEM),
            pl.BlockSpec(memory_space=pl.ANY),
        ],
        input_output_aliases={11: 3},
        out_shape=[
            jax.ShapeDtypeStruct((m, d_model), F32),
            jax.ShapeDtypeStruct((m, LANES), F32),
            jax.ShapeDtypeStruct((nt + 1, N_GROUPS), jnp.int32),
            jax.ShapeDtypeStruct((N_GROUPS * cap, width), BF16),
        ],
        scratch_shapes=[
            pltpu.VMEM((2, SORT_ROWS + tm, width), BF16),
            pltpu.SMEM((N_GROUPS,), jnp.int32),
            pltpu.SemaphoreType.DMA((2, N_GROUPS)),
        ],
        compiler_params=_cparams(("arbitrary",)),
        name="dispatch",
    )(conf, att, sc, x, w_out_b, ln_g, ln_b, wr_hi, wr_lo, br, ltri, rows_buf)


def _expert_kernel(cpe, tg_ref, tb_ref, tn_ref, tot_ref, xs_ref, wg_ref, wu_ref, wd_ref, ys_ref,
                   acc_ref, wgb_ref, wub_ref, wdb_ref):
    t = pl.program_id(0)
    c = pl.program_id(1)
    nc = EXPERTS_PER_GROUP * cpe
    d_model = wg_ref.shape[0]

    @pl.when(t < tot_ref[0])
    def _():
        @pl.when(c == 0)
        def _():
            acc_ref[...] = jnp.zeros(acc_ref.shape, F32)

        wgb_ref[...] = wg_ref[...].astype(BF16)
        wub_ref[...] = wu_ref[...].astype(BF16)
        wdb_ref[...] = wd_ref[...].astype(BF16)
        e_local = c // cpe

        def sub(s, carry):
            r0 = pl.multiple_of(s * EXP_SUB, EXP_SUB)
            xb = xs_ref[pl.ds(r0, EXP_SUB), 0:d_model]
            gates = xs_ref[pl.ds(r0, EXP_SUB), d_model:d_model + LANES].astype(F32)
            lane = lax.broadcasted_iota(jnp.int32, gates.shape, 1)
            mine = (lane == e_local) | (lane == e_local + EXPERTS_PER_GROUP)
            ce = jnp.sum(jnp.where(mine, gates, 0.0), axis=1, keepdims=True)
            gate = jnp.dot(xb, wgb_ref[...], preferred_element_type=F32)
            up = jnp.dot(xb, wub_ref[...], preferred_element_type=F32)
            hid = (gate * jax.nn.sigmoid(gate)) * up * ce
            acc_ref[pl.ds(r0, EXP_SUB), :] += jnp.dot(hid.astype(BF16), wdb_ref[...],
                                                      preferred_element_type=F32)
            return carry

        lax.fori_loop(0, tn_ref[t], sub, 0)

        @pl.when(c == nc - 1)
        def _():
            ys_ref[:, 0:d_model] = acc_ref[...].astype(ys_ref.dtype)
            ys_ref[:, d_model:d_model + LANES] = xs_ref[:, d_model:d_model + LANES]


def _experts(xs, tile_g, tile_blk, tile_nsub, total, wg, wu, wd, layer, cap):
    width = xs.shape[1]
    d_model = width - LANES
    d_exp = wg.shape[-1]
    t_max = tile_g.shape[0]
    cpe = d_exp // EXP_HC
    nc = EXPERTS_PER_GROUP * cpe
    blocks_per_group = cap // EXP_TM

    def chunk(t, c, tot):
        return jnp.where(t < tot[0], c, nc - 1)

    def rows(t, c, tg, tb, tn, tot):
        return (tg[t] * blocks_per_group + tb[t], 0)

    def w_in_map(t, c, tg, tb, tn, tot):
        ce = chunk(t, c, tot)
        return (layer, tg[t] * EXPERTS_PER_GROUP + ce // cpe, 0, ce % cpe)

    def w_out_map(t, c, tg, tb, tn, tot):
        ce = chunk(t, c, tot)
        return (layer, tg[t] * EXPERTS_PER_GROUP + ce // cpe, ce % cpe, 0)

    grid_spec = pltpu.PrefetchScalarGridSpec(
        num_scalar_prefetch=4,
        grid=(t_max, nc),
        in_specs=[
            pl.BlockSpec((EXP_TM, width), rows),
            pl.BlockSpec((None, None, d_model, EXP_HC), w_in_map),
            pl.BlockSpec((None, None, d_model, EXP_HC), w_in_map),
            pl.BlockSpec((None, None, EXP_HC, d_model), w_out_map),
        ],
        out_specs=pl.BlockSpec((EXP_TM, width), rows),
        scratch_shapes=[
            pltpu.VMEM((EXP_TM, d_model), F32),
            pltpu.VMEM((d_model, EXP_HC), BF16),
            pltpu.VMEM((d_model, EXP_HC), BF16),
            pltpu.VMEM((EXP_HC, d_model), BF16),
        ],
    )
    return pl.pallas_call(
        functools.partial(_expert_kernel, cpe),
        grid_spec=grid_spec,
        out_shape=jax.ShapeDtypeStruct(xs.shape, xs.dtype),
        input_output_aliases={4: 0},
        compiler_params=_cparams(("arbitrary", "arbitrary")),
        name="experts",
    )(tile_g, tile_blk, tile_nsub, total, xs, wg, wu, wd)


def _combine_kernel(alpha, ts_ref, back_ref, x1_ref, info_ref, *refs):
    wins = refs[:N_GROUPS]
    g_ref, b_ref, o_ref = refs[N_GROUPS:]
    i = pl.program_id(0)
    tm = x1_ref.shape[0]
    info = info_ref[...]
    gidx = info[:, 0:1]
    rank = info[:, 1:2]
    col = lax.broadcasted_iota(jnp.int32, (tm, tm), 1).astype(F32)
    y = jnp.zeros(x1_ref.shape, F32)
    for g in range(N_GROUPS):
        row_in_window = rank + back_ref[i * N_GROUPS + g].astype(F32)
        pick = jnp.where((gidx == float(g)) & (col == row_in_window), 1.0, 0.0).astype(BF16)
        y = y + jnp.dot(pick, wins[g][...], preferred_element_type=F32)
    o_ref[...] = _layer_norm(alpha * x1_ref[...] + y, g_ref[...], b_ref[...])


def _combine(x1, info, tstart, ys, layer, alpha, ln_g, ln_b, cap):
    m, d_model = x1.shape
    tm = DISP_TM
    last = _tiles_per_group(tstart[-1]) * EXP_TM - tm
    ts = jnp.minimum(tstart, last[None, :]).reshape(-1)
    back = tstart.reshape(-1) - ts

    def window(g):
        def index_map(i, ts_ref, back_ref):
            return (pl.multiple_of(g * cap + ts_ref[i * N_GROUPS + g], ROW_ALIGN), 0)
        return pl.BlockSpec((pl.Element(tm), pl.Element(d_model)), index_map)

    row = lambda i, ts_ref, back_ref: (i, 0)
    par3 = lambda i, ts_ref, back_ref: (layer, 0, 0)
    grid_spec = pltpu.PrefetchScalarGridSpec(
        num_scalar_prefetch=2,
        grid=(m // tm,),
        in_specs=[pl.BlockSpec((tm, d_model), row), pl.BlockSpec((tm, LANES), row)]
        + [window(g) for g in range(N_GROUPS)]
        + [pl.BlockSpec((None, 1, d_model), par3), pl.BlockSpec((None, 1, d_model), par3)],
        out_specs=pl.BlockSpec((tm, d_model), row),
    )
    return pl.pallas_call(
        functools.partial(_combine_kernel, alpha),
        grid_spec=grid_spec,
        out_shape=jax.ShapeDtypeStruct((m, d_model), F32),
        compiler_params=_cparams(("arbitrary",)),
        name="combine",
    )(ts, back, x1, info, *([ys] * N_GROUPS), ln_g, ln_b)


def _tiles_per_group(ends):
    return jnp.maximum((ends + EXP_TM - 1) // EXP_TM, 1)


def _expert_tiles(tstart, t_max):
    ends = tstart[-1]
    per_group = _tiles_per_group(ends)
    cum = jnp.cumsum(per_group)
    total = cum[-1:]
    t = jnp.minimum(jnp.arange(t_max, dtype=jnp.int32), total[0] - 1)
    tile_g = jnp.sum((t[:, None] >= cum[None, :]).astype(jnp.int32), axis=1)
    tile_blk = t - (cum - per_group)[tile_g]
    valid = jnp.clip(ends[tile_g] - tile_blk * EXP_TM, 0, EXP_TM)
    tile_nsub = (valid + EXP_SUB - 1) // EXP_SUB
    return tile_g, tile_blk, tile_nsub.astype(jnp.int32), total.astype(jnp.int32)


def kernel(x_prompt, x_sample, cache_k, cache_v, state_conv, state_sconv, page_table, w_in, conv_w, conv_b, conv_ln_g, conv_ln_b, sconv_w, sb_bias, w_out, ln1_g, ln1_b, router_group_w, router_group_b, router_expert_w, router_expert_b, expert_w_gate, expert_w_up, expert_w_down, ln2_g, ln2_b):
    batch, seq, d_model = x_prompt.shape
    nb = x_sample.shape[0]
    depth = w_in.shape[0]
    d_conf = conv_w.shape[-1]
    d_sc = sconv_w.shape[-1]
    heads = sb_bias.shape[-1]
    d_sb = heads * SB_HEAD_DIM
    alpha = (2 * depth) ** 0.25

    row3 = lambda a: a.reshape(depth, 1, a.shape[-1])
    conv_b3, conv_ln_g3, conv_ln_b3 = row3(conv_b), row3(conv_ln_g), row3(conv_ln_b)
    ln1_g3, ln1_b3, ln2_g3, ln2_b3 = row3(ln1_g), row3(ln1_b), row3(ln2_g), row3(ln2_b)

    n_log = N_GROUPS + N_EXPERTS
    wr = jnp.concatenate(
        [router_group_w, router_expert_w.transpose(0, 2, 1, 3).reshape(depth, d_model, N_EXPERTS)], axis=-1)
    wr = jnp.pad(wr, ((0, 0), (0, 0), (0, LANES - n_log)))
    wr_hi = wr.astype(BF16)
    wr_lo = (wr - wr_hi.astype(F32)).astype(BF16)
    br = jnp.concatenate([router_group_b, router_expert_b.reshape(depth, N_EXPERTS)], axis=-1)
    br = jnp.pad(br, ((0, 0), (0, LANES - n_log))).reshape(depth, 1, LANES)

    w_in_b = w_in.astype(BF16)
    w_out_b = w_out.astype(BF16)
    experts = (expert_w_gate, expert_w_up, expert_w_down)
    _, cap, t_max = _moe_layout(batch * seq)
    rows = jnp.zeros((N_GROUPS * cap, d_model + LANES), BF16)

    cs_t = state_conv.transpose(0, 2, 1, 3)
    ss_t = state_sconv.transpose(0, 2, 1, 3)

    yp = x_prompt.reshape(batch * seq, d_model)
    ys = x_sample.reshape(nb, d_model)
    kp, vp, cp, sp, kd, vd, cd, sd = [], [], [], [], [], [], [], []
    for l in range(depth):
        cg, q, k, v, gch = _inproj(yp, w_in_b, l, d_conf, d_sb, d_sc)
        conf, sc, cst, sst = _mixer_prompt(cg, gch, batch, seq, l, conv_w, conv_b3, conv_ln_g3, conv_ln_b3,
                                           sconv_w)
        att = _attn_prompt(q, k, v, sb_bias[l] * LOG2_E, batch, seq)
        x1, info, tstart, rows = _dispatch(conf, att, sc, yp, w_out_b, l, alpha, ln1_g3, ln1_b3, wr_hi, wr_lo, br,
                                           rows)
        rows = _experts(rows, *_expert_tiles(tstart, t_max), *experts, l, cap)
        yp = _combine(x1, info, tstart, rows, l, alpha, ln2_g3, ln2_b3, cap)
        kp.append(k.reshape(batch, seq, heads, SB_HEAD_DIM))
        vp.append(v.reshape(batch, seq, heads, SB_HEAD_DIM))
        cp.append(cst[:, CONV_HIST - (CONF_WIDTH - 1):, :])
        sp.append(sst[:, SC_HIST - (SC_WIDTH - 1):, :])

        cg, q, k, v, gch = _inproj(ys, w_in_b, l, d_conf, d_sb, d_sc)
        conf, sc, ncs, nss = _mixer_sample(cg, gch, cs_t, ss_t, l, conv_w, conv_b3, conv_ln_g3, conv_ln_b3,
                                           sconv_w)
        att = _attn_sample(q, sb_bias[l] * LOG2_E, cache_k, cache_v, page_table, l)
        x1, x1b, comb = _outproj(conf, att, sc, ys, w_out_b, l, alpha, ln1_g3, ln1_b3, wr_hi, wr_lo, br)
        ys = _moe(x1b, x1, comb, *experts, l, alpha, ln2_g3, ln2_b3)
        kd.append(k.reshape(nb, 1, heads, SB_HEAD_DIM))
        vd.append(v.reshape(nb, 1, heads, SB_HEAD_DIM))
        cd.append(ncs.transpose(1, 0, 2))
        sd.append(nss.transpose(1, 0, 2))

    return (yp.reshape(batch, seq, d_model), ys.reshape(nb, 1, d_model),
            jnp.stack(kp), jnp.stack(vp), jnp.stack(cp), jnp.stack(sp),
            jnp.stack(kd), jnp.stack(vd), jnp.stack(cd), jnp.stack(sd))
```

```python
import functools
import math

import jax
import jax.numpy as jnp
from jax import lax
from jax.experimental import pallas as pl
from jax.experimental.pallas import tpu as pltpu

F32 = jnp.float32
BF16 = jnp.bfloat16

SB_HEAD_DIM = 128
CONF_WIDTH = 31
SC_WIDTH = 3
N_GROUPS = 4
EXPERTS_PER_GROUP = 4
N_EXPERTS = N_GROUPS * EXPERTS_PER_GROUP
LN_EPS = 1e-5
LOG2_E = math.log2(math.e)

LANES = 128
SUBLANES = 8
VMEM_LIMIT_MB = 56

INPROJ_TM = 1024
INPROJ_TN = 512
MIX_T = 256
MIX_RC = 32
CONV_HIST = 32
SC_HIST = 8
ATT_TQ = 1024
ATT_TK = 256
ATT_UNROLL = 4
PAGES_PER_STEP = 8
DISP_TM = 256
ROW_ALIGN = 16
EXP_TM = 1280
EXP_SUB = 256
EXP_HC = 256


def _cparams(sem):
    return pltpu.CompilerParams(dimension_semantics=sem,
                                vmem_limit_bytes=VMEM_LIMIT_MB * 1024 * 1024)


def _layer_norm(y, g, b):
    mu = jnp.mean(y, axis=-1, keepdims=True)
    d = y - mu
    var = jnp.mean(d * d, axis=-1, keepdims=True)
    return d * lax.rsqrt(var + LN_EPS) * g + b


def _softplus2(z):
    return jnp.maximum(z, 0.0) + jnp.log2(1.0 + jnp.exp2(-jnp.abs(z)))


def _split_bf16(x):
    hi = x.astype(BF16)
    lo = (x - hi.astype(F32)).astype(BF16)
    return hi, lo


def _inproj_kernel(bounds, scale, x_ref, w_ref, cg_ref, q_ref, k_ref, v_ref, gch_ref, xb_ref):
    j = pl.program_id(1)

    @pl.when(j == 0)
    def _():
        xb_ref[...] = x_ref[...].astype(BF16)

    acc = jnp.dot(xb_ref[...], w_ref[...].astype(BF16), preferred_element_type=F32)
    b_cg, b_q, b_k, b_v = bounds

    @pl.when(j < b_cg)
    def _():
        cg_ref[...] = acc

    @pl.when((j >= b_cg) & (j < b_q))
    def _():
        q_ref[...] = (acc * scale).astype(BF16)

    @pl.when((j >= b_q) & (j < b_k))
    def _():
        k_ref[...] = acc

    @pl.when((j >= b_k) & (j < b_v))
    def _():
        v_ref[...] = acc

    @pl.when(j >= b_v)
    def _():
        gch_ref[...] = acc


def _inproj(x, w_in, layer, d_conf, d_sb, d_sc):
    m, d_model = x.shape
    p_in = w_in.shape[-1]
    tm = min(INPROJ_TM, m)
    tn = INPROJ_TN
    n_cg, n_sb, n_gch = 2 * d_conf // tn, d_sb // tn, 3 * d_sc // tn
    bounds = (n_cg, n_cg + n_sb, n_cg + 2 * n_sb, n_cg + 3 * n_sb)
    assert bounds[-1] + n_gch == p_in // tn and m % tm == 0

    def out_map(lo, n):
        return lambda i, j: (i, jnp.clip(j - lo, 0, n - 1))

    kernel = functools.partial(_inproj_kernel, bounds, SB_HEAD_DIM ** -0.5 * LOG2_E)
    return pl.pallas_call(
        kernel,
        grid=(m // tm, p_in // tn),
        in_specs=[
            pl.BlockSpec((tm, d_model), lambda i, j: (i, 0)),
            pl.BlockSpec((None, d_model, tn), lambda i, j: (layer, 0, j)),
        ],
        out_specs=[
            pl.BlockSpec((tm, tn), out_map(0, n_cg)),
            pl.BlockSpec((tm, tn), out_map(bounds[0], n_sb)),
            pl.BlockSpec((tm, tn), out_map(bounds[1], n_sb)),
            pl.BlockSpec((tm, tn), out_map(bounds[2], n_sb)),
            pl.BlockSpec((tm, tn), out_map(bounds[3], n_gch)),
        ],
        out_shape=[
            jax.ShapeDtypeStruct((m, 2 * d_conf), F32),
            jax.ShapeDtypeStruct((m, d_sb), BF16),
            jax.ShapeDtypeStruct((m, d_sb), F32),
            jax.ShapeDtypeStruct((m, d_sb), F32),
            jax.ShapeDtypeStruct((m, 3 * d_sc), F32),
        ],
        scratch_shapes=[pltpu.VMEM((tm, d_model), BF16)],
        compiler_params=_cparams(("arbitrary", "arbitrary")),
        name="inproj",
    )(x, w_in)


def _mixer_prompt_kernel(val_ref, gate_ref, gb_ref, gc_ref, h_ref, cw_ref, cb_ref, lg_ref, lb_ref, sw_ref,
                         conf_ref, sc_ref, cst_ref, sst_ref, ext_ref, ext2_ref, shift_ref):
    t = pl.program_id(1)
    tt, c = val_ref.shape

    @pl.when(t == 0)
    def _():
        ext_ref[0:CONV_HIST, :] = jnp.zeros((CONV_HIST, c), F32)
        ext2_ref[0:SC_HIST, :] = jnp.zeros((SC_HIST, ext2_ref.shape[1]), F32)

    ext_ref[CONV_HIST:CONV_HIST + tt, :] = val_ref[...] * jax.nn.sigmoid(gate_ref[...])
    ext2_ref[SC_HIST:SC_HIST + tt, :] = gc_ref[...] * h_ref[...]

    shifted_rows = shift_ref.shape[1]
    for b in range(1, SUBLANES):
        shift_ref[b - 1] = ext_ref[b:b + shifted_rows, :]

    def tap_rows(start, rows):
        b, base = start % SUBLANES, start - start % SUBLANES
        if b == 0:
            return ext_ref[base:base + rows, :]
        return shift_ref[b - 1, base:base + rows, :]

    cb = cb_ref[...]
    lg = lg_ref[...]
    lb = lb_ref[...]
    conv_off = CONV_HIST - (CONF_WIDTH - 1)
    sc_off = SC_HIST - (SC_WIDTH - 1)
    for ch in range(tt // MIX_RC):
        r0 = ch * MIX_RC
        acc = jnp.zeros((MIX_RC, c), F32)
        for w in range(CONF_WIDTH):
            acc = acc + tap_rows(r0 + conv_off + w, MIX_RC) * cw_ref[w:w + 1, :]
        yn = _layer_norm(acc + cb, lg, lb)
        conf_ref[r0:r0 + MIX_RC, :] = (yn * jax.nn.sigmoid(yn)).astype(conf_ref.dtype)
        s = jnp.zeros((MIX_RC, ext2_ref.shape[1]), F32)
        for w in range(SC_WIDTH):
            s = s + ext2_ref[r0 + sc_off + w:r0 + sc_off + w + MIX_RC, :] * sw_ref[w:w + 1, :]
        sc_ref[r0:r0 + MIX_RC, :] = (gb_ref[r0:r0 + MIX_RC, :] * s).astype(sc_ref.dtype)

    ext_ref[0:CONV_HIST, :] = ext_ref[tt:tt + CONV_HIST, :]
    ext2_ref[0:SC_HIST, :] = ext2_ref[tt:tt + SC_HIST, :]

    @pl.when(t == pl.num_programs(1) - 1)
    def _():
        cst_ref[...] = ext_ref[tt:tt + CONV_HIST, :]
        sst_ref[...] = ext2_ref[tt:tt + SC_HIST, :]


def _mixer_prompt(cg, gch, batch, seq, layer, conv_w, conv_b, conv_ln_g, conv_ln_b, sconv_w):
    d_conf = cg.shape[1] // 2
    d_sc = gch.shape[1] // 3
    tt = MIX_T
    nt = seq // tt
    row = lambda b, t: (b * nt + t, 0)
    col = lambda cidx: (lambda b, t: (b * nt + t, cidx))
    par3 = lambda b, t: (layer, 0, 0)
    return pl.pallas_call(
        _mixer_prompt_kernel,
        grid=(batch, nt),
        in_specs=[
            pl.BlockSpec((tt, d_conf), col(0)),
            pl.BlockSpec((tt, d_conf), col(1)),
            pl.BlockSpec((tt, d_sc), col(0)),
            pl.BlockSpec((tt, d_sc), col(1)),
            pl.BlockSpec((tt, d_sc), col(2)),
            pl.BlockSpec((None, CONF_WIDTH, d_conf), par3),
            pl.BlockSpec((None, 1, d_conf), par3),
            pl.BlockSpec((None, 1, d_conf), par3),
            pl.BlockSpec((None, 1, d_conf), par3),
            pl.BlockSpec((None, SC_WIDTH, d_sc), par3),
        ],
        out_specs=[
            pl.BlockSpec((tt, d_conf), row),
            pl.BlockSpec((tt, d_sc), row),
            pl.BlockSpec((None, CONV_HIST, d_conf), lambda b, t: (b, 0, 0)),
            pl.BlockSpec((None, SC_HIST, d_sc), lambda b, t: (b, 0, 0)),
        ],
        out_shape=[
            jax.ShapeDtypeStruct((batch * seq, d_conf), BF16),
            jax.ShapeDtypeStruct((batch * seq, d_sc), BF16),
            jax.ShapeDtypeStruct((batch, CONV_HIST, d_conf), F32),
            jax.ShapeDtypeStruct((batch, SC_HIST, d_sc), F32),
        ],
        scratch_shapes=[pltpu.VMEM((tt + CONV_HIST, d_conf), F32),
                        pltpu.VMEM((tt + SC_HIST, d_sc), F32),
                        pltpu.VMEM((SUBLANES - 1, tt + CONV_HIST - SUBLANES, d_conf), F32)],
        compiler_params=_cparams(("arbitrary", "arbitrary")),
        name="mixer_prompt",
    )(cg, cg, gch, gch, gch, conv_w, conv_b, conv_ln_g, conv_ln_b, sconv_w)


def _mixer_sample_kernel(cg_ref, gch_ref, cs_ref, ss_ref, cw_ref, cb_ref, lg_ref, lb_ref, sw_ref,
                         conf_ref, sc_ref, ncs_ref, nss_ref):
    c = cs_ref.shape[-1]
    c2 = ss_ref.shape[-1]
    hist = CONF_WIDTH - 1
    glu = cg_ref[:, 0:c] * jax.nn.sigmoid(cg_ref[:, c:2 * c])
    acc = glu * cw_ref[hist:hist + 1, :]
    for w in range(hist):
        acc = acc + cs_ref[w] * cw_ref[w:w + 1, :]
    yn = _layer_norm(acc + cb_ref[...], lg_ref[...], lb_ref[...])
    conf_ref[...] = (yn * jax.nn.sigmoid(yn)).astype(conf_ref.dtype)

    p = gch_ref[:, c2:2 * c2] * gch_ref[:, 2 * c2:3 * c2]
    s = p * sw_ref[SC_WIDTH - 1:SC_WIDTH, :]
    for w in range(SC_WIDTH - 1):
        s = s + ss_ref[w] * sw_ref[w:w + 1, :]
    sc_ref[...] = (gch_ref[:, 0:c2] * s).astype(sc_ref.dtype)

    for w in range(hist - 1):
        ncs_ref[w] = cs_ref[w + 1]
    ncs_ref[hist - 1] = glu
    for w in range(SC_WIDTH - 2):
        nss_ref[w] = ss_ref[w + 1]
    nss_ref[SC_WIDTH - 2] = p


def _mixer_sample(cg, gch, cs_t, ss_t, layer, conv_w, conv_b, conv_ln_g, conv_ln_b, sconv_w):
    nb = cg.shape[0]
    d_conf = cg.shape[1] // 2
    d_sc = gch.shape[1] // 3
    full2 = lambda i: (0, 0)
    par3 = lambda i: (layer, 0, 0)
    st4 = lambda i: (layer, 0, 0, 0)
    return pl.pallas_call(
        _mixer_sample_kernel,
        grid=(1,),
        in_specs=[
            pl.BlockSpec(cg.shape, full2),
            pl.BlockSpec(gch.shape, full2),
            pl.BlockSpec((None,) + cs_t.shape[1:], st4),
            pl.BlockSpec((None,) + ss_t.shape[1:], st4),
            pl.BlockSpec((None, CONF_WIDTH, d_conf), par3),
            pl.BlockSpec((None, 1, d_conf), par3),
            pl.BlockSpec((None, 1, d_conf), par3),
            pl.BlockSpec((None, 1, d_conf), par3),
            pl.BlockSpec((None, SC_WIDTH, d_sc), par3),
        ],
        out_specs=[
            pl.BlockSpec((nb, d_conf), full2),
            pl.BlockSpec((nb, d_sc), full2),
            pl.BlockSpec(cs_t.shape[1:], lambda i: (0, 0, 0)),
            pl.BlockSpec(ss_t.shape[1:], lambda i: (0, 0, 0)),
        ],
        out_shape=[
            jax.ShapeDtypeStruct((nb, d_conf), BF16),
            jax.ShapeDtypeStruct((nb, d_sc), BF16),
            jax.ShapeDtypeStruct(cs_t.shape[1:], F32),
            jax.ShapeDtypeStruct(ss_t.shape[1:], F32),
        ],
        compiler_params=_cparams(("arbitrary",)),
        name="mixer_sample",
    )(cg, gch, cs_t, ss_t, conv_w, conv_b, conv_ln_g, conv_ln_b, sconv_w)


def _attn_prompt_kernel(bias_ref, q_ref, k_ref, v_ref, tri_ref, o_ref, kb_ref, vt_ref):
    h = pl.program_id(1)
    i = pl.program_id(2)
    tq = q_ref.shape[0]
    tk = tri_ref.shape[0]
    ratio = tq // tk

    @pl.when(i == 0)
    def _():
        kb_ref[...] = k_ref[...].astype(BF16)
        for kb in range(vt_ref.shape[0]):
            vt_ref[kb] = v_ref[kb * tk:(kb + 1) * tk, :].T.astype(BF16)

    bias = bias_ref[h]
    q = q_ref[...]
    tri2 = tri_ref[...]

    def blocks(kbs, q0s, acc, run, masked):
        zs, keeps, splits, csums, avs = [], [], [], [], []
        for kb, q0 in zip(kbs, q0s):
            start = pl.multiple_of(kb * tk, tk)
            kblk = kb_ref[pl.ds(start, tk), :]
            zs.append(lax.dot_general(kblk, q[q0:, :], (((1,), (1,)), ((), ())),
                                      preferred_element_type=F32) + bias)
        for kb, q0, z in zip(kbs, q0s, zs):
            sp = _softplus2(z)
            if masked:
                keyp = kb * tk + lax.broadcasted_iota(jnp.int32, z.shape, 0)
                qryp = i * tq + q0 + lax.broadcasted_iota(jnp.int32, z.shape, 1)
                keeps.append(keyp < qryp)
                sp = jnp.where(keeps[-1], sp, 0.0)
            hi, lo = _split_bf16(sp)
            splits.append(jnp.concatenate([hi, lo], axis=0))
        for w in splits:
            csums.append(jnp.dot(tri2, w, preferred_element_type=F32))
        def from_q0(full, q0, tail):
            return tail if q0 == 0 else jnp.concatenate([full[:, :q0], tail], axis=1)

        for j, q0 in enumerate(q0s):
            csum = csums[j] + run[:, q0:]
            run = from_q0(run, q0, csum[0:1, :])
            a = jnp.exp2(zs[j] - csum)
            if masked:
                a = jnp.where(keeps[j], a, 0.0)
            avs.append(jnp.dot(vt_ref[kbs[j]], a.astype(BF16), preferred_element_type=F32))
        for q0, av in zip(q0s, avs):
            acc = from_q0(acc, q0, acc[:, q0:] + av)
        return acc, run

    acc = jnp.zeros((q_ref.shape[1], tq), F32)
    run = jnp.zeros((1, tq), F32)
    diag = list(reversed(range(ratio)))
    acc, run = blocks([i * ratio + c for c in diag], [c * tk for c in diag], acc, run, True)

    def body(jj, carry):
        first = i * ratio - 1 - jj * ATT_UNROLL
        return blocks([first - d for d in range(ATT_UNROLL)], [0] * ATT_UNROLL, carry[0], carry[1], False)

    acc, run = lax.fori_loop(0, i * ratio // ATT_UNROLL, body, (acc, run))
    o_ref[...] = acc.T.astype(o_ref.dtype)


def _attn_prompt(q, k, v, bias, batch, seq):
    d_sb = q.shape[1]
    heads = d_sb // SB_HEAD_DIM
    tq, tk = ATT_TQ, ATT_TK
    nq = seq // tq
    assert seq % tq == 0 and (tq // tk) % ATT_UNROLL == 0
    r = lax.broadcasted_iota(jnp.int32, (tk, tk), 0)
    c = lax.broadcasted_iota(jnp.int32, (tk, tk), 1)
    upper = (c >= r).astype(BF16)
    tri = jnp.concatenate([upper, upper], axis=1)
    grid_spec = pltpu.PrefetchScalarGridSpec(
        num_scalar_prefetch=0,
        grid=(batch, heads, nq),
        in_specs=[
            pl.BlockSpec(memory_space=pltpu.SMEM),
            pl.BlockSpec((tq, SB_HEAD_DIM), lambda b, h, i: (b * nq + i, h)),
            pl.BlockSpec((seq, SB_HEAD_DIM), lambda b, h, i: (b, h)),
            pl.BlockSpec((seq, SB_HEAD_DIM), lambda b, h, i: (b, h)),
            pl.BlockSpec((tk, 2 * tk), lambda b, h, i: (0, 0)),
        ],
        out_specs=pl.BlockSpec((tq, SB_HEAD_DIM), lambda b, h, i: (b * nq + i, h)),
        scratch_shapes=[pltpu.VMEM((seq, SB_HEAD_DIM), BF16),
                        pltpu.VMEM((seq // tk, SB_HEAD_DIM, tk), BF16)],
    )
    return pl.pallas_call(
        _attn_prompt_kernel,
        grid_spec=grid_spec,
        out_shape=jax.ShapeDtypeStruct((batch * seq, d_sb), BF16),
        compiler_params=_cparams(("arbitrary", "arbitrary", "arbitrary")),
        name="attn_prompt",
    )(bias, q, k, v, tri)


def _attn_sample_kernel(pps, pt_ref, qbd_ref, bias_ref, tri_ref, *refs):
    k_refs = refs[:pps]
    v_refs = refs[pps:2 * pps]
    o_ref = refs[2 * pps]
    acc_ref, run_ref = refs[2 * pps + 1:]
    s = pl.program_id(1)
    heads = acc_ref.shape[1] // k_refs[0].shape[1]
    page = k_refs[0].shape[0] // heads

    @pl.when(s == 0)
    def _():
        acc_ref[...] = jnp.zeros(acc_ref.shape, F32)
        run_ref[...] = jnp.zeros(run_ref.shape, F32)

    def flat_page(ref):
        return jnp.concatenate([ref[pl.ds(h, page, stride=heads), :] for h in range(heads)],
                               axis=1).astype(BF16)

    vc = jnp.concatenate([flat_page(r) for r in v_refs], axis=0)
    half = pps // 2
    z_halves = [jnp.dot(jnp.concatenate([flat_page(r) for r in k_refs[lo:lo + half]], axis=0), qbd_ref[...],
                        preferred_element_type=F32) + bias_ref[...] for lo in (0, half)]
    tri = tri_ref[...]
    run = run_ref[...]
    a_parts = []
    for r in range(pps):
        z = z_halves[r // half][(r % half) * page:(r % half + 1) * page, :]
        hi, lo = _split_bf16(_softplus2(z))
        csum = (jnp.dot(tri, hi, preferred_element_type=F32)
                + jnp.dot(tri, lo, preferred_element_type=F32)) + run
        a_parts.append(jnp.exp2(z - csum).astype(BF16))
        run = csum[0:1, :]
    run_ref[...] = run
    a_all = jnp.concatenate(a_parts, axis=0)
    acc_ref[...] += lax.dot_general(a_all, vc, (((0,), (0,)), ((), ())), preferred_element_type=F32)

    @pl.when(s == pl.num_programs(1) - 1)
    def _():
        top = acc_ref[0:heads, :]
        row = lax.broadcasted_iota(jnp.int32, top.shape, 0)
        col = lax.broadcasted_iota(jnp.int32, top.shape, 1)
        lo_c = row * SB_HEAD_DIM
        keep = (col >= lo_c) & (col < lo_c + SB_HEAD_DIM)
        o_ref[...] = jnp.sum(jnp.where(keep, top, 0.0), axis=0, keepdims=True).astype(o_ref.dtype)


def _attn_sample(q, bias, cache_k, cache_v, page_table, layer):
    nb, d_sb = q.shape
    heads = d_sb // SB_HEAD_DIM
    depth, n_pool, page = cache_k.shape[:3]
    n_pages = page_table.shape[1]
    pps = PAGES_PER_STEP
    assert n_pages % pps == 0 and heads == SUBLANES
    pt = page_table.reshape(-1)
    ck = cache_k.reshape(depth, n_pool, page * heads, SB_HEAD_DIM)
    cv = cache_v.reshape(depth, n_pool, page * heads, SB_HEAD_DIM)
    r = lax.broadcasted_iota(jnp.int32, (d_sb, LANES), 0) // SB_HEAD_DIM
    c = lax.broadcasted_iota(jnp.int32, (d_sb, LANES), 1)
    qbd = jnp.where((r == c)[None], q[:, :, None], jnp.zeros((), q.dtype))
    bias_row = jnp.zeros((1, LANES), F32).at[0, :heads].set(bias)
    rr = lax.broadcasted_iota(jnp.int32, (page, page), 0)
    cc = lax.broadcasted_iota(jnp.int32, (page, page), 1)
    tri = (cc >= rr).astype(BF16)

    def page_map(rk):
        def index_map(b, s, pt_ref):
            pg = pt_ref[b * n_pages + (n_pages - 1 - (s * pps + rk))]
            return (layer, pg, 0, 0)
        return index_map

    page_specs = [pl.BlockSpec((None, None, page * heads, SB_HEAD_DIM), page_map(rk)) for rk in range(pps)]
    grid_spec = pltpu.PrefetchScalarGridSpec(
        num_scalar_prefetch=1,
        grid=(nb, n_pages // pps),
        in_specs=[
            pl.BlockSpec((None, d_sb, LANES), lambda b, s, pt_ref: (b, 0, 0)),
            pl.BlockSpec((1, LANES), lambda b, s, pt_ref: (0, 0)),
            pl.BlockSpec((page, page), lambda b, s, pt_ref: (0, 0)),
        ] + page_specs + page_specs,
        out_specs=pl.BlockSpec((None, 1, d_sb), lambda b, s, pt_ref: (b, 0, 0)),
        scratch_shapes=[pltpu.VMEM((LANES, d_sb), F32), pltpu.VMEM((1, LANES), F32)],
    )
    out = pl.pallas_call(
        functools.partial(_attn_sample_kernel, pps),
        grid_spec=grid_spec,
        out_shape=jax.ShapeDtypeStruct((nb, 1, d_sb), BF16),
        compiler_params=_cparams(("arbitrary", "arbitrary")),
        name="attn_sample",
    )(pt, qbd, bias_row, tri, *([ck] * pps), *([cv] * pps))
    return out.reshape(nb, d_sb)


def _route(logits):
    lane = lax.broadcasted_iota(jnp.int32, logits.shape, 1).astype(F32)
    neg = -jnp.inf
    far = float(LANES)
    is_group = lane < N_GROUPS
    lg = jnp.where(is_group, logits, neg)
    gmax = jnp.max(lg, axis=1, keepdims=True)
    gidx = jnp.min(jnp.where(lg == gmax, lane, far), axis=1, keepdims=True)
    den = jnp.sum(jnp.where(is_group, jnp.exp(logits - gmax), 0.0), axis=1, keepdims=True)
    ggate = 1.0 / den
    first = N_GROUPS + gidx * EXPERTS_PER_GROUP
    le = jnp.where((lane >= first) & (lane < first + EXPERTS_PER_GROUP), logits, neg)
    v1 = jnp.max(le, axis=1, keepdims=True)
    i1 = jnp.min(jnp.where(le == v1, lane, far), axis=1, keepdims=True)
    le2 = jnp.where(lane == i1, neg, le)
    v2 = jnp.max(le2, axis=1, keepdims=True)
    i2 = jnp.min(jnp.where(le2 == v2, lane, far), axis=1, keepdims=True)
    e21 = jnp.exp(v2 - v1)
    p1 = 1.0 / (1.0 + e21)
    p2 = e21 * p1
    return lane, gidx, (i1 - N_GROUPS, ggate * p1), (i2 - N_GROUPS, ggate * p2)


def _router_logits(x1, wrh_ref, wrl_ref, br_ref):
    hi, lo = _split_bf16(x1)
    wrh = wrh_ref[...]
    logits = (jnp.dot(hi, wrh, preferred_element_type=F32)
              + jnp.dot(lo, wrh, preferred_element_type=F32)
              + jnp.dot(hi, wrl_ref[...], preferred_element_type=F32)) + br_ref[...]
    return hi, logits


def _disp_tile(n_tokens):
    return min(DISP_TM, n_tokens)


def _moe_layout(token_sets):
    tiles = sum(n // _disp_tile(n) for n in token_sets)
    max_rows = sum(token_sets) + tiles * (ROW_ALIGN - 1)
    cap = pl.cdiv(max_rows + DISP_TM + EXP_TM, EXP_TM) * EXP_TM
    t_max = pl.cdiv(max_rows, EXP_TM) + N_GROUPS
    return cap, t_max


def _sort_rows(tm):
    return pl.cdiv(tm + N_GROUPS * (ROW_ALIGN - 1), LANES) * LANES


def _dispatch_kernel(alpha, cap, conf_ref, att_ref, sc_ref, x_ref, w_ref, g_ref, b_ref, wrh_ref, wrl_ref,
                     br_ref, ltri_ref, cnt0_ref, rows_in_ref, x1_ref, info_ref, tstart_ref, xs_ref,
                     sbuf, cnt_ref, sem):
    del rows_in_ref
    i = pl.program_id(0)
    nt = pl.num_programs(0)
    slot = i % 2
    tm = x_ref.shape[0]
    sort_rows = sbuf.shape[1] - tm

    @pl.when(i == 0)
    def _():
        for g in range(N_GROUPS):
            cnt_ref[g] = cnt0_ref[g]
        sbuf[...] = jnp.zeros(sbuf.shape, BF16)

    cat = jnp.concatenate([conf_ref[...], att_ref[...], sc_ref[...]], axis=1)
    mix = jnp.dot(cat, w_ref[...], preferred_element_type=F32)
    x1 = _layer_norm(alpha * x_ref[...] + mix, g_ref[...], b_ref[...])
    x1_ref[...] = x1
    hi, logits = _router_logits(x1, wrh_ref, wrl_ref, br_ref)
    lane, gidx, (e1, w1), (e2, w2) = _route(logits)

    first = gidx * EXPERTS_PER_GROUP
    l1, l2 = e1 - first, e2 - first
    w1h = w1.astype(BF16).astype(F32)
    w2h = w2.astype(BF16).astype(F32)
    gates = jnp.where(lane == l1, w1h, jnp.where(lane == l2, w2h, 0.0))
    gates = jnp.where(lane == l1 + EXPERTS_PER_GROUP, w1 - w1h,
                      jnp.where(lane == l2 + EXPERTS_PER_GROUP, w2 - w2h, gates))
    xrow = jnp.concatenate([hi, gates.astype(BF16)], axis=1)

    onehot = jnp.where(lane == gidx, 1.0, 0.0)
    before = jnp.dot(ltri_ref[...], onehot.astype(BF16), preferred_element_type=F32)
    rank = jnp.sum(jnp.where(lane == gidx, before, 0.0), axis=1, keepdims=True)
    lens = (before[tm - 1:tm, :] + onehot[tm - 1:tm, :]).astype(jnp.int32)
    info_ref[...] = jnp.where(lane == 0.0, gidx, jnp.where(lane == 1.0, rank, 0.0))

    lbase = jnp.int32(0)
    lbases, pads = [], []
    dest = rank
    for g in range(N_GROUPS):
        pad = ((lens[0, g] + (ROW_ALIGN - 1)) // ROW_ALIGN) * ROW_ALIGN
        lbases.append(lbase)
        pads.append(pad)
        dest = dest + jnp.where(gidx == float(g), lbase.astype(F32), 0.0)
        lbase = lbase + pad
    col = lax.broadcasted_iota(jnp.int32, (tm, sort_rows), 1).astype(F32)
    onehot_dest = jnp.where(col == dest, 1.0, 0.0).astype(BF16)
    srt = lax.dot_general(onehot_dest, xrow, (((0,), (0,)), ((), ())), preferred_element_type=F32)
    sbuf[slot, 0:sort_rows, :] = srt.astype(BF16)

    def copy(sl, g, src, dst):
        return pltpu.make_async_copy(
            sbuf.at[sl, pl.ds(pl.multiple_of(src, ROW_ALIGN), tm)],
            xs_ref.at[pl.ds(pl.multiple_of(dst, ROW_ALIGN), tm)], sem.at[sl, g])

    @pl.when(i > 0)
    def _():
        for g in range(N_GROUPS):
            copy(1 - slot, g, 0, 0).wait()

    for g in range(N_GROUPS):
        start = cnt_ref[g]
        copy(slot, g, lbases[g], g * cap + start).start()
        tstart_ref[i, g] = start
        cnt_ref[g] = start + pads[g]

    @pl.when(i == nt - 1)
    def _():
        for g in range(N_GROUPS):
            copy(slot, g, 0, 0).wait()
            tstart_ref[nt, g] = cnt_ref[g]


def _dispatch(conf, att, sc, x, w_out_b, layer, alpha, ln_g, ln_b, wr_hi, wr_lo, br, rows_buf, cnt0, cap):
    m, d_model = x.shape
    tm = _disp_tile(m)
    nt = m // tm
    assert m % tm == 0 and tm % ROW_ALIGN == 0
    d_mix = w_out_b.shape[1]
    width = d_model + LANES
    rr = lax.broadcasted_iota(jnp.int32, (tm, tm), 0)
    cc = lax.broadcasted_iota(jnp.int32, (tm, tm), 1)
    ltri = (cc < rr).astype(BF16)
    row = lambda i: (i, 0)
    par3 = lambda i: (layer, 0, 0)
    return pl.pallas_call(
        functools.partial(_dispatch_kernel, alpha, cap),
        grid=(nt,),
        in_specs=[
            pl.BlockSpec((tm, conf.shape[1]), row),
            pl.BlockSpec((tm, att.shape[1]), row),
            pl.BlockSpec((tm, sc.shape[1]), row),
            pl.BlockSpec((tm, d_model), row),
            pl.BlockSpec((None, d_mix, d_model), par3),
            pl.BlockSpec((None, 1, d_model), par3),
            pl.BlockSpec((None, 1, d_model), par3),
            pl.BlockSpec((None, d_model, LANES), par3),
            pl.BlockSpec((None, d_model, LANES), par3),
            pl.BlockSpec((None, 1, LANES), par3),
            pl.BlockSpec((tm, tm), lambda i: (0, 0)),
            pl.BlockSpec(memory_space=pltpu.SMEM),
            pl.BlockSpec(memory_space=pl.ANY),
        ],
        out_specs=[
            pl.BlockSpec((tm, d_model), row),
            pl.BlockSpec((tm, LANES), row),
            pl.BlockSpec(memory_space=pltpu.SMEM),
            pl.BlockSpec(memory_space=pl.ANY),
        ],
        input_output_aliases={12: 3},
        out_shape=[
            jax.ShapeDtypeStruct((m, d_model), F32),
            jax.ShapeDtypeStruct((m, LANES), F32),
            jax.ShapeDtypeStruct((nt + 1, N_GROUPS), jnp.int32),
            jax.ShapeDtypeStruct(rows_buf.shape, rows_buf.dtype),
        ],
        scratch_shapes=[
            pltpu.VMEM((2, _sort_rows(tm) + tm, width), BF16),
            pltpu.SMEM((N_GROUPS,), jnp.int32),
            pltpu.SemaphoreType.DMA((2, N_GROUPS)),
        ],
        compiler_params=_cparams(("arbitrary",)),
        name="dispatch",
    )(conf, att, sc, x, w_out_b, ln_g, ln_b, wr_hi, wr_lo, br, ltri, cnt0, rows_buf)


def _expert_kernel(cpe, tg_ref, tb_ref, tn_ref, tot_ref, xs_ref, wg_ref, wu_ref, wd_ref, ys_ref,
                   acc_ref, wgb_ref, wub_ref, wdb_ref):
    t = pl.program_id(0)
    c = pl.program_id(1)
    nc = EXPERTS_PER_GROUP * cpe
    d_model = wg_ref.shape[0]

    @pl.when(t < tot_ref[0])
    def _():
        @pl.when(c == 0)
        def _():
            acc_ref[...] = jnp.zeros(acc_ref.shape, F32)

        wgb_ref[...] = wg_ref[...].astype(BF16)
        wub_ref[...] = wu_ref[...].astype(BF16)
        wdb_ref[...] = wd_ref[...].astype(BF16)
        e_local = c // cpe

        def sub(s, carry):
            r0 = pl.multiple_of(s * EXP_SUB, EXP_SUB)
            xb = xs_ref[pl.ds(r0, EXP_SUB), 0:d_model]
            gates = xs_ref[pl.ds(r0, EXP_SUB), d_model:d_model + LANES].astype(F32)
            lane = lax.broadcasted_iota(jnp.int32, gates.shape, 1)
            mine = (lane == e_local) | (lane == e_local + EXPERTS_PER_GROUP)
            ce = jnp.sum(jnp.where(mine, gates, 0.0), axis=1, keepdims=True)
            gate = jnp.dot(xb, wgb_ref[...], preferred_element_type=F32)
            up = jnp.dot(xb, wub_ref[...], preferred_element_type=F32)
            hid = (gate * jax.nn.sigmoid(gate)) * up * ce
            acc_ref[pl.ds(r0, EXP_SUB), :] += jnp.dot(hid.astype(BF16), wdb_ref[...],
                                                      preferred_element_type=F32)
            return carry

        lax.fori_loop(0, tn_ref[t], sub, 0)

        @pl.when(c == nc - 1)
        def _():
            ys_ref[:, 0:d_model] = acc_ref[...].astype(ys_ref.dtype)
            ys_ref[:, d_model:d_model + LANES] = xs_ref[:, d_model:d_model + LANES]


def _experts(xs, tile_g, tile_blk, tile_nsub, total, wg, wu, wd, layer, cap):
    width = xs.shape[1]
    d_model = width - LANES
    d_exp = wg.shape[-1]
    t_max = tile_g.shape[0]
    cpe = d_exp // EXP_HC
    nc = EXPERTS_PER_GROUP * cpe
    blocks_per_group = cap // EXP_TM

    def chunk(t, c, tot):
        return jnp.where(t < tot[0], c, nc - 1)

    def rows(t, c, tg, tb, tn, tot):
        return (tg[t] * blocks_per_group + tb[t], 0)

    def w_in_map(t, c, tg, tb, tn, tot):
        ce = chunk(t, c, tot)
        return (layer, tg[t] * EXPERTS_PER_GROUP + ce // cpe, 0, ce % cpe)

    def w_out_map(t, c, tg, tb, tn, tot):
        ce = chunk(t, c, tot)
        return (layer, tg[t] * EXPERTS_PER_GROUP + ce // cpe, ce % cpe, 0)

    grid_spec = pltpu.PrefetchScalarGridSpec(
        num_scalar_prefetch=4,
        grid=(t_max, nc),
        in_specs=[
            pl.BlockSpec((EXP_TM, width), rows),
            pl.BlockSpec((None, None, d_model, EXP_HC), w_in_map),
            pl.BlockSpec((None, None, d_model, EXP_HC), w_in_map),
            pl.BlockSpec((None, None, EXP_HC, d_model), w_out_map),
        ],
        out_specs=pl.BlockSpec((EXP_TM, width), rows),
        scratch_shapes=[
            pltpu.VMEM((EXP_TM, d_model), F32),
            pltpu.VMEM((d_model, EXP_HC), BF16),
            pltpu.VMEM((d_model, EXP_HC), BF16),
            pltpu.VMEM((EXP_HC, d_model), BF16),
        ],
    )
    return pl.pallas_call(
        functools.partial(_expert_kernel, cpe),
        grid_spec=grid_spec,
        out_shape=jax.ShapeDtypeStruct(xs.shape, xs.dtype),
        input_output_aliases={4: 0},
        compiler_params=_cparams(("arbitrary", "arbitrary")),
        name="experts",
    )(tile_g, tile_blk, tile_nsub, total, xs, wg, wu, wd)


def _combine_kernel(alpha, ts_ref, back_ref, x1_ref, info_ref, *refs):
    wins = refs[:N_GROUPS]
    g_ref, b_ref, o_ref = refs[N_GROUPS:]
    i = pl.program_id(0)
    tm = x1_ref.shape[0]
    info = info_ref[...]
    gidx = info[:, 0:1]
    rank = info[:, 1:2]
    col = lax.broadcasted_iota(jnp.int32, (tm, tm), 1).astype(F32)
    y = jnp.zeros(x1_ref.shape, F32)
    for g in range(N_GROUPS):
        row_in_window = rank + back_ref[i * N_GROUPS + g].astype(F32)
        pick = jnp.where((gidx == float(g)) & (col == row_in_window), 1.0, 0.0).astype(BF16)
        y = y + jnp.dot(pick, wins[g][...], preferred_element_type=F32)
    o_ref[...] = _layer_norm(alpha * x1_ref[...] + y, g_ref[...], b_ref[...])


def _combine(x1, info, tstart, ends, ys, layer, alpha, ln_g, ln_b, cap):
    m, d_model = x1.shape
    tm = _disp_tile(m)
    last = _tiles_per_group(ends) * EXP_TM - tm
    ts = jnp.minimum(tstart, last[None, :]).reshape(-1)
    back = tstart.reshape(-1) - ts

    def window(g):
        def index_map(i, ts_ref, back_ref):
            return (pl.multiple_of(g * cap + ts_ref[i * N_GROUPS + g], ROW_ALIGN), 0)
        return pl.BlockSpec((pl.Element(tm), pl.Element(d_model)), index_map)

    row = lambda i, ts_ref, back_ref: (i, 0)
    par3 = lambda i, ts_ref, back_ref: (layer, 0, 0)
    grid_spec = pltpu.PrefetchScalarGridSpec(
        num_scalar_prefetch=2,
        grid=(m // tm,),
        in_specs=[pl.BlockSpec((tm, d_model), row), pl.BlockSpec((tm, LANES), row)]
        + [window(g) for g in range(N_GROUPS)]
        + [pl.BlockSpec((None, 1, d_model), par3), pl.BlockSpec((None, 1, d_model), par3)],
        out_specs=pl.BlockSpec((tm, d_model), row),
    )
    return pl.pallas_call(
        functools.partial(_combine_kernel, alpha),
        grid_spec=grid_spec,
        out_shape=jax.ShapeDtypeStruct((m, d_model), F32),
        compiler_params=_cparams(("arbitrary",)),
        name="combine",
    )(ts, back, x1, info, *([ys] * N_GROUPS), ln_g, ln_b)


def _tiles_per_group(ends):
    return jnp.maximum((ends + EXP_TM - 1) // EXP_TM, 1)


def _expert_tiles(ends, t_max):
    per_group = _tiles_per_group(ends)
    cum = jnp.cumsum(per_group)
    total = cum[-1:]
    t = jnp.minimum(jnp.arange(t_max, dtype=jnp.int32), total[0] - 1)
    tile_g = jnp.sum((t[:, None] >= cum[None, :]).astype(jnp.int32), axis=1)
    tile_blk = t - (cum - per_group)[tile_g]
    valid = jnp.clip(ends[tile_g] - tile_blk * EXP_TM, 0, EXP_TM)
    tile_nsub = (valid + EXP_SUB - 1) // EXP_SUB
    return tile_g, tile_blk, tile_nsub.astype(jnp.int32), total.astype(jnp.int32)


def kernel(x_prompt, x_sample, cache_k, cache_v, state_conv, state_sconv, page_table, w_in, conv_w, conv_b, conv_ln_g, conv_ln_b, sconv_w, sb_bias, w_out, ln1_g, ln1_b, router_group_w, router_group_b, router_expert_w, router_expert_b, expert_w_gate, expert_w_up, expert_w_down, ln2_g, ln2_b):
    batch, seq, d_model = x_prompt.shape
    nb = x_sample.shape[0]
    depth = w_in.shape[0]
    d_conf = conv_w.shape[-1]
    d_sc = sconv_w.shape[-1]
    heads = sb_bias.shape[-1]
    d_sb = heads * SB_HEAD_DIM
    alpha = (2 * depth) ** 0.25

    row3 = lambda a: a.reshape(depth, 1, a.shape[-1])
    conv_b3, conv_ln_g3, conv_ln_b3 = row3(conv_b), row3(conv_ln_g), row3(conv_ln_b)
    ln1_g3, ln1_b3, ln2_g3, ln2_b3 = row3(ln1_g), row3(ln1_b), row3(ln2_g), row3(ln2_b)

    n_log = N_GROUPS + N_EXPERTS
    wr = jnp.concatenate(
        [router_group_w, router_expert_w.transpose(0, 2, 1, 3).reshape(depth, d_model, N_EXPERTS)], axis=-1)
    wr = jnp.pad(wr, ((0, 0), (0, 0), (0, LANES - n_log)))
    wr_hi = wr.astype(BF16)
    wr_lo = (wr - wr_hi.astype(F32)).astype(BF16)
    br = jnp.concatenate([router_group_b, router_expert_b.reshape(depth, N_EXPERTS)], axis=-1)
    br = jnp.pad(br, ((0, 0), (0, LANES - n_log))).reshape(depth, 1, LANES)

    w_out_b = w_out.astype(BF16)
    experts = (expert_w_gate, expert_w_up, expert_w_down)
    cap, t_max = _moe_layout((batch * seq, nb))
    rows = jnp.zeros((N_GROUPS * cap, d_model + LANES), BF16)
    no_rows = jnp.zeros((N_GROUPS,), jnp.int32)

    cs_t = state_conv.transpose(0, 2, 1, 3)
    ss_t = state_sconv.transpose(0, 2, 1, 3)

    yp = x_prompt.reshape(batch * seq, d_model)
    ys = x_sample.reshape(nb, d_model)
    kp, vp, cp, sp, kd, vd, cd, sd = [], [], [], [], [], [], [], []
    for l in range(depth):
        router = (w_out_b, l, alpha, ln1_g3, ln1_b3, wr_hi, wr_lo, br)
        cg, q, k, v, gch = _inproj(yp, w_in, l, d_conf, d_sb, d_sc)
        conf, sc, cst, sst = _mixer_prompt(cg, gch, batch, seq, l, conv_w, conv_b3, conv_ln_g3, conv_ln_b3,
                                           sconv_w)
        att = _attn_prompt(q, k, v, sb_bias[l] * LOG2_E, batch, seq)
        x1p, info_p, tstart_p, rows = _dispatch(conf, att, sc, yp, *router, rows, no_rows, cap)
        kp.append(k.reshape(batch, seq, heads, SB_HEAD_DIM))
        vp.append(v.reshape(batch, seq, heads, SB_HEAD_DIM))
        cp.append(cst[:, CONV_HIST - (CONF_WIDTH - 1):, :])
        sp.append(sst[:, SC_HIST - (SC_WIDTH - 1):, :])

        cg, q, k, v, gch = _inproj(ys, w_in, l, d_conf, d_sb, d_sc)
        conf, sc, ncs, nss = _mixer_sample(cg, gch, cs_t, ss_t, l, conv_w, conv_b3, conv_ln_g3, conv_ln_b3,
                                           sconv_w)
        att = _attn_sample(q, sb_bias[l] * LOG2_E, cache_k, cache_v, page_table, l)
        x1s, info_s, tstart_s, rows = _dispatch(conf, att, sc, ys, *router, rows, tstart_p[-1], cap)
        kd.append(k.reshape(nb, 1, heads, SB_HEAD_DIM))
        vd.append(v.reshape(nb, 1, heads, SB_HEAD_DIM))
        cd.append(ncs.transpose(1, 0, 2))
        sd.append(nss.transpose(1, 0, 2))

        ends = tstart_s[-1]
        rows = _experts(rows, *_expert_tiles(ends, t_max), *experts, l, cap)
        yp = _combine(x1p, info_p, tstart_p, ends, rows, l, alpha, ln2_g3, ln2_b3, cap)
        ys = _combine(x1s, info_s, tstart_s, ends, rows, l, alpha, ln2_g3, ln2_b3, cap)

    return (yp.reshape(batch, seq, d_model), ys.reshape(nb, 1, d_model),
            jnp.stack(kp), jnp.stack(vp), jnp.stack(cp), jnp.stack(sp),
            jnp.stack(kd), jnp.stack(vd), jnp.stack(cd), jnp.stack(sd))
```

```python
import functools
import math

import jax
import jax.numpy as jnp
from jax import lax
from jax.experimental import pallas as pl
from jax.experimental.pallas import tpu as pltpu

F32 = jnp.float32
BF16 = jnp.bfloat16

SB_HEAD_DIM = 128
CONF_WIDTH = 31
SC_WIDTH = 3
N_GROUPS = 4
EXPERTS_PER_GROUP = 4
N_EXPERTS = N_GROUPS * EXPERTS_PER_GROUP
LN_EPS = 1e-5
LOG2_E = math.log2(math.e)

LANES = 128
SUBLANES = 8
VMEM_LIMIT_MB = 56

INPROJ_TM = 1024
INPROJ_TN = 512
MIX_T = 256
MIX_RC = 32
CONV_HIST = 32
SC_HIST = 8
ATT_TQ = 1024
ATT_TK = 256
ATT_UNROLL = 4
PAGES_PER_STEP = 16
DISP_TM = 256
ROW_ALIGN = 16
EXP_TM = 1280
EXP_SUB = 256
EXP_HC = 256


def _cparams(sem):
    return pltpu.CompilerParams(dimension_semantics=sem,
                                vmem_limit_bytes=VMEM_LIMIT_MB * 1024 * 1024)


def _layer_norm(y, g, b):
    mu = jnp.mean(y, axis=-1, keepdims=True)
    d = y - mu
    var = jnp.mean(d * d, axis=-1, keepdims=True)
    return d * lax.rsqrt(var + LN_EPS) * g + b


def _softplus2(z):
    return jnp.maximum(z, 0.0) + jnp.log2(1.0 + jnp.exp2(-jnp.abs(z)))


def _split_bf16(x):
    hi = x.astype(BF16)
    lo = (x - hi.astype(F32)).astype(BF16)
    return hi, lo


def _inproj_kernel(bounds, scale, x_ref, w_ref, cg_ref, q_ref, k_ref, v_ref, gch_ref, xb_ref):
    j = pl.program_id(1)

    @pl.when(j == 0)
    def _():
        xb_ref[...] = x_ref[...].astype(BF16)

    acc = jnp.dot(xb_ref[...], w_ref[...].astype(BF16), preferred_element_type=F32)
    b_cg, b_q, b_k, b_v = bounds

    @pl.when(j < b_cg)
    def _():
        cg_ref[...] = acc

    @pl.when((j >= b_cg) & (j < b_q))
    def _():
        q_ref[...] = (acc * scale).astype(BF16)

    @pl.when((j >= b_q) & (j < b_k))
    def _():
        k_ref[...] = acc

    @pl.when((j >= b_k) & (j < b_v))
    def _():
        v_ref[...] = acc

    @pl.when(j >= b_v)
    def _():
        gch_ref[...] = acc


def _inproj(x, w_in, layer, d_conf, d_sb, d_sc):
    m, d_model = x.shape
    p_in = w_in.shape[-1]
    tm = min(INPROJ_TM, m)
    tn = INPROJ_TN
    n_cg, n_sb, n_gch = 2 * d_conf // tn, d_sb // tn, 3 * d_sc // tn
    bounds = (n_cg, n_cg + n_sb, n_cg + 2 * n_sb, n_cg + 3 * n_sb)
    assert bounds[-1] + n_gch == p_in // tn and m % tm == 0

    def out_map(lo, n):
        return lambda i, j: (i, jnp.clip(j - lo, 0, n - 1))

    kernel = functools.partial(_inproj_kernel, bounds, SB_HEAD_DIM ** -0.5 * LOG2_E)
    return pl.pallas_call(
        kernel,
        grid=(m // tm, p_in // tn),
        in_specs=[
            pl.BlockSpec((tm, d_model), lambda i, j: (i, 0)),
            pl.BlockSpec((None, d_model, tn), lambda i, j: (layer, 0, j)),
        ],
        out_specs=[
            pl.BlockSpec((tm, tn), out_map(0, n_cg)),
            pl.BlockSpec((tm, tn), out_map(bounds[0], n_sb)),
            pl.BlockSpec((tm, tn), out_map(bounds[1], n_sb)),
            pl.BlockSpec((tm, tn), out_map(bounds[2], n_sb)),
            pl.BlockSpec((tm, tn), out_map(bounds[3], n_gch)),
        ],
        out_shape=[
            jax.ShapeDtypeStruct((m, 2 * d_conf), F32),
            jax.ShapeDtypeStruct((m, d_sb), BF16),
            jax.ShapeDtypeStruct((m, d_sb), F32),
            jax.ShapeDtypeStruct((m, d_sb), F32),
            jax.ShapeDtypeStruct((m, 3 * d_sc), F32),
        ],
        scratch_shapes=[pltpu.VMEM((tm, d_model), BF16)],
        compiler_params=_cparams(("arbitrary", "arbitrary")),
        name="inproj",
    )(x, w_in)


def _mixer_prompt_kernel(val_ref, gate_ref, gb_ref, gc_ref, h_ref, cw_ref, cb_ref, lg_ref, lb_ref, sw_ref,
                         conf_ref, sc_ref, cst_ref, sst_ref, ext_ref, ext2_ref, shift_ref):
    t = pl.program_id(1)
    tt, c = val_ref.shape

    @pl.when(t == 0)
    def _():
        ext_ref[0:CONV_HIST, :] = jnp.zeros((CONV_HIST, c), F32)
        ext2_ref[0:SC_HIST, :] = jnp.zeros((SC_HIST, ext2_ref.shape[1]), F32)

    ext_ref[CONV_HIST:CONV_HIST + tt, :] = val_ref[...] * jax.nn.sigmoid(gate_ref[...])
    ext2_ref[SC_HIST:SC_HIST + tt, :] = gc_ref[...] * h_ref[...]

    shifted_rows = shift_ref.shape[1]
    for b in range(1, SUBLANES):
        shift_ref[b - 1] = ext_ref[b:b + shifted_rows, :]

    def tap_rows(start, rows):
        b, base = start % SUBLANES, start - start % SUBLANES
        if b == 0:
            return ext_ref[base:base + rows, :]
        return shift_ref[b - 1, base:base + rows, :]

    cb = cb_ref[...]
    lg = lg_ref[...]
    lb = lb_ref[...]
    conv_off = CONV_HIST - (CONF_WIDTH - 1)
    sc_off = SC_HIST - (SC_WIDTH - 1)
    for ch in range(tt // MIX_RC):
        r0 = ch * MIX_RC
        acc = jnp.zeros((MIX_RC, c), F32)
        for w in range(CONF_WIDTH):
            acc = acc + tap_rows(r0 + conv_off + w, MIX_RC) * cw_ref[w:w + 1, :]
        yn = _layer_norm(acc + cb, lg, lb)
        conf_ref[r0:r0 + MIX_RC, :] = (yn * jax.nn.sigmoid(yn)).astype(conf_ref.dtype)
        s = jnp.zeros((MIX_RC, ext2_ref.shape[1]), F32)
        for w in range(SC_WIDTH):
            s = s + ext2_ref[r0 + sc_off + w:r0 + sc_off + w + MIX_RC, :] * sw_ref[w:w + 1, :]
        sc_ref[r0:r0 + MIX_RC, :] = (gb_ref[r0:r0 + MIX_RC, :] * s).astype(sc_ref.dtype)

    ext_ref[0:CONV_HIST, :] = ext_ref[tt:tt + CONV_HIST, :]
    ext2_ref[0:SC_HIST, :] = ext2_ref[tt:tt + SC_HIST, :]

    @pl.when(t == pl.num_programs(1) - 1)
    def _():
        cst_ref[...] = ext_ref[tt:tt + CONV_HIST, :]
        sst_ref[...] = ext2_ref[tt:tt + SC_HIST, :]


def _mixer_prompt(cg, gch, batch, seq, layer, conv_w, conv_b, conv_ln_g, conv_ln_b, sconv_w):
    d_conf = cg.shape[1] // 2
    d_sc = gch.shape[1] // 3
    tt = MIX_T
    nt = seq // tt
    row = lambda b, t: (b * nt + t, 0)
    col = lambda cidx: (lambda b, t: (b * nt + t, cidx))
    par3 = lambda b, t: (layer, 0, 0)
    return pl.pallas_call(
        _mixer_prompt_kernel,
        grid=(batch, nt),
        in_specs=[
            pl.BlockSpec((tt, d_conf), col(0)),
            pl.BlockSpec((tt, d_conf), col(1)),
            pl.BlockSpec((tt, d_sc), col(0)),
            pl.BlockSpec((tt, d_sc), col(1)),
            pl.BlockSpec((tt, d_sc), col(2)),
            pl.BlockSpec((None, CONF_WIDTH, d_conf), par3),
            pl.BlockSpec((None, 1, d_conf), par3),
            pl.BlockSpec((None, 1, d_conf), par3),
            pl.BlockSpec((None, 1, d_conf), par3),
            pl.BlockSpec((None, SC_WIDTH, d_sc), par3),
        ],
        out_specs=[
            pl.BlockSpec((tt, d_conf), row),
            pl.BlockSpec((tt, d_sc), row),
            pl.BlockSpec((None, CONV_HIST, d_conf), lambda b, t: (b, 0, 0)),
            pl.BlockSpec((None, SC_HIST, d_sc), lambda b, t: (b, 0, 0)),
        ],
        out_shape=[
            jax.ShapeDtypeStruct((batch * seq, d_conf), BF16),
            jax.ShapeDtypeStruct((batch * seq, d_sc), BF16),
            jax.ShapeDtypeStruct((batch, CONV_HIST, d_conf), F32),
            jax.ShapeDtypeStruct((batch, SC_HIST, d_sc), F32),
        ],
        scratch_shapes=[pltpu.VMEM((tt + CONV_HIST, d_conf), F32),
                        pltpu.VMEM((tt + SC_HIST, d_sc), F32),
                        pltpu.VMEM((SUBLANES - 1, tt + CONV_HIST - SUBLANES, d_conf), F32)],
        compiler_params=_cparams(("arbitrary", "arbitrary")),
        name="mixer_prompt",
    )(cg, cg, gch, gch, gch, conv_w, conv_b, conv_ln_g, conv_ln_b, sconv_w)


def _mixer_sample_kernel(cg_ref, gch_ref, cs_ref, ss_ref, cw_ref, cb_ref, lg_ref, lb_ref, sw_ref,
                         conf_ref, sc_ref, ncs_ref, nss_ref):
    c = cs_ref.shape[-1]
    c2 = ss_ref.shape[-1]
    hist = CONF_WIDTH - 1
    glu = cg_ref[:, 0:c] * jax.nn.sigmoid(cg_ref[:, c:2 * c])
    acc = glu * cw_ref[hist:hist + 1, :]
    for w in range(hist):
        acc = acc + cs_ref[w] * cw_ref[w:w + 1, :]
    yn = _layer_norm(acc + cb_ref[...], lg_ref[...], lb_ref[...])
    conf_ref[...] = (yn * jax.nn.sigmoid(yn)).astype(conf_ref.dtype)

    p = gch_ref[:, c2:2 * c2] * gch_ref[:, 2 * c2:3 * c2]
    s = p * sw_ref[SC_WIDTH - 1:SC_WIDTH, :]
    for w in range(SC_WIDTH - 1):
        s = s + ss_ref[w] * sw_ref[w:w + 1, :]
    sc_ref[...] = (gch_ref[:, 0:c2] * s).astype(sc_ref.dtype)

    for w in range(hist - 1):
        ncs_ref[w] = cs_ref[w + 1]
    ncs_ref[hist - 1] = glu
    for w in range(SC_WIDTH - 2):
        nss_ref[w] = ss_ref[w + 1]
    nss_ref[SC_WIDTH - 2] = p


def _mixer_sample(cg, gch, cs_t, ss_t, layer, conv_w, conv_b, conv_ln_g, conv_ln_b, sconv_w):
    nb = cg.shape[0]
    d_conf = cg.shape[1] // 2
    d_sc = gch.shape[1] // 3
    full2 = lambda i: (0, 0)
    par3 = lambda i: (layer, 0, 0)
    st4 = lambda i: (layer, 0, 0, 0)
    return pl.pallas_call(
        _mixer_sample_kernel,
        grid=(1,),
        in_specs=[
            pl.BlockSpec(cg.shape, full2),
            pl.BlockSpec(gch.shape, full2),
            pl.BlockSpec((None,) + cs_t.shape[1:], st4),
            pl.BlockSpec((None,) + ss_t.shape[1:], st4),
            pl.BlockSpec((None, CONF_WIDTH, d_conf), par3),
            pl.BlockSpec((None, 1, d_conf), par3),
            pl.BlockSpec((None, 1, d_conf), par3),
            pl.BlockSpec((None, 1, d_conf), par3),
            pl.BlockSpec((None, SC_WIDTH, d_sc), par3),
        ],
        out_specs=[
            pl.BlockSpec((nb, d_conf), full2),
            pl.BlockSpec((nb, d_sc), full2),
            pl.BlockSpec(cs_t.shape[1:], lambda i: (0, 0, 0)),
            pl.BlockSpec(ss_t.shape[1:], lambda i: (0, 0, 0)),
        ],
        out_shape=[
            jax.ShapeDtypeStruct((nb, d_conf), BF16),
            jax.ShapeDtypeStruct((nb, d_sc), BF16),
            jax.ShapeDtypeStruct(cs_t.shape[1:], F32),
            jax.ShapeDtypeStruct(ss_t.shape[1:], F32),
        ],
        compiler_params=_cparams(("arbitrary",)),
        name="mixer_sample",
    )(cg, gch, cs_t, ss_t, conv_w, conv_b, conv_ln_g, conv_ln_b, sconv_w)


def _attn_prompt_kernel(bias_ref, q_ref, k_ref, v_ref, tri_ref, o_ref, kb_ref, vt_ref):
    h = pl.program_id(1)
    i = pl.program_id(2)
    tq = q_ref.shape[0]
    tk = tri_ref.shape[0]
    ratio = tq // tk

    @pl.when(i == 0)
    def _():
        kb_ref[...] = k_ref[...].astype(BF16)
        for kb in range(vt_ref.shape[0]):
            vt_ref[kb] = v_ref[kb * tk:(kb + 1) * tk, :].T.astype(BF16)

    bias = bias_ref[h]
    q = q_ref[...]
    tri2 = tri_ref[...]

    def blocks(kbs, q0s, acc, run, masked):
        zs, keeps, splits, csums, avs = [], [], [], [], []
        for kb, q0 in zip(kbs, q0s):
            start = pl.multiple_of(kb * tk, tk)
            kblk = kb_ref[pl.ds(start, tk), :]
            zs.append(lax.dot_general(kblk, q[q0:, :], (((1,), (1,)), ((), ())),
                                      preferred_element_type=F32) + bias)
        for kb, q0, z in zip(kbs, q0s, zs):
            sp = _softplus2(z)
            if masked:
                keyp = kb * tk + lax.broadcasted_iota(jnp.int32, z.shape, 0)
                qryp = i * tq + q0 + lax.broadcasted_iota(jnp.int32, z.shape, 1)
                keeps.append(keyp < qryp)
                sp = jnp.where(keeps[-1], sp, 0.0)
            hi, lo = _split_bf16(sp)
            splits.append(jnp.concatenate([hi, lo], axis=0))
        for w in splits:
            csums.append(jnp.dot(tri2, w, preferred_element_type=F32))
        def from_q0(full, q0, tail):
            return tail if q0 == 0 else jnp.concatenate([full[:, :q0], tail], axis=1)

        for j, q0 in enumerate(q0s):
            csum = csums[j] + run[:, q0:]
            run = from_q0(run, q0, csum[0:1, :])
            a = jnp.exp2(zs[j] - csum)
            if masked:
                a = jnp.where(keeps[j], a, 0.0)
            avs.append(jnp.dot(vt_ref[kbs[j]], a.astype(BF16), preferred_element_type=F32))
        for q0, av in zip(q0s, avs):
            acc = from_q0(acc, q0, acc[:, q0:] + av)
        return acc, run

    acc = jnp.zeros((q_ref.shape[1], tq), F32)
    run = jnp.zeros((1, tq), F32)
    diag = list(reversed(range(ratio)))
    acc, run = blocks([i * ratio + c for c in diag], [c * tk for c in diag], acc, run, True)

    def body(jj, carry):
        first = i * ratio - 1 - jj * ATT_UNROLL
        return blocks([first - d for d in range(ATT_UNROLL)], [0] * ATT_UNROLL, carry[0], carry[1], False)

    acc, run = lax.fori_loop(0, i * ratio // ATT_UNROLL, body, (acc, run))
    o_ref[...] = acc.T.astype(o_ref.dtype)


def _attn_prompt(q, k, v, bias, batch, seq):
    d_sb = q.shape[1]
    heads = d_sb // SB_HEAD_DIM
    tq, tk = ATT_TQ, ATT_TK
    nq = seq // tq
    assert seq % tq == 0 and (tq // tk) % ATT_UNROLL == 0
    r = lax.broadcasted_iota(jnp.int32, (tk, tk), 0)
    c = lax.broadcasted_iota(jnp.int32, (tk, tk), 1)
    upper = (c >= r).astype(BF16)
    tri = jnp.concatenate([upper, upper], axis=1)
    grid_spec = pltpu.PrefetchScalarGridSpec(
        num_scalar_prefetch=0,
        grid=(batch, heads, nq),
        in_specs=[
            pl.BlockSpec(memory_space=pltpu.SMEM),
            pl.BlockSpec((tq, SB_HEAD_DIM), lambda b, h, i: (b * nq + i, h)),
            pl.BlockSpec((seq, SB_HEAD_DIM), lambda b, h, i: (b, h)),
            pl.BlockSpec((seq, SB_HEAD_DIM), lambda b, h, i: (b, h)),
            pl.BlockSpec((tk, 2 * tk), lambda b, h, i: (0, 0)),
        ],
        out_specs=pl.BlockSpec((tq, SB_HEAD_DIM), lambda b, h, i: (b * nq + i, h)),
        scratch_shapes=[pltpu.VMEM((seq, SB_HEAD_DIM), BF16),
                        pltpu.VMEM((seq // tk, SB_HEAD_DIM, tk), BF16)],
    )
    return pl.pallas_call(
        _attn_prompt_kernel,
        grid_spec=grid_spec,
        out_shape=jax.ShapeDtypeStruct((batch * seq, d_sb), BF16),
        compiler_params=_cparams(("arbitrary", "arbitrary", "arbitrary")),
        name="attn_prompt",
    )(bias, q, k, v, tri)


def _attn_sample_kernel(layer, n_pages, pt_ref, qbd_ref, bias_ref, tri_ref, ck_ref, cv_ref, o_ref,
                        acc_ref, run_ref, kbuf, vbuf, sem):
    b = pl.program_id(0)
    s = pl.program_id(1)
    pps = kbuf.shape[1]
    ns = n_pages // pps
    step = b * ns + s
    slot = step % 2
    heads = acc_ref.shape[1] // kbuf.shape[3]
    page = kbuf.shape[2] // heads

    def page_copies(at_step, sl):
        bb = at_step // ns
        first = bb * n_pages + (n_pages - 1) - (at_step - bb * ns) * pps
        copies = []
        for r in range(pps):
            pg = pt_ref[first - r]
            copies.append(pltpu.make_async_copy(ck_ref.at[layer, pg], kbuf.at[sl, r], sem.at[sl, 0]))
            copies.append(pltpu.make_async_copy(cv_ref.at[layer, pg], vbuf.at[sl, r], sem.at[sl, 1]))
        return copies

    @pl.when(step == 0)
    def _():
        for cp in page_copies(step, slot):
            cp.start()

    @pl.when(step + 1 < pl.num_programs(0) * ns)
    def _():
        for cp in page_copies(step + 1, 1 - slot):
            cp.start()

    for cp in page_copies(step, slot):
        cp.wait()

    @pl.when(s == 0)
    def _():
        acc_ref[...] = jnp.zeros(acc_ref.shape, F32)
        run_ref[...] = jnp.zeros(run_ref.shape, F32)

    def flat_page(buf, r):
        return jnp.concatenate([buf[slot, r, pl.ds(h, page, stride=heads), :] for h in range(heads)],
                               axis=1).astype(BF16)

    vc = jnp.concatenate([flat_page(vbuf, r) for r in range(pps)], axis=0)
    half = pps // 2
    z_halves = [jnp.dot(jnp.concatenate([flat_page(kbuf, r) for r in range(lo, lo + half)], axis=0),
                        qbd_ref[...], preferred_element_type=F32) + bias_ref[...] for lo in (0, half)]
    tri = tri_ref[...]
    run = run_ref[...]
    a_parts = []
    for r in range(pps):
        z = z_halves[r // half][(r % half) * page:(r % half + 1) * page, :]
        hi, lo = _split_bf16(_softplus2(z))
        csum = (jnp.dot(tri, hi, preferred_element_type=F32)
                + jnp.dot(tri, lo, preferred_element_type=F32)) + run
        a_parts.append(jnp.exp2(z - csum).astype(BF16))
        run = csum[0:1, :]
    run_ref[...] = run
    a_all = jnp.concatenate(a_parts, axis=0)
    acc_ref[...] += lax.dot_general(a_all, vc, (((0,), (0,)), ((), ())), preferred_element_type=F32)

    @pl.when(s == pl.num_programs(1) - 1)
    def _():
        top = acc_ref[0:heads, :]
        row = lax.broadcasted_iota(jnp.int32, top.shape, 0)
        col = lax.broadcasted_iota(jnp.int32, top.shape, 1)
        lo_c = row * SB_HEAD_DIM
        keep = (col >= lo_c) & (col < lo_c + SB_HEAD_DIM)
        o_ref[...] = jnp.sum(jnp.where(keep, top, 0.0), axis=0, keepdims=True).astype(o_ref.dtype)


def _attn_sample(q, bias, cache_k, cache_v, page_table, layer):
    nb, d_sb = q.shape
    heads = d_sb // SB_HEAD_DIM
    depth, n_pool, page = cache_k.shape[:3]
    n_pages = page_table.shape[1]
    pps = PAGES_PER_STEP
    assert n_pages % pps == 0 and heads == SUBLANES
    pt = page_table.reshape(-1)
    ck = cache_k.reshape(depth, n_pool, page * heads, SB_HEAD_DIM)
    cv = cache_v.reshape(depth, n_pool, page * heads, SB_HEAD_DIM)
    r = lax.broadcasted_iota(jnp.int32, (d_sb, LANES), 0) // SB_HEAD_DIM
    c = lax.broadcasted_iota(jnp.int32, (d_sb, LANES), 1)
    qbd = jnp.where((r == c)[None], q[:, :, None], jnp.zeros((), q.dtype))
    bias_row = jnp.zeros((1, LANES), F32).at[0, :heads].set(bias)
    rr = lax.broadcasted_iota(jnp.int32, (page, page), 0)
    cc = lax.broadcasted_iota(jnp.int32, (page, page), 1)
    tri = (cc >= rr).astype(BF16)

    grid_spec = pltpu.PrefetchScalarGridSpec(
        num_scalar_prefetch=1,
        grid=(nb, n_pages // pps),
        in_specs=[
            pl.BlockSpec((None, d_sb, LANES), lambda b, s, pt_ref: (b, 0, 0)),
            pl.BlockSpec((1, LANES), lambda b, s, pt_ref: (0, 0)),
            pl.BlockSpec((page, page), lambda b, s, pt_ref: (0, 0)),
            pl.BlockSpec(memory_space=pl.ANY),
            pl.BlockSpec(memory_space=pl.ANY),
        ],
        out_specs=pl.BlockSpec((None, 1, d_sb), lambda b, s, pt_ref: (b, 0, 0)),
        scratch_shapes=[
            pltpu.VMEM((LANES, d_sb), F32),
            pltpu.VMEM((1, LANES), F32),
            pltpu.VMEM((2, pps, page * heads, SB_HEAD_DIM), cache_k.dtype),
            pltpu.VMEM((2, pps, page * heads, SB_HEAD_DIM), cache_v.dtype),
            pltpu.SemaphoreType.DMA((2, 2)),
        ],
    )
    out = pl.pallas_call(
        functools.partial(_attn_sample_kernel, layer, n_pages),
        grid_spec=grid_spec,
        out_shape=jax.ShapeDtypeStruct((nb, 1, d_sb), BF16),
        compiler_params=_cparams(("arbitrary", "arbitrary")),
        name="attn_sample",
    )(pt, qbd, bias_row, tri, ck, cv)
    return out.reshape(nb, d_sb)


def _route(logits):
    lane = lax.broadcasted_iota(jnp.int32, logits.shape, 1).astype(F32)
    neg = -jnp.inf
    far = float(LANES)
    is_group = lane < N_GROUPS
    lg = jnp.where(is_group, logits, neg)
    gmax = jnp.max(lg, axis=1, keepdims=True)
    gidx = jnp.min(jnp.where(lg == gmax, lane, far), axis=1, keepdims=True)
    den = jnp.sum(jnp.where(is_group, jnp.exp(logits - gmax), 0.0), axis=1, keepdims=True)
    ggate = 1.0 / den
    first = N_GROUPS + gidx * EXPERTS_PER_GROUP
    le = jnp.where((lane >= first) & (lane < first + EXPERTS_PER_GROUP), logits, neg)
    v1 = jnp.max(le, axis=1, keepdims=True)
    i1 = jnp.min(jnp.where(le == v1, lane, far), axis=1, keepdims=True)
    le2 = jnp.where(lane == i1, neg, le)
    v2 = jnp.max(le2, axis=1, keepdims=True)
    i2 = jnp.min(jnp.where(le2 == v2, lane, far), axis=1, keepdims=True)
    e21 = jnp.exp(v2 - v1)
    p1 = 1.0 / (1.0 + e21)
    p2 = e21 * p1
    return lane, gidx, (i1 - N_GROUPS, ggate * p1), (i2 - N_GROUPS, ggate * p2)


def _router_logits(x1, wrh_ref, wrl_ref, br_ref):
    hi, lo = _split_bf16(x1)
    wrh = wrh_ref[...]
    logits = (jnp.dot(hi, wrh, preferred_element_type=F32)
              + jnp.dot(lo, wrh, preferred_element_type=F32)
              + jnp.dot(hi, wrl_ref[...], preferred_element_type=F32)) + br_ref[...]
    return hi, logits


def _disp_tile(n_tokens):
    return min(DISP_TM, n_tokens)


def _moe_layout(token_sets):
    tiles = sum(n // _disp_tile(n) for n in token_sets)
    max_rows = sum(token_sets) + tiles * (ROW_ALIGN - 1)
    cap = pl.cdiv(max_rows + DISP_TM + EXP_TM, EXP_TM) * EXP_TM
    t_max = pl.cdiv(max_rows, EXP_TM) + N_GROUPS
    return cap, t_max


def _sort_rows(tm):
    return pl.cdiv(tm + N_GROUPS * (ROW_ALIGN - 1), LANES) * LANES


def _dispatch_kernel(alpha, cap, conf_ref, att_ref, sc_ref, x_ref, w_ref, g_ref, b_ref, wrh_ref, wrl_ref,
                     br_ref, ltri_ref, cnt0_ref, rows_in_ref, x1_ref, info_ref, tstart_ref, xs_ref,
                     sbuf, cnt_ref, sem):
    del rows_in_ref
    i = pl.program_id(0)
    nt = pl.num_programs(0)
    slot = i % 2
    tm = x_ref.shape[0]
    sort_rows = sbuf.shape[1] - tm

    @pl.when(i == 0)
    def _():
        for g in range(N_GROUPS):
            cnt_ref[g] = cnt0_ref[g]
        sbuf[...] = jnp.zeros(sbuf.shape, BF16)

    cat = jnp.concatenate([conf_ref[...], att_ref[...], sc_ref[...]], axis=1)
    mix = jnp.dot(cat, w_ref[...], preferred_element_type=F32)
    x1 = _layer_norm(alpha * x_ref[...] + mix, g_ref[...], b_ref[...])
    x1_ref[...] = x1
    hi, logits = _router_logits(x1, wrh_ref, wrl_ref, br_ref)
    lane, gidx, (e1, w1), (e2, w2) = _route(logits)

    first = gidx * EXPERTS_PER_GROUP
    l1, l2 = e1 - first, e2 - first
    w1h = w1.astype(BF16).astype(F32)
    w2h = w2.astype(BF16).astype(F32)
    gates = jnp.where(lane == l1, w1h, jnp.where(lane == l2, w2h, 0.0))
    gates = jnp.where(lane == l1 + EXPERTS_PER_GROUP, w1 - w1h,
                      jnp.where(lane == l2 + EXPERTS_PER_GROUP, w2 - w2h, gates))
    xrow = jnp.concatenate([hi, gates.astype(BF16)], axis=1)

    onehot = jnp.where(lane == gidx, 1.0, 0.0)
    before = jnp.dot(ltri_ref[...], onehot.astype(BF16), preferred_element_type=F32)
    rank = jnp.sum(jnp.where(lane == gidx, before, 0.0), axis=1, keepdims=True)
    lens = (before[tm - 1:tm, :] + onehot[tm - 1:tm, :]).astype(jnp.int32)
    info_ref[...] = jnp.where(lane == 0.0, gidx, jnp.where(lane == 1.0, rank, 0.0))

    lbase = jnp.int32(0)
    lbases, pads = [], []
    dest = rank
    for g in range(N_GROUPS):
        pad = ((lens[0, g] + (ROW_ALIGN - 1)) // ROW_ALIGN) * ROW_ALIGN
        lbases.append(lbase)
        pads.append(pad)
        dest = dest + jnp.where(gidx == float(g), lbase.astype(F32), 0.0)
        lbase = lbase + pad
    col = lax.broadcasted_iota(jnp.int32, (tm, sort_rows), 1).astype(F32)
    onehot_dest = jnp.where(col == dest, 1.0, 0.0).astype(BF16)
    srt = lax.dot_general(onehot_dest, xrow, (((0,), (0,)), ((), ())), preferred_element_type=F32)
    sbuf[slot, 0:sort_rows, :] = srt.astype(BF16)

    def copy(sl, g, src, dst):
        return pltpu.make_async_copy(
            sbuf.at[sl, pl.ds(pl.multiple_of(src, ROW_ALIGN), tm)],
            xs_ref.at[pl.ds(pl.multiple_of(dst, ROW_ALIGN), tm)], sem.at[sl, g])

    @pl.when(i > 0)
    def _():
        for g in range(N_GROUPS):
            copy(1 - slot, g, 0, 0).wait()

    for g in range(N_GROUPS):
        start = cnt_ref[g]
        copy(slot, g, lbases[g], g * cap + start).start()
        tstart_ref[i, g] = start
        cnt_ref[g] = start + pads[g]

    @pl.when(i == nt - 1)
    def _():
        for g in range(N_GROUPS):
            copy(slot, g, 0, 0).wait()
            tstart_ref[nt, g] = cnt_ref[g]


def _dispatch(conf, att, sc, x, w_out_b, layer, alpha, ln_g, ln_b, wr_hi, wr_lo, br, rows_buf, cnt0, cap):
    m, d_model = x.shape
    tm = _disp_tile(m)
    nt = m // tm
    assert m % tm == 0 and tm % ROW_ALIGN == 0
    d_mix = w_out_b.shape[1]
    width = d_model + LANES
    rr = lax.broadcasted_iota(jnp.int32, (tm, tm), 0)
    cc = lax.broadcasted_iota(jnp.int32, (tm, tm), 1)
    ltri = (cc < rr).astype(BF16)
    row = lambda i: (i, 0)
    par3 = lambda i: (layer, 0, 0)
    return pl.pallas_call(
        functools.partial(_dispatch_kernel, alpha, cap),
        grid=(nt,),
        in_specs=[
            pl.BlockSpec((tm, conf.shape[1]), row),
            pl.BlockSpec((tm, att.shape[1]), row),
            pl.BlockSpec((tm, sc.shape[1]), row),
            pl.BlockSpec((tm, d_model), row),
            pl.BlockSpec((None, d_mix, d_model), par3),
            pl.BlockSpec((None, 1, d_model), par3),
            pl.BlockSpec((None, 1, d_model), par3),
            pl.BlockSpec((None, d_model, LANES), par3),
            pl.BlockSpec((None, d_model, LANES), par3),
            pl.BlockSpec((None, 1, LANES), par3),
            pl.BlockSpec((tm, tm), lambda i: (0, 0)),
            pl.BlockSpec(memory_space=pltpu.SMEM),
            pl.BlockSpec(memory_space=pl.ANY),
        ],
        out_specs=[
            pl.BlockSpec((tm, d_model), row),
            pl.BlockSpec((tm, LANES), row),
            pl.BlockSpec(memory_space=pltpu.SMEM),
            pl.BlockSpec(memory_space=pl.ANY),
        ],
        input_output_aliases={12: 3},
        out_shape=[
            jax.ShapeDtypeStruct((m, d_model), F32),
            jax.ShapeDtypeStruct((m, LANES), F32),
            jax.ShapeDtypeStruct((nt + 1, N_GROUPS), jnp.int32),
            jax.ShapeDtypeStruct(rows_buf.shape, rows_buf.dtype),
        ],
        scratch_shapes=[
            pltpu.VMEM((2, _sort_rows(tm) + tm, width), BF16),
            pltpu.SMEM((N_GROUPS,), jnp.int32),
            pltpu.SemaphoreType.DMA((2, N_GROUPS)),
        ],
        compiler_params=_cparams(("arbitrary",)),
        name="dispatch",
    )(conf, att, sc, x, w_out_b, ln_g, ln_b, wr_hi, wr_lo, br, ltri, cnt0, rows_buf)


def _expert_kernel(cpe, tg_ref, tb_ref, tn_ref, tot_ref, xs_ref, wg_ref, wu_ref, wd_ref, ys_ref,
                   acc_ref, wgb_ref, wub_ref, wdb_ref):
    t = pl.program_id(0)
    c = pl.program_id(1)
    nc = EXPERTS_PER_GROUP * cpe
    d_model = wg_ref.shape[0]

    @pl.when(t < tot_ref[0])
    def _():
        @pl.when(c == 0)
        def _():
            acc_ref[...] = jnp.zeros(acc_ref.shape, F32)

        wgb_ref[...] = wg_ref[...].astype(BF16)
        wub_ref[...] = wu_ref[...].astype(BF16)
        wdb_ref[...] = wd_ref[...].astype(BF16)
        e_local = c // cpe

        def sub(r0, rows):
            xb = xs_ref[pl.ds(r0, rows), 0:d_model]
            gates = xs_ref[pl.ds(r0, rows), d_model:d_model + LANES].astype(F32)
            lane = lax.broadcasted_iota(jnp.int32, gates.shape, 1)
            mine = (lane == e_local) | (lane == e_local + EXPERTS_PER_GROUP)
            ce = jnp.sum(jnp.where(mine, gates, 0.0), axis=1, keepdims=True)
            gate = jnp.dot(xb, wgb_ref[...], preferred_element_type=F32)
            up = jnp.dot(xb, wub_ref[...], preferred_element_type=F32)
            hid = (gate * jax.nn.sigmoid(gate)) * up * ce
            acc_ref[pl.ds(r0, rows), :] += jnp.dot(hid.astype(BF16), wdb_ref[...],
                                                   preferred_element_type=F32)

        nsub = tn_ref[t]

        def pair(s, carry):
            sub(pl.multiple_of(s * (2 * EXP_SUB), 2 * EXP_SUB), 2 * EXP_SUB)
            return carry

        lax.fori_loop(0, lax.shift_right_logical(nsub, 1), pair, 0)

        @pl.when((nsub & 1) == 1)
        def _():
            sub(pl.multiple_of((nsub - 1) * EXP_SUB, EXP_SUB), EXP_SUB)

        @pl.when(c == nc - 1)
        def _():
            ys_ref[:, 0:d_model] = acc_ref[...].astype(ys_ref.dtype)
            ys_ref[:, d_model:d_model + LANES] = xs_ref[:, d_model:d_model + LANES]


def _experts(xs, tile_g, tile_blk, tile_nsub, total, wg, wu, wd, layer, cap):
    width = xs.shape[1]
    d_model = width - LANES
    d_exp = wg.shape[-1]
    t_max = tile_g.shape[0]
    cpe = d_exp // EXP_HC
    nc = EXPERTS_PER_GROUP * cpe
    blocks_per_group = cap // EXP_TM

    def chunk(t, c, tot):
        return jnp.where(t < tot[0], c, nc - 1)

    def rows(t, c, tg, tb, tn, tot):
        return (tg[t] * blocks_per_group + tb[t], 0)

    def w_in_map(t, c, tg, tb, tn, tot):
        ce = chunk(t, c, tot)
        return (layer, tg[t] * EXPERTS_PER_GROUP + ce // cpe, 0, ce % cpe)

    def w_out_map(t, c, tg, tb, tn, tot):
        ce = chunk(t, c, tot)
        return (layer, tg[t] * EXPERTS_PER_GROUP + ce // cpe, ce % cpe, 0)

    grid_spec = pltpu.PrefetchScalarGridSpec(
        num_scalar_prefetch=4,
        grid=(t_max, nc),
        in_specs=[
            pl.BlockSpec((EXP_TM, width), rows),
            pl.BlockSpec((None, None, d_model, EXP_HC), w_in_map),
            pl.BlockSpec((None, None, d_model, EXP_HC), w_in_map),
            pl.BlockSpec((None, None, EXP_HC, d_model), w_out_map),
        ],
        out_specs=pl.BlockSpec((EXP_TM, width), rows),
        scratch_shapes=[
            pltpu.VMEM((EXP_TM, d_model), F32),
            pltpu.VMEM((d_model, EXP_HC), BF16),
            pltpu.VMEM((d_model, EXP_HC), BF16),
            pltpu.VMEM((EXP_HC, d_model), BF16),
        ],
    )
    return pl.pallas_call(
        functools.partial(_expert_kernel, cpe),
        grid_spec=grid_spec,
        out_shape=jax.ShapeDtypeStruct(xs.shape, xs.dtype),
        input_output_aliases={4: 0},
        compiler_params=_cparams(("arbitrary", "arbitrary")),
        name="experts",
    )(tile_g, tile_blk, tile_nsub, total, xs, wg, wu, wd)


def _combine_kernel(alpha, ts_ref, back_ref, x1_ref, info_ref, *refs):
    wins = refs[:N_GROUPS]
    g_ref, b_ref, o_ref = refs[N_GROUPS:]
    i = pl.program_id(0)
    tm = x1_ref.shape[0]
    info = info_ref[...]
    gidx = info[:, 0:1]
    rank = info[:, 1:2]
    col = lax.broadcasted_iota(jnp.int32, (tm, tm), 1).astype(F32)
    y = jnp.zeros(x1_ref.shape, F32)
    for g in range(N_GROUPS):
        row_in_window = rank + back_ref[i * N_GROUPS + g].astype(F32)
        pick = jnp.where((gidx == float(g)) & (col == row_in_window), 1.0, 0.0).astype(BF16)
        y = y + jnp.dot(pick, wins[g][...], preferred_element_type=F32)
    o_ref[...] = _layer_norm(alpha * x1_ref[...] + y, g_ref[...], b_ref[...])


def _combine(x1, info, tstart, ends, ys, layer, alpha, ln_g, ln_b, cap):
    m, d_model = x1.shape
    tm = _disp_tile(m)
    last = _tiles_per_group(ends) * EXP_TM - tm
    ts = jnp.minimum(tstart, last[None, :]).reshape(-1)
    back = tstart.reshape(-1) - ts

    def window(g):
        def index_map(i, ts_ref, back_ref):
            return (pl.multiple_of(g * cap + ts_ref[i * N_GROUPS + g], ROW_ALIGN), 0)
        return pl.BlockSpec((pl.Element(tm), pl.Element(d_model)), index_map)

    row = lambda i, ts_ref, back_ref: (i, 0)
    par3 = lambda i, ts_ref, back_ref: (layer, 0, 0)
    grid_spec = pltpu.PrefetchScalarGridSpec(
        num_scalar_prefetch=2,
        grid=(m // tm,),
        in_specs=[pl.BlockSpec((tm, d_model), row), pl.BlockSpec((tm, LANES), row)]
        + [window(g) for g in range(N_GROUPS)]
        + [pl.BlockSpec((None, 1, d_model), par3), pl.BlockSpec((None, 1, d_model), par3)],
        out_specs=pl.BlockSpec((tm, d_model), row),
    )
    return pl.pallas_call(
        functools.partial(_combine_kernel, alpha),
        grid_spec=grid_spec,
        out_shape=jax.ShapeDtypeStruct((m, d_model), F32),
        compiler_params=_cparams(("arbitrary",)),
        name="combine",
    )(ts, back, x1, info, *([ys] * N_GROUPS), ln_g, ln_b)


def _tiles_per_group(ends):
    return jnp.maximum((ends + EXP_TM - 1) // EXP_TM, 1)


def _expert_tiles(ends, t_max):
    per_group = _tiles_per_group(ends)
    cum = jnp.cumsum(per_group)
    total = cum[-1:]
    t = jnp.minimum(jnp.arange(t_max, dtype=jnp.int32), total[0] - 1)
    tile_g = jnp.sum((t[:, None] >= cum[None, :]).astype(jnp.int32), axis=1)
    tile_blk = t - (cum - per_group)[tile_g]
    valid = jnp.clip(ends[tile_g] - tile_blk * EXP_TM, 0, EXP_TM)
    tile_nsub = (valid + EXP_SUB - 1) // EXP_SUB
    return tile_g, tile_blk, tile_nsub.astype(jnp.int32), total.astype(jnp.int32)


def kernel(x_prompt, x_sample, cache_k, cache_v, state_conv, state_sconv, page_table, w_in, conv_w, conv_b, conv_ln_g, conv_ln_b, sconv_w, sb_bias, w_out, ln1_g, ln1_b, router_group_w, router_group_b, router_expert_w, router_expert_b, expert_w_gate, expert_w_up, expert_w_down, ln2_g, ln2_b):
    batch, seq, d_model = x_prompt.shape
    nb = x_sample.shape[0]
    depth = w_in.shape[0]
    d_conf = conv_w.shape[-1]
    d_sc = sconv_w.shape[-1]
    heads = sb_bias.shape[-1]
    d_sb = heads * SB_HEAD_DIM
    alpha = (2 * depth) ** 0.25

    row3 = lambda a: a.reshape(depth, 1, a.shape[-1])
    conv_b3, conv_ln_g3, conv_ln_b3 = row3(conv_b), row3(conv_ln_g), row3(conv_ln_b)
    ln1_g3, ln1_b3, ln2_g3, ln2_b3 = row3(ln1_g), row3(ln1_b), row3(ln2_g), row3(ln2_b)

    n_log = N_GROUPS + N_EXPERTS
    wr = jnp.concatenate(
        [router_group_w, router_expert_w.transpose(0, 2, 1, 3).reshape(depth, d_model, N_EXPERTS)], axis=-1)
    wr = jnp.pad(wr, ((0, 0), (0, 0), (0, LANES - n_log)))
    wr_hi = wr.astype(BF16)
    wr_lo = (wr - wr_hi.astype(F32)).astype(BF16)
    br = jnp.concatenate([router_group_b, router_expert_b.reshape(depth, N_EXPERTS)], axis=-1)
    br = jnp.pad(br, ((0, 0), (0, LANES - n_log))).reshape(depth, 1, LANES)

    w_out_b = w_out.astype(BF16)
    experts = (expert_w_gate, expert_w_up, expert_w_down)
    cap, t_max = _moe_layout((batch * seq, nb))
    rows = jnp.zeros((N_GROUPS * cap, d_model + LANES), BF16)
    no_rows = jnp.zeros((N_GROUPS,), jnp.int32)

    cs_t = state_conv.transpose(0, 2, 1, 3)
    ss_t = state_sconv.transpose(0, 2, 1, 3)

    yp = x_prompt.reshape(batch * seq, d_model)
    ys = x_sample.reshape(nb, d_model)
    kp, vp, cp, sp, kd, vd, cd, sd = [], [], [], [], [], [], [], []
    for l in range(depth):
        router = (w_out_b, l, alpha, ln1_g3, ln1_b3, wr_hi, wr_lo, br)
        cg, q, k, v, gch = _inproj(yp, w_in, l, d_conf, d_sb, d_sc)
        conf, sc, cst, sst = _mixer_prompt(cg, gch, batch, seq, l, conv_w, conv_b3, conv_ln_g3, conv_ln_b3,
                                           sconv_w)
        att = _attn_prompt(q, k, v, sb_bias[l] * LOG2_E, batch, seq)
        x1p, info_p, tstart_p, rows = _dispatch(conf, att, sc, yp, *router, rows, no_rows, cap)
        kp.append(k.reshape(batch, seq, heads, SB_HEAD_DIM))
        vp.append(v.reshape(batch, seq, heads, SB_HEAD_DIM))
        cp.append(cst[:, CONV_HIST - (CONF_WIDTH - 1):, :])
        sp.append(sst[:, SC_HIST - (SC_WIDTH - 1):, :])

        cg, q, k, v, gch = _inproj(ys, w_in, l, d_conf, d_sb, d_sc)
        conf, sc, ncs, nss = _mixer_sample(cg, gch, cs_t, ss_t, l, conv_w, conv_b3, conv_ln_g3, conv_ln_b3,
                                           sconv_w)
        att = _attn_sample(q, sb_bias[l] * LOG2_E, cache_k, cache_v, page_table, l)
        x1s, info_s, tstart_s, rows = _dispatch(conf, att, sc, ys, *router, rows, tstart_p[-1], cap)
        kd.append(k.reshape(nb, 1, heads, SB_HEAD_DIM))
        vd.append(v.reshape(nb, 1, heads, SB_HEAD_DIM))
        cd.append(ncs.transpose(1, 0, 2))
        sd.append(nss.transpose(1, 0, 2))

        ends = tstart_s[-1]
        rows = _experts(rows, *_expert_tiles(ends, t_max), *experts, l, cap)
        yp = _combine(x1p, info_p, tstart_p, ends, rows, l, alpha, ln2_g3, ln2_b3, cap)
        ys = _combine(x1s, info_s, tstart_s, ends, rows, l, alpha, ln2_g3, ln2_b3, cap)

    return (yp.reshape(batch, seq, d_model), ys.reshape(nb, 1, d_model),
            jnp.stack(kp), jnp.stack(vp), jnp.stack(cp), jnp.stack(sp),
            jnp.stack(kd), jnp.stack(vd), jnp.stack(cd), jnp.stack(sd))
```

```python
import functools
import math

import jax
import jax.numpy as jnp
from jax import lax
from jax.experimental import pallas as pl
from jax.experimental.pallas import tpu as pltpu

F32 = jnp.float32
BF16 = jnp.bfloat16

SB_HEAD_DIM = 128
CONF_WIDTH = 31
SC_WIDTH = 3
N_GROUPS = 4
EXPERTS_PER_GROUP = 4
N_EXPERTS = N_GROUPS * EXPERTS_PER_GROUP
LN_EPS = 1e-5
LOG2_E = math.log2(math.e)

LANES = 128
SUBLANES = 8
VMEM_LIMIT_MB = 56

INPROJ_TM = 1024
INPROJ_TN = 512
MIX_T = 256
MIX_RC = 32
CONV_HIST = 32
SC_HIST = 8
ATT_TQ = 1024
ATT_TK = 256
ATT_UNROLL = 4
PAGES_PER_STEP = 16
DISP_TM = 256
ROW_ALIGN = 16
EXP_TM = 1280
EXP_SUB = 256
EXP_HC = 256


def _cparams(sem):
    return pltpu.CompilerParams(dimension_semantics=sem,
                                vmem_limit_bytes=VMEM_LIMIT_MB * 1024 * 1024)


def _layer_norm(y, g, b):
    mu = jnp.mean(y, axis=-1, keepdims=True)
    d = y - mu
    var = jnp.mean(d * d, axis=-1, keepdims=True)
    return d * lax.rsqrt(var + LN_EPS) * g + b


def _softplus2(z):
    return jnp.maximum(z, 0.0) + jnp.log2(1.0 + jnp.exp2(-jnp.abs(z)))


def _split_bf16(x):
    hi = x.astype(BF16)
    lo = (x - hi.astype(F32)).astype(BF16)
    return hi, lo


def _inproj_kernel(bounds, scale, x_ref, w_ref, cg_ref, q_ref, k_ref, v_ref, gch_ref, xb_ref):
    j = pl.program_id(1)

    @pl.when(j == 0)
    def _():
        xb_ref[...] = x_ref[...].astype(BF16)

    acc = jnp.dot(xb_ref[...], w_ref[...].astype(BF16), preferred_element_type=F32)
    b_cg, b_q, b_k, b_v = bounds

    @pl.when(j < b_cg)
    def _():
        cg_ref[...] = acc

    @pl.when((j >= b_cg) & (j < b_q))
    def _():
        q_ref[...] = (acc * scale).astype(BF16)

    @pl.when((j >= b_q) & (j < b_k))
    def _():
        k_ref[...] = acc

    @pl.when((j >= b_k) & (j < b_v))
    def _():
        v_ref[...] = acc

    @pl.when(j >= b_v)
    def _():
        gch_ref[...] = acc


def _inproj(x, w_in, layer, d_conf, d_sb, d_sc):
    m, d_model = x.shape
    p_in = w_in.shape[-1]
    tm = min(INPROJ_TM, m)
    tn = INPROJ_TN
    n_cg, n_sb, n_gch = 2 * d_conf // tn, d_sb // tn, 3 * d_sc // tn
    bounds = (n_cg, n_cg + n_sb, n_cg + 2 * n_sb, n_cg + 3 * n_sb)
    assert bounds[-1] + n_gch == p_in // tn and m % tm == 0

    def out_map(lo, n):
        return lambda i, j: (i, jnp.clip(j - lo, 0, n - 1))

    kernel = functools.partial(_inproj_kernel, bounds, SB_HEAD_DIM ** -0.5 * LOG2_E)
    return pl.pallas_call(
        kernel,
        grid=(m // tm, p_in // tn),
        in_specs=[
            pl.BlockSpec((tm, d_model), lambda i, j: (i, 0)),
            pl.BlockSpec((None, d_model, tn), lambda i, j: (layer, 0, j)),
        ],
        out_specs=[
            pl.BlockSpec((tm, tn), out_map(0, n_cg)),
            pl.BlockSpec((tm, tn), out_map(bounds[0], n_sb)),
            pl.BlockSpec((tm, tn), out_map(bounds[1], n_sb)),
            pl.BlockSpec((tm, tn), out_map(bounds[2], n_sb)),
            pl.BlockSpec((tm, tn), out_map(bounds[3], n_gch)),
        ],
        out_shape=[
            jax.ShapeDtypeStruct((m, 2 * d_conf), F32),
            jax.ShapeDtypeStruct((m, d_sb), BF16),
            jax.ShapeDtypeStruct((m, d_sb), F32),
            jax.ShapeDtypeStruct((m, d_sb), F32),
            jax.ShapeDtypeStruct((m, 3 * d_sc), F32),
        ],
        scratch_shapes=[pltpu.VMEM((tm, d_model), BF16)],
        compiler_params=_cparams(("arbitrary", "arbitrary")),
        name="inproj",
    )(x, w_in)


def _mixer_prompt_kernel(val_ref, gate_ref, gb_ref, gc_ref, h_ref, cw_ref, cb_ref, lg_ref, lb_ref, sw_ref,
                         conf_ref, sc_ref, cst_ref, sst_ref, ext_ref, ext2_ref, shift_ref):
    t = pl.program_id(1)
    tt, c = val_ref.shape

    @pl.when(t == 0)
    def _():
        ext_ref[0:CONV_HIST, :] = jnp.zeros((CONV_HIST, c), F32)
        ext2_ref[0:SC_HIST, :] = jnp.zeros((SC_HIST, ext2_ref.shape[1]), F32)

    ext_ref[CONV_HIST:CONV_HIST + tt, :] = val_ref[...] * jax.nn.sigmoid(gate_ref[...])
    ext2_ref[SC_HIST:SC_HIST + tt, :] = gc_ref[...] * h_ref[...]

    shifted_rows = shift_ref.shape[1]
    for b in range(1, SUBLANES):
        shift_ref[b - 1] = ext_ref[b:b + shifted_rows, :]

    def tap_rows(start, rows):
        b, base = start % SUBLANES, start - start % SUBLANES
        if b == 0:
            return ext_ref[base:base + rows, :]
        return shift_ref[b - 1, base:base + rows, :]

    cb = cb_ref[...]
    lg = lg_ref[...]
    lb = lb_ref[...]
    conv_off = CONV_HIST - (CONF_WIDTH - 1)
    sc_off = SC_HIST - (SC_WIDTH - 1)
    for ch in range(tt // MIX_RC):
        r0 = ch * MIX_RC
        acc = jnp.zeros((MIX_RC, c), F32)
        for w in range(CONF_WIDTH):
            acc = acc + tap_rows(r0 + conv_off + w, MIX_RC) * cw_ref[w:w + 1, :]
        yn = _layer_norm(acc + cb, lg, lb)
        conf_ref[r0:r0 + MIX_RC, :] = (yn * jax.nn.sigmoid(yn)).astype(conf_ref.dtype)
        s = jnp.zeros((MIX_RC, ext2_ref.shape[1]), F32)
        for w in range(SC_WIDTH):
            s = s + ext2_ref[r0 + sc_off + w:r0 + sc_off + w + MIX_RC, :] * sw_ref[w:w + 1, :]
        sc_ref[r0:r0 + MIX_RC, :] = (gb_ref[r0:r0 + MIX_RC, :] * s).astype(sc_ref.dtype)

    ext_ref[0:CONV_HIST, :] = ext_ref[tt:tt + CONV_HIST, :]
    ext2_ref[0:SC_HIST, :] = ext2_ref[tt:tt + SC_HIST, :]

    @pl.when(t == pl.num_programs(1) - 1)
    def _():
        cst_ref[...] = ext_ref[tt:tt + CONV_HIST, :]
        sst_ref[...] = ext2_ref[tt:tt + SC_HIST, :]


def _mixer_prompt(cg, gch, batch, seq, layer, conv_w, conv_b, conv_ln_g, conv_ln_b, sconv_w):
    d_conf = cg.shape[1] // 2
    d_sc = gch.shape[1] // 3
    tt = MIX_T
    nt = seq // tt
    row = lambda b, t: (b * nt + t, 0)
    col = lambda cidx: (lambda b, t: (b * nt + t, cidx))
    par3 = lambda b, t: (layer, 0, 0)
    return pl.pallas_call(
        _mixer_prompt_kernel,
        grid=(batch, nt),
        in_specs=[
            pl.BlockSpec((tt, d_conf), col(0)),
            pl.BlockSpec((tt, d_conf), col(1)),
            pl.BlockSpec((tt, d_sc), col(0)),
            pl.BlockSpec((tt, d_sc), col(1)),
            pl.BlockSpec((tt, d_sc), col(2)),
            pl.BlockSpec((None, CONF_WIDTH, d_conf), par3),
            pl.BlockSpec((None, 1, d_conf), par3),
            pl.BlockSpec((None, 1, d_conf), par3),
            pl.BlockSpec((None, 1, d_conf), par3),
            pl.BlockSpec((None, SC_WIDTH, d_sc), par3),
        ],
        out_specs=[
            pl.BlockSpec((tt, d_conf), row),
            pl.BlockSpec((tt, d_sc), row),
            pl.BlockSpec((None, CONV_HIST, d_conf), lambda b, t: (b, 0, 0)),
            pl.BlockSpec((None, SC_HIST, d_sc), lambda b, t: (b, 0, 0)),
        ],
        out_shape=[
            jax.ShapeDtypeStruct((batch * seq, d_conf), BF16),
            jax.ShapeDtypeStruct((batch * seq, d_sc), BF16),
            jax.ShapeDtypeStruct((batch, CONV_HIST, d_conf), F32),
            jax.ShapeDtypeStruct((batch, SC_HIST, d_sc), F32),
        ],
        scratch_shapes=[pltpu.VMEM((tt + CONV_HIST, d_conf), F32),
                        pltpu.VMEM((tt + SC_HIST, d_sc), F32),
                        pltpu.VMEM((SUBLANES - 1, tt + CONV_HIST - SUBLANES, d_conf), F32)],
        compiler_params=_cparams(("arbitrary", "arbitrary")),
        name="mixer_prompt",
    )(cg, cg, gch, gch, gch, conv_w, conv_b, conv_ln_g, conv_ln_b, sconv_w)


def _mixer_sample_kernel(cg_ref, gch_ref, cs_ref, ss_ref, cw_ref, cb_ref, lg_ref, lb_ref, sw_ref,
                         conf_ref, sc_ref, ncs_ref, nss_ref):
    c = cs_ref.shape[-1]
    c2 = ss_ref.shape[-1]
    hist = CONF_WIDTH - 1
    glu = cg_ref[:, 0:c] * jax.nn.sigmoid(cg_ref[:, c:2 * c])
    acc = glu * cw_ref[hist:hist + 1, :]
    for w in range(hist):
        acc = acc + cs_ref[w] * cw_ref[w:w + 1, :]
    yn = _layer_norm(acc + cb_ref[...], lg_ref[...], lb_ref[...])
    conf_ref[...] = (yn * jax.nn.sigmoid(yn)).astype(conf_ref.dtype)

    p = gch_ref[:, c2:2 * c2] * gch_ref[:, 2 * c2:3 * c2]
    s = p * sw_ref[SC_WIDTH - 1:SC_WIDTH, :]
    for w in range(SC_WIDTH - 1):
        s = s + ss_ref[w] * sw_ref[w:w + 1, :]
    sc_ref[...] = (gch_ref[:, 0:c2] * s).astype(sc_ref.dtype)

    for w in range(hist - 1):
        ncs_ref[w] = cs_ref[w + 1]
    ncs_ref[hist - 1] = glu
    for w in range(SC_WIDTH - 2):
        nss_ref[w] = ss_ref[w + 1]
    nss_ref[SC_WIDTH - 2] = p


def _mixer_sample(cg, gch, cs_t, ss_t, layer, conv_w, conv_b, conv_ln_g, conv_ln_b, sconv_w):
    nb = cg.shape[0]
    d_conf = cg.shape[1] // 2
    d_sc = gch.shape[1] // 3
    full2 = lambda i: (0, 0)
    par3 = lambda i: (layer, 0, 0)
    st4 = lambda i: (layer, 0, 0, 0)
    return pl.pallas_call(
        _mixer_sample_kernel,
        grid=(1,),
        in_specs=[
            pl.BlockSpec(cg.shape, full2),
            pl.BlockSpec(gch.shape, full2),
            pl.BlockSpec((None,) + cs_t.shape[1:], st4),
            pl.BlockSpec((None,) + ss_t.shape[1:], st4),
            pl.BlockSpec((None, CONF_WIDTH, d_conf), par3),
            pl.BlockSpec((None, 1, d_conf), par3),
            pl.BlockSpec((None, 1, d_conf), par3),
            pl.BlockSpec((None, 1, d_conf), par3),
            pl.BlockSpec((None, SC_WIDTH, d_sc), par3),
        ],
        out_specs=[
            pl.BlockSpec((nb, d_conf), full2),
            pl.BlockSpec((nb, d_sc), full2),
            pl.BlockSpec(cs_t.shape[1:], lambda i: (0, 0, 0)),
            pl.BlockSpec(ss_t.shape[1:], lambda i: (0, 0, 0)),
        ],
        out_shape=[
            jax.ShapeDtypeStruct((nb, d_conf), BF16),
            jax.ShapeDtypeStruct((nb, d_sc), BF16),
            jax.ShapeDtypeStruct(cs_t.shape[1:], F32),
            jax.ShapeDtypeStruct(ss_t.shape[1:], F32),
        ],
        compiler_params=_cparams(("arbitrary",)),
        name="mixer_sample",
    )(cg, gch, cs_t, ss_t, conv_w, conv_b, conv_ln_g, conv_ln_b, sconv_w)


def _attn_prompt_kernel(bias_ref, q_ref, k_ref, v_ref, tri_ref, o_ref, kb_ref, vt_ref):
    h = pl.program_id(1)
    i = pl.program_id(2)
    tq = q_ref.shape[0]
    tk = tri_ref.shape[0]
    ratio = tq // tk

    @pl.when(i == 0)
    def _():
        kb_ref[...] = k_ref[...].astype(BF16)
        for kb in range(vt_ref.shape[0]):
            vt_ref[kb] = v_ref[kb * tk:(kb + 1) * tk, :].T.astype(BF16)

    bias = bias_ref[h]
    q = q_ref[...]
    tri2 = tri_ref[...]

    def blocks(kbs, q0s, acc, run, masked):
        zs, splits, csums, avs = [], [], [], []

        def causal(x):
            sq = jnp.where(lax.broadcasted_iota(jnp.int32, (tk, tk), 0)
                           < lax.broadcasted_iota(jnp.int32, (tk, tk), 1), x[:, :tk], 0.0)
            return sq if x.shape[1] == tk else jnp.concatenate([sq, x[:, tk:]], axis=1)

        for kb, q0 in zip(kbs, q0s):
            start = pl.multiple_of(kb * tk, tk)
            kblk = kb_ref[pl.ds(start, tk), :]
            zs.append(lax.dot_general(kblk, q[q0:, :], (((1,), (1,)), ((), ())),
                                      preferred_element_type=F32) + bias)
        for z in zs:
            sp = _softplus2(z)
            if masked:
                sp = causal(sp)
            hi, lo = _split_bf16(sp)
            splits.append(jnp.concatenate([hi, lo], axis=0))
        for w in splits:
            csums.append(jnp.dot(tri2, w, preferred_element_type=F32))
        def from_q0(full, q0, tail):
            return tail if q0 == 0 else jnp.concatenate([full[:, :q0], tail], axis=1)

        for j, q0 in enumerate(q0s):
            csum = csums[j] + run[:, q0:]
            run = from_q0(run, q0, csum[0:1, :])
            a = jnp.exp2(zs[j] - csum)
            if masked:
                a = causal(a)
            avs.append(jnp.dot(vt_ref[kbs[j]], a.astype(BF16), preferred_element_type=F32))
        for q0, av in zip(q0s, avs):
            acc = from_q0(acc, q0, acc[:, q0:] + av)
        return acc, run

    acc = jnp.zeros((q_ref.shape[1], tq), F32)
    run = jnp.zeros((1, tq), F32)
    diag = list(reversed(range(ratio)))
    acc, run = blocks([i * ratio + c for c in diag], [c * tk for c in diag], acc, run, True)

    def body(jj, carry):
        first = i * ratio - 1 - jj * ATT_UNROLL
        return blocks([first - d for d in range(ATT_UNROLL)], [0] * ATT_UNROLL, carry[0], carry[1], False)

    acc, run = lax.fori_loop(0, i * ratio // ATT_UNROLL, body, (acc, run))
    o_ref[...] = acc.T.astype(o_ref.dtype)


def _attn_prompt(q, k, v, bias, batch, seq):
    d_sb = q.shape[1]
    heads = d_sb // SB_HEAD_DIM
    tq, tk = ATT_TQ, ATT_TK
    nq = seq // tq
    assert seq % tq == 0 and (tq // tk) % ATT_UNROLL == 0
    r = lax.broadcasted_iota(jnp.int32, (tk, tk), 0)
    c = lax.broadcasted_iota(jnp.int32, (tk, tk), 1)
    upper = (c >= r).astype(BF16)
    tri = jnp.concatenate([upper, upper], axis=1)
    grid_spec = pltpu.PrefetchScalarGridSpec(
        num_scalar_prefetch=0,
        grid=(batch, heads, nq),
        in_specs=[
            pl.BlockSpec(memory_space=pltpu.SMEM),
            pl.BlockSpec((tq, SB_HEAD_DIM), lambda b, h, i: (b * nq + i, h)),
            pl.BlockSpec((seq, SB_HEAD_DIM), lambda b, h, i: (b, h)),
            pl.BlockSpec((seq, SB_HEAD_DIM), lambda b, h, i: (b, h)),
            pl.BlockSpec((tk, 2 * tk), lambda b, h, i: (0, 0)),
        ],
        out_specs=pl.BlockSpec((tq, SB_HEAD_DIM), lambda b, h, i: (b * nq + i, h)),
        scratch_shapes=[pltpu.VMEM((seq, SB_HEAD_DIM), BF16),
                        pltpu.VMEM((seq // tk, SB_HEAD_DIM, tk), BF16)],
    )
    return pl.pallas_call(
        _attn_prompt_kernel,
        grid_spec=grid_spec,
        out_shape=jax.ShapeDtypeStruct((batch * seq, d_sb), BF16),
        compiler_params=_cparams(("arbitrary", "arbitrary", "arbitrary")),
        name="attn_prompt",
    )(bias, q, k, v, tri)


def _attn_sample_kernel(layer, n_pages, pt_ref, qbd_ref, bias_ref, tri_ref, ck_ref, cv_ref, o_ref,
                        acc_ref, run_ref, kbuf, vbuf, sem):
    b = pl.program_id(0)
    s = pl.program_id(1)
    pps = kbuf.shape[1]
    ns = n_pages // pps
    step = b * ns + s
    slot = step % 2
    heads = acc_ref.shape[1] // kbuf.shape[3]
    page = kbuf.shape[2] // heads

    def page_copies(at_step, sl):
        bb = at_step // ns
        first = bb * n_pages + (n_pages - 1) - (at_step - bb * ns) * pps
        copies = []
        for r in range(pps):
            pg = pt_ref[first - r]
            copies.append(pltpu.make_async_copy(ck_ref.at[layer, pg], kbuf.at[sl, r], sem.at[sl, 0]))
            copies.append(pltpu.make_async_copy(cv_ref.at[layer, pg], vbuf.at[sl, r], sem.at[sl, 1]))
        return copies

    @pl.when(step == 0)
    def _():
        for cp in page_copies(step, slot):
            cp.start()

    @pl.when(step + 1 < pl.num_programs(0) * ns)
    def _():
        for cp in page_copies(step + 1, 1 - slot):
            cp.start()

    for cp in page_copies(step, slot):
        cp.wait()

    @pl.when(s == 0)
    def _():
        acc_ref[...] = jnp.zeros(acc_ref.shape, F32)
        run_ref[...] = jnp.zeros(run_ref.shape, F32)

    def flat_page(buf, r):
        return jnp.concatenate([buf[slot, r, pl.ds(h, page, stride=heads), :] for h in range(heads)],
                               axis=1).astype(BF16)

    vc = jnp.concatenate([flat_page(vbuf, r) for r in range(pps)], axis=0)
    half = pps // 2
    z_halves = [jnp.dot(jnp.concatenate([flat_page(kbuf, r) for r in range(lo, lo + half)], axis=0),
                        qbd_ref[...], preferred_element_type=F32) + bias_ref[...] for lo in (0, half)]
    tri = tri_ref[...]
    run = run_ref[...]
    a_parts = []
    for r in range(pps):
        z = z_halves[r // half][(r % half) * page:(r % half + 1) * page, :]
        hi, lo = _split_bf16(_softplus2(z))
        csum = (jnp.dot(tri, hi, preferred_element_type=F32)
                + jnp.dot(tri, lo, preferred_element_type=F32)) + run
        a_parts.append(jnp.exp2(z - csum).astype(BF16))
        run = csum[0:1, :]
    run_ref[...] = run
    a_all = jnp.concatenate(a_parts, axis=0)
    acc_ref[...] += lax.dot_general(a_all, vc, (((0,), (0,)), ((), ())), preferred_element_type=F32)

    @pl.when(s == pl.num_programs(1) - 1)
    def _():
        top = acc_ref[0:heads, :]
        row = lax.broadcasted_iota(jnp.int32, top.shape, 0)
        col = lax.broadcasted_iota(jnp.int32, top.shape, 1)
        lo_c = row * SB_HEAD_DIM
        keep = (col >= lo_c) & (col < lo_c + SB_HEAD_DIM)
        o_ref[...] = jnp.sum(jnp.where(keep, top, 0.0), axis=0, keepdims=True).astype(o_ref.dtype)


def _attn_sample(q, bias, cache_k, cache_v, page_table, layer):
    nb, d_sb = q.shape
    heads = d_sb // SB_HEAD_DIM
    depth, n_pool, page = cache_k.shape[:3]
    n_pages = page_table.shape[1]
    pps = PAGES_PER_STEP
    assert n_pages % pps == 0 and heads == SUBLANES
    pt = page_table.reshape(-1)
    ck = cache_k.reshape(depth, n_pool, page * heads, SB_HEAD_DIM)
    cv = cache_v.reshape(depth, n_pool, page * heads, SB_HEAD_DIM)
    r = lax.broadcasted_iota(jnp.int32, (d_sb, LANES), 0) // SB_HEAD_DIM
    c = lax.broadcasted_iota(jnp.int32, (d_sb, LANES), 1)
    qbd = jnp.where((r == c)[None], q[:, :, None], jnp.zeros((), q.dtype))
    bias_row = jnp.zeros((1, LANES), F32).at[0, :heads].set(bias)
    rr = lax.broadcasted_iota(jnp.int32, (page, page), 0)
    cc = lax.broadcasted_iota(jnp.int32, (page, page), 1)
    tri = (cc >= rr).astype(BF16)

    grid_spec = pltpu.PrefetchScalarGridSpec(
        num_scalar_prefetch=1,
        grid=(nb, n_pages // pps),
        in_specs=[
            pl.BlockSpec((None, d_sb, LANES), lambda b, s, pt_ref: (b, 0, 0)),
            pl.BlockSpec((1, LANES), lambda b, s, pt_ref: (0, 0)),
            pl.BlockSpec((page, page), lambda b, s, pt_ref: (0, 0)),
            pl.BlockSpec(memory_space=pl.ANY),
            pl.BlockSpec(memory_space=pl.ANY),
        ],
        out_specs=pl.BlockSpec((None, 1, d_sb), lambda b, s, pt_ref: (b, 0, 0)),
        scratch_shapes=[
            pltpu.VMEM((LANES, d_sb), F32),
            pltpu.VMEM((1, LANES), F32),
            pltpu.VMEM((2, pps, page * heads, SB_HEAD_DIM), cache_k.dtype),
            pltpu.VMEM((2, pps, page * heads, SB_HEAD_DIM), cache_v.dtype),
            pltpu.SemaphoreType.DMA((2, 2)),
        ],
    )
    out = pl.pallas_call(
        functools.partial(_attn_sample_kernel, layer, n_pages),
        grid_spec=grid_spec,
        out_shape=jax.ShapeDtypeStruct((nb, 1, d_sb), BF16),
        compiler_params=_cparams(("arbitrary", "arbitrary")),
        name="attn_sample",
    )(pt, qbd, bias_row, tri, ck, cv)
    return out.reshape(nb, d_sb)


def _route(logits):
    lane = lax.broadcasted_iota(jnp.int32, logits.shape, 1).astype(F32)
    neg = -jnp.inf
    far = float(LANES)
    is_group = lane < N_GROUPS
    lg = jnp.where(is_group, logits, neg)
    gmax = jnp.max(lg, axis=1, keepdims=True)
    gidx = jnp.min(jnp.where(lg == gmax, lane, far), axis=1, keepdims=True)
    den = jnp.sum(jnp.where(is_group, jnp.exp(logits - gmax), 0.0), axis=1, keepdims=True)
    ggate = 1.0 / den
    first = N_GROUPS + gidx * EXPERTS_PER_GROUP
    le = jnp.where((lane >= first) & (lane < first + EXPERTS_PER_GROUP), logits, neg)
    v1 = jnp.max(le, axis=1, keepdims=True)
    i1 = jnp.min(jnp.where(le == v1, lane, far), axis=1, keepdims=True)
    le2 = jnp.where(lane == i1, neg, le)
    v2 = jnp.max(le2, axis=1, keepdims=True)
    i2 = jnp.min(jnp.where(le2 == v2, lane, far), axis=1, keepdims=True)
    e21 = jnp.exp(v2 - v1)
    p1 = 1.0 / (1.0 + e21)
    p2 = e21 * p1
    return lane, gidx, (i1 - N_GROUPS, ggate * p1), (i2 - N_GROUPS, ggate * p2)


def _router_logits(x1, wr_ref, br_ref):
    hi, lo = _split_bf16(x1)
    both = jnp.dot(hi, wr_ref[...], preferred_element_type=F32)
    logits = (both[:, :LANES] + both[:, LANES:]
              + jnp.dot(lo, wr_ref[:, :LANES], preferred_element_type=F32)) + br_ref[...]
    return hi, logits


def _disp_tile(n_tokens):
    return min(DISP_TM, n_tokens)


def _moe_layout(token_sets):
    tiles = sum(n // _disp_tile(n) for n in token_sets)
    max_rows = sum(token_sets) + tiles * (ROW_ALIGN - 1)
    cap = pl.cdiv(max_rows + DISP_TM + EXP_TM, EXP_TM) * EXP_TM
    t_max = pl.cdiv(max_rows, EXP_TM) + N_GROUPS
    return cap, t_max


def _sort_rows(tm):
    return pl.cdiv(tm + N_GROUPS * (ROW_ALIGN - 1), LANES) * LANES


def _dispatch_kernel(alpha, cap, conf_ref, att_ref, sc_ref, x_ref, w_ref, g_ref, b_ref, wr_ref,
                     br_ref, ltri_ref, cnt0_ref, rows_in_ref, x1_ref, info_ref, tstart_ref, xs_ref,
                     sbuf, cnt_ref, sem):
    del rows_in_ref
    i = pl.program_id(0)
    nt = pl.num_programs(0)
    slot = i % 2
    tm = x_ref.shape[0]
    sort_rows = sbuf.shape[1] - tm

    @pl.when(i == 0)
    def _():
        for g in range(N_GROUPS):
            cnt_ref[g] = cnt0_ref[g]
        sbuf[...] = jnp.zeros(sbuf.shape, BF16)

    cat = jnp.concatenate([conf_ref[...], att_ref[...], sc_ref[...]], axis=1)
    mix = jnp.dot(cat, w_ref[...], preferred_element_type=F32)
    x1 = _layer_norm(alpha * x_ref[...] + mix, g_ref[...], b_ref[...])
    x1_ref[...] = x1
    hi, logits = _router_logits(x1, wr_ref, br_ref)
    lane, gidx, (e1, w1), (e2, w2) = _route(logits)

    first = gidx * EXPERTS_PER_GROUP
    l1, l2 = e1 - first, e2 - first
    w1h = w1.astype(BF16).astype(F32)
    w2h = w2.astype(BF16).astype(F32)
    gates = jnp.where(lane == l1, w1h, jnp.where(lane == l2, w2h, 0.0))
    gates = jnp.where(lane == l1 + EXPERTS_PER_GROUP, w1 - w1h,
                      jnp.where(lane == l2 + EXPERTS_PER_GROUP, w2 - w2h, gates))
    xrow = jnp.concatenate([hi, gates.astype(BF16)], axis=1)

    onehot = jnp.where(lane == gidx, 1.0, 0.0)
    before = jnp.dot(ltri_ref[...], onehot.astype(BF16), preferred_element_type=F32)
    rank = jnp.sum(jnp.where(lane == gidx, before, 0.0), axis=1, keepdims=True)
    lens = (before[tm - 1:tm, :] + onehot[tm - 1:tm, :]).astype(jnp.int32)
    info_ref[...] = jnp.where(lane == 0.0, gidx, jnp.where(lane == 1.0, rank, 0.0))

    lbase = jnp.int32(0)
    lbases, pads = [], []
    dest = rank
    for g in range(N_GROUPS):
        pad = ((lens[0, g] + (ROW_ALIGN - 1)) // ROW_ALIGN) * ROW_ALIGN
        lbases.append(lbase)
        pads.append(pad)
        dest = dest + jnp.where(gidx == float(g), lbase.astype(F32), 0.0)
        lbase = lbase + pad
    col = lax.broadcasted_iota(jnp.int32, (tm, sort_rows), 1).astype(F32)
    onehot_dest = jnp.where(col == dest, 1.0, 0.0).astype(BF16)
    srt = lax.dot_general(onehot_dest, xrow, (((0,), (0,)), ((), ())), preferred_element_type=F32)
    sbuf[slot, 0:sort_rows, :] = srt.astype(BF16)

    def copy(sl, g, src, dst):
        return pltpu.make_async_copy(
            sbuf.at[sl, pl.ds(pl.multiple_of(src, ROW_ALIGN), tm)],
            xs_ref.at[pl.ds(pl.multiple_of(dst, ROW_ALIGN), tm)], sem.at[sl, g])

    @pl.when(i > 0)
    def _():
        for g in range(N_GROUPS):
            copy(1 - slot, g, 0, 0).wait()

    for g in range(N_GROUPS):
        start = cnt_ref[g]
        copy(slot, g, lbases[g], g * cap + start).start()
        tstart_ref[i, g] = start
        cnt_ref[g] = start + pads[g]

    @pl.when(i == nt - 1)
    def _():
        for g in range(N_GROUPS):
            copy(slot, g, 0, 0).wait()
            tstart_ref[nt, g] = cnt_ref[g]


def _dispatch(conf, att, sc, x, w_out_b, layer, alpha, ln_g, ln_b, wr, br, rows_buf, cnt0, cap):
    m, d_model = x.shape
    tm = _disp_tile(m)
    nt = m // tm
    assert m % tm == 0 and tm % ROW_ALIGN == 0
    d_mix = w_out_b.shape[1]
    width = d_model + LANES
    rr = lax.broadcasted_iota(jnp.int32, (tm, tm), 0)
    cc = lax.broadcasted_iota(jnp.int32, (tm, tm), 1)
    ltri = (cc < rr).astype(BF16)
    row = lambda i: (i, 0)
    par3 = lambda i: (layer, 0, 0)
    return pl.pallas_call(
        functools.partial(_dispatch_kernel, alpha, cap),
        grid=(nt,),
        in_specs=[
            pl.BlockSpec((tm, conf.shape[1]), row),
            pl.BlockSpec((tm, att.shape[1]), row),
            pl.BlockSpec((tm, sc.shape[1]), row),
            pl.BlockSpec((tm, d_model), row),
            pl.BlockSpec((None, d_mix, d_model), par3),
            pl.BlockSpec((None, 1, d_model), par3),
            pl.BlockSpec((None, 1, d_model), par3),
            pl.BlockSpec((None, d_model, 2 * LANES), par3),
            pl.BlockSpec((None, 1, LANES), par3),
            pl.BlockSpec((tm, tm), lambda i: (0, 0)),
            pl.BlockSpec(memory_space=pltpu.SMEM),
            pl.BlockSpec(memory_space=pl.ANY),
        ],
        out_specs=[
            pl.BlockSpec((tm, d_model), row),
            pl.BlockSpec((tm, LANES), row),
            pl.BlockSpec(memory_space=pltpu.SMEM),
            pl.BlockSpec(memory_space=pl.ANY),
        ],
        input_output_aliases={11: 3},
        out_shape=[
            jax.ShapeDtypeStruct((m, d_model), F32),
            jax.ShapeDtypeStruct((m, LANES), F32),
            jax.ShapeDtypeStruct((nt + 1, N_GROUPS), jnp.int32),
            jax.ShapeDtypeStruct(rows_buf.shape, rows_buf.dtype),
        ],
        scratch_shapes=[
            pltpu.VMEM((2, _sort_rows(tm) + tm, width), BF16),
            pltpu.SMEM((N_GROUPS,), jnp.int32),
            pltpu.SemaphoreType.DMA((2, N_GROUPS)),
        ],
        compiler_params=_cparams(("arbitrary",)),
        name="dispatch",
    )(conf, att, sc, x, w_out_b, ln_g, ln_b, wr, br, ltri, cnt0, rows_buf)


def _expert_kernel(cpe, tg_ref, tb_ref, tn_ref, tot_ref, xs_ref, wg_ref, wu_ref, wd_ref, ys_ref,
                   acc_ref, wgb_ref, wub_ref, wdb_ref):
    t = pl.program_id(0)
    c = pl.program_id(1)
    nc = EXPERTS_PER_GROUP * cpe
    d_model = wg_ref.shape[0]

    @pl.when(t < tot_ref[0])
    def _():
        @pl.when(c == 0)
        def _():
            acc_ref[...] = jnp.zeros(acc_ref.shape, F32)

        wgb_ref[...] = wg_ref[...].astype(BF16)
        wub_ref[...] = wu_ref[...].astype(BF16)
        wdb_ref[...] = wd_ref[...].astype(BF16)
        e_local = c // cpe

        def sub(r0, rows):
            xb = xs_ref[pl.ds(r0, rows), 0:d_model]
            gates = xs_ref[pl.ds(r0, rows), d_model:d_model + LANES].astype(F32)
            lane = lax.broadcasted_iota(jnp.int32, gates.shape, 1)
            mine = (lane == e_local) | (lane == e_local + EXPERTS_PER_GROUP)
            ce = jnp.sum(jnp.where(mine, gates, 0.0), axis=1, keepdims=True)
            gate = jnp.dot(xb, wgb_ref[...], preferred_element_type=F32)
            up = jnp.dot(xb, wub_ref[...], preferred_element_type=F32)
            hid = (gate * jax.nn.sigmoid(gate)) * up * ce
            acc_ref[pl.ds(r0, rows), :] += jnp.dot(hid.astype(BF16), wdb_ref[...],
                                                   preferred_element_type=F32)

        nsub = tn_ref[t]

        def pair(s, carry):
            sub(pl.multiple_of(s * (2 * EXP_SUB), 2 * EXP_SUB), 2 * EXP_SUB)
            return carry

        lax.fori_loop(0, lax.shift_right_logical(nsub, 1), pair, 0)

        @pl.when((nsub & 1) == 1)
        def _():
            sub(pl.multiple_of((nsub - 1) * EXP_SUB, EXP_SUB), EXP_SUB)

        @pl.when(c == nc - 1)
        def _():
            ys_ref[:, 0:d_model] = acc_ref[...].astype(ys_ref.dtype)
            ys_ref[:, d_model:d_model + LANES] = xs_ref[:, d_model:d_model + LANES]


def _experts(xs, tile_g, tile_blk, tile_nsub, total, wg, wu, wd, layer, cap):
    width = xs.shape[1]
    d_model = width - LANES
    d_exp = wg.shape[-1]
    t_max = tile_g.shape[0]
    cpe = d_exp // EXP_HC
    nc = EXPERTS_PER_GROUP * cpe
    blocks_per_group = cap // EXP_TM

    def chunk(t, c, tot):
        return jnp.where(t < tot[0], c, nc - 1)

    def rows(t, c, tg, tb, tn, tot):
        return (tg[t] * blocks_per_group + tb[t], 0)

    def w_in_map(t, c, tg, tb, tn, tot):
        ce = chunk(t, c, tot)
        return (layer, tg[t] * EXPERTS_PER_GROUP + ce // cpe, 0, ce % cpe)

    def w_out_map(t, c, tg, tb, tn, tot):
        ce = chunk(t, c, tot)
        return (layer, tg[t] * EXPERTS_PER_GROUP + ce // cpe, ce % cpe, 0)

    grid_spec = pltpu.PrefetchScalarGridSpec(
        num_scalar_prefetch=4,
        grid=(t_max, nc),
        in_specs=[
            pl.BlockSpec((EXP_TM, width), rows),
            pl.BlockSpec((None, None, d_model, EXP_HC), w_in_map),
            pl.BlockSpec((None, None, d_model, EXP_HC), w_in_map),
            pl.BlockSpec((None, None, EXP_HC, d_model), w_out_map),
        ],
        out_specs=pl.BlockSpec((EXP_TM, width), rows),
        scratch_shapes=[
            pltpu.VMEM((EXP_TM, d_model), F32),
            pltpu.VMEM((d_model, EXP_HC), BF16),
            pltpu.VMEM((d_model, EXP_HC), BF16),
            pltpu.VMEM((EXP_HC, d_model), BF16),
        ],
    )
    return pl.pallas_call(
        functools.partial(_expert_kernel, cpe),
        grid_spec=grid_spec,
        out_shape=jax.ShapeDtypeStruct(xs.shape, xs.dtype),
        input_output_aliases={4: 0},
        compiler_params=_cparams(("arbitrary", "arbitrary")),
        name="experts",
    )(tile_g, tile_blk, tile_nsub, total, xs, wg, wu, wd)


def _combine_kernel(alpha, ts_ref, back_ref, x1_ref, info_ref, *refs):
    wins = refs[:N_GROUPS]
    g_ref, b_ref, o_ref = refs[N_GROUPS:]
    i = pl.program_id(0)
    tm = x1_ref.shape[0]
    info = info_ref[...]
    gidx = info[:, 0:1]
    rank = info[:, 1:2]
    col = lax.broadcasted_iota(jnp.int32, (tm, tm), 1).astype(F32)
    y = jnp.zeros(x1_ref.shape, F32)
    for g in range(N_GROUPS):
        row_in_window = rank + back_ref[i * N_GROUPS + g].astype(F32)
        pick = jnp.where((gidx == float(g)) & (col == row_in_window), 1.0, 0.0).astype(BF16)
        y = y + jnp.dot(pick, wins[g][...], preferred_element_type=F32)
    o_ref[...] = _layer_norm(alpha * x1_ref[...] + y, g_ref[...], b_ref[...])


def _combine(x1, info, tstart, ends, ys, layer, alpha, ln_g, ln_b, cap):
    m, d_model = x1.shape
    tm = _disp_tile(m)
    last = _tiles_per_group(ends) * EXP_TM - tm
    ts = jnp.minimum(tstart, last[None, :]).reshape(-1)
    back = tstart.reshape(-1) - ts

    def window(g):
        def index_map(i, ts_ref, back_ref):
            return (pl.multiple_of(g * cap + ts_ref[i * N_GROUPS + g], ROW_ALIGN), 0)
        return pl.BlockSpec((pl.Element(tm), pl.Element(d_model)), index_map)

    row = lambda i, ts_ref, back_ref: (i, 0)
    par3 = lambda i, ts_ref, back_ref: (layer, 0, 0)
    grid_spec = pltpu.PrefetchScalarGridSpec(
        num_scalar_prefetch=2,
        grid=(m // tm,),
        in_specs=[pl.BlockSpec((tm, d_model), row), pl.BlockSpec((tm, LANES), row)]
        + [window(g) for g in range(N_GROUPS)]
        + [pl.BlockSpec((None, 1, d_model), par3), pl.BlockSpec((None, 1, d_model), par3)],
        out_specs=pl.BlockSpec((tm, d_model), row),
    )
    return pl.pallas_call(
        functools.partial(_combine_kernel, alpha),
        grid_spec=grid_spec,
        out_shape=jax.ShapeDtypeStruct((m, d_model), F32),
        compiler_params=_cparams(("arbitrary",)),
        name="combine",
    )(ts, back, x1, info, *([ys] * N_GROUPS), ln_g, ln_b)


def _tiles_per_group(ends):
    return jnp.maximum((ends + EXP_TM - 1) // EXP_TM, 1)


def _expert_tiles(ends, t_max):
    per_group = _tiles_per_group(ends)
    cum = jnp.cumsum(per_group)
    total = cum[-1:]
    t = jnp.minimum(jnp.arange(t_max, dtype=jnp.int32), total[0] - 1)
    tile_g = jnp.sum((t[:, None] >= cum[None, :]).astype(jnp.int32), axis=1)
    tile_blk = t - (cum - per_group)[tile_g]
    valid = jnp.clip(ends[tile_g] - tile_blk * EXP_TM, 0, EXP_TM)
    tile_nsub = (valid + EXP_SUB - 1) // EXP_SUB
    return tile_g, tile_blk, tile_nsub.astype(jnp.int32), total.astype(jnp.int32)


def kernel(x_prompt, x_sample, cache_k, cache_v, state_conv, state_sconv, page_table, w_in, conv_w, conv_b, conv_ln_g, conv_ln_b, sconv_w, sb_bias, w_out, ln1_g, ln1_b, router_group_w, router_group_b, router_expert_w, router_expert_b, expert_w_gate, expert_w_up, expert_w_down, ln2_g, ln2_b):
    batch, seq, d_model = x_prompt.shape
    nb = x_sample.shape[0]
    depth = w_in.shape[0]
    d_conf = conv_w.shape[-1]
    d_sc = sconv_w.shape[-1]
    heads = sb_bias.shape[-1]
    d_sb = heads * SB_HEAD_DIM
    alpha = (2 * depth) ** 0.25

    row3 = lambda a: a.reshape(depth, 1, a.shape[-1])
    conv_b3, conv_ln_g3, conv_ln_b3 = row3(conv_b), row3(conv_ln_g), row3(conv_ln_b)
    ln1_g3, ln1_b3, ln2_g3, ln2_b3 = row3(ln1_g), row3(ln1_b), row3(ln2_g), row3(ln2_b)

    n_log = N_GROUPS + N_EXPERTS
    wr = jnp.concatenate(
        [router_group_w, router_expert_w.transpose(0, 2, 1, 3).reshape(depth, d_model, N_EXPERTS)], axis=-1)
    wr = jnp.pad(wr, ((0, 0), (0, 0), (0, LANES - n_log)))
    wr_hi = wr.astype(BF16)
    wr_parts = jnp.concatenate([wr_hi, (wr - wr_hi.astype(F32)).astype(BF16)], axis=-1)
    br = jnp.concatenate([router_group_b, router_expert_b.reshape(depth, N_EXPERTS)], axis=-1)
    br = jnp.pad(br, ((0, 0), (0, LANES - n_log))).reshape(depth, 1, LANES)

    w_out_b = w_out.astype(BF16)
    experts = (expert_w_gate, expert_w_up, expert_w_down)
    cap, t_max = _moe_layout((batch * seq, nb))
    rows = jnp.zeros((N_GROUPS * cap, d_model + LANES), BF16)
    no_rows = jnp.zeros((N_GROUPS,), jnp.int32)

    cs_t = state_conv.transpose(0, 2, 1, 3)
    ss_t = state_sconv.transpose(0, 2, 1, 3)

    yp = x_prompt.reshape(batch * seq, d_model)
    ys = x_sample.reshape(nb, d_model)
    kp, vp, cp, sp, kd, vd, cd, sd = [], [], [], [], [], [], [], []
    for l in range(depth):
        router = (w_out_b, l, alpha, ln1_g3, ln1_b3, wr_parts, br)
        cg, q, k, v, gch = _inproj(yp, w_in, l, d_conf, d_sb, d_sc)
        conf, sc, cst, sst = _mixer_prompt(cg, gch, batch, seq, l, conv_w, conv_b3, conv_ln_g3, conv_ln_b3,
                                           sconv_w)
        att = _attn_prompt(q, k, v, sb_bias[l] * LOG2_E, batch, seq)
        x1p, info_p, tstart_p, rows = _dispatch(conf, att, sc, yp, *router, rows, no_rows, cap)
        kp.append(k.reshape(batch, seq, heads, SB_HEAD_DIM))
        vp.append(v.reshape(batch, seq, heads, SB_HEAD_DIM))
        cp.append(cst[:, CONV_HIST - (CONF_WIDTH - 1):, :])
        sp.append(sst[:, SC_HIST - (SC_WIDTH - 1):, :])

        cg, q, k, v, gch = _inproj(ys, w_in, l, d_conf, d_sb, d_sc)
        conf, sc, ncs, nss = _mixer_sample(cg, gch, cs_t, ss_t, l, conv_w, conv_b3, conv_ln_g3, conv_ln_b3,
                                           sconv_w)
        att = _attn_sample(q, sb_bias[l] * LOG2_E, cache_k, cache_v, page_table, l)
        x1s, info_s, tstart_s, rows = _dispatch(conf, att, sc, ys, *router, rows, tstart_p[-1], cap)
        kd.append(k.reshape(nb, 1, heads, SB_HEAD_DIM))
        vd.append(v.reshape(nb, 1, heads, SB_HEAD_DIM))
        cd.append(ncs.transpose(1, 0, 2))
        sd.append(nss.transpose(1, 0, 2))

        ends = tstart_s[-1]
        rows = _experts(rows, *_expert_tiles(ends, t_max), *experts, l, cap)
        yp = _combine(x1p, info_p, tstart_p, ends, rows, l, alpha, ln2_g3, ln2_b3, cap)
        ys = _combine(x1s, info_s, tstart_s, ends, rows, l, alpha, ln2_g3, ln2_b3, cap)

    return (yp.reshape(batch, seq, d_model), ys.reshape(nb, 1, d_model),
            jnp.stack(kp), jnp.stack(vp), jnp.stack(cp), jnp.stack(sp),
            jnp.stack(kd), jnp.stack(vd), jnp.stack(cd), jnp.stack(sd))
```

```python
import functools
import math

import jax
import jax.numpy as jnp
from jax import lax
from jax.experimental import pallas as pl
from jax.experimental.pallas import tpu as pltpu

F32 = jnp.float32
BF16 = jnp.bfloat16

SB_HEAD_DIM = 128
CONF_WIDTH = 31
SC_WIDTH = 3
N_GROUPS = 4
EXPERTS_PER_GROUP = 4
N_EXPERTS = N_GROUPS * EXPERTS_PER_GROUP
LN_EPS = 1e-5
LOG2_E = math.log2(math.e)

LANES = 128
SUBLANES = 8
VMEM_LIMIT_MB = 56

INPROJ_TM = 1024
INPROJ_TN = 512
MIX_T = 512
MIX_RC = 32
CONV_HIST = 32
SC_HIST = 8
ATT_TQ = 1024
ATT_TK = 256
ATT_UNROLL = 4
PAGES_PER_STEP = 16
DISP_TM = 256
ROW_ALIGN = 16
EXP_TM = 1280
EXP_SUB = 256
EXP_HC = 256


def _cparams(sem):
    return pltpu.CompilerParams(dimension_semantics=sem,
                                vmem_limit_bytes=VMEM_LIMIT_MB * 1024 * 1024)


def _layer_norm(y, g, b):
    mu = jnp.mean(y, axis=-1, keepdims=True)
    d = y - mu
    var = jnp.mean(d * d, axis=-1, keepdims=True)
    return d * lax.rsqrt(var + LN_EPS) * g + b


def _softplus2(z):
    return jnp.maximum(z, 0.0) + jnp.log2(1.0 + jnp.exp2(-jnp.abs(z)))


def _split_bf16(x):
    hi = x.astype(BF16)
    lo = (x - hi.astype(F32)).astype(BF16)
    return hi, lo


def _inproj_kernel(bounds, scale, x_ref, w_ref, cg_ref, q_ref, k_ref, v_ref, gch_ref, xb_ref):
    j = pl.program_id(1)

    @pl.when(j == 0)
    def _():
        xb_ref[...] = x_ref[...].astype(BF16)

    acc = jnp.dot(xb_ref[...], w_ref[...], preferred_element_type=F32)
    b_cg, b_q, b_k, b_v = bounds

    @pl.when(j < b_cg)
    def _():
        cg_ref[...] = acc

    @pl.when((j >= b_cg) & (j < b_q))
    def _():
        q_ref[...] = (acc * scale).astype(BF16)

    @pl.when((j >= b_q) & (j < b_k))
    def _():
        k_ref[...] = acc

    @pl.when((j >= b_k) & (j < b_v))
    def _():
        v_ref[...] = acc

    @pl.when(j >= b_v)
    def _():
        gch_ref[...] = acc


def _inproj(x, w_in, layer, d_conf, d_sb, d_sc):
    m, d_model = x.shape
    p_in = w_in.shape[-1]
    tm = min(INPROJ_TM, m)
    tn = INPROJ_TN
    n_cg, n_sb, n_gch = 2 * d_conf // tn, d_sb // tn, 3 * d_sc // tn
    bounds = (n_cg, n_cg + n_sb, n_cg + 2 * n_sb, n_cg + 3 * n_sb)
    assert bounds[-1] + n_gch == p_in // tn and m % tm == 0

    def out_map(lo, n):
        return lambda i, j: (i, jnp.clip(j - lo, 0, n - 1))

    kernel = functools.partial(_inproj_kernel, bounds, SB_HEAD_DIM ** -0.5 * LOG2_E)
    return pl.pallas_call(
        kernel,
        grid=(m // tm, p_in // tn),
        in_specs=[
            pl.BlockSpec((tm, d_model), lambda i, j: (i, 0)),
            pl.BlockSpec((None, d_model, tn), lambda i, j: (layer, 0, j)),
        ],
        out_specs=[
            pl.BlockSpec((tm, tn), out_map(0, n_cg)),
            pl.BlockSpec((tm, tn), out_map(bounds[0], n_sb)),
            pl.BlockSpec((tm, tn), out_map(bounds[1], n_sb)),
            pl.BlockSpec((tm, tn), out_map(bounds[2], n_sb)),
            pl.BlockSpec((tm, tn), out_map(bounds[3], n_gch)),
        ],
        out_shape=[
            jax.ShapeDtypeStruct((m, 2 * d_conf), F32),
            jax.ShapeDtypeStruct((m, d_sb), BF16),
            jax.ShapeDtypeStruct((m, d_sb), F32),
            jax.ShapeDtypeStruct((m, d_sb), F32),
            jax.ShapeDtypeStruct((m, 3 * d_sc), F32),
        ],
        scratch_shapes=[pltpu.VMEM((tm, d_model), BF16)],
        compiler_params=_cparams(("arbitrary", "arbitrary")),
        name="inproj",
    )(x, w_in)


def _mixer_prompt_kernel(val_ref, gate_ref, gb_ref, gc_ref, h_ref, cw_ref, cb_ref, lg_ref, lb_ref, sw_ref,
                         conf_ref, sc_ref, cst_ref, sst_ref, ext_ref, ext2_ref, shift_ref):
    t = pl.program_id(1)
    tt, c = val_ref.shape

    @pl.when(t == 0)
    def _():
        ext_ref[0:CONV_HIST, :] = jnp.zeros((CONV_HIST, c), F32)
        ext2_ref[0:SC_HIST, :] = jnp.zeros((SC_HIST, ext2_ref.shape[1]), F32)

    ext_ref[CONV_HIST:CONV_HIST + tt, :] = val_ref[...] * jax.nn.sigmoid(gate_ref[...])
    ext2_ref[SC_HIST:SC_HIST + tt, :] = gc_ref[...] * h_ref[...]

    shifted_rows = shift_ref.shape[1]
    for b in range(1, SUBLANES):
        shift_ref[b - 1] = ext_ref[b:b + shifted_rows, :]

    def tap_rows(start, rows):
        b, base = start % SUBLANES, start - start % SUBLANES
        if b == 0:
            return ext_ref[base:base + rows, :]
        return shift_ref[b - 1, base:base + rows, :]

    cb = cb_ref[...]
    lg = lg_ref[...]
    lb = lb_ref[...]
    conv_off = CONV_HIST - (CONF_WIDTH - 1)
    sc_off = SC_HIST - (SC_WIDTH - 1)
    for ch in range(tt // MIX_RC):
        r0 = ch * MIX_RC
        acc = jnp.zeros((MIX_RC, c), F32)
        for w in range(CONF_WIDTH):
            acc = acc + tap_rows(r0 + conv_off + w, MIX_RC) * cw_ref[w:w + 1, :]
        yn = _layer_norm(acc + cb, lg, lb)
        conf_ref[r0:r0 + MIX_RC, :] = (yn * jax.nn.sigmoid(yn)).astype(conf_ref.dtype)
        s = jnp.zeros((MIX_RC, ext2_ref.shape[1]), F32)
        for w in range(SC_WIDTH):
            s = s + ext2_ref[r0 + sc_off + w:r0 + sc_off + w + MIX_RC, :] * sw_ref[w:w + 1, :]
        sc_ref[r0:r0 + MIX_RC, :] = (gb_ref[r0:r0 + MIX_RC, :] * s).astype(sc_ref.dtype)

    ext_ref[0:CONV_HIST, :] = ext_ref[tt:tt + CONV_HIST, :]
    ext2_ref[0:SC_HIST, :] = ext2_ref[tt:tt + SC_HIST, :]

    @pl.when(t == pl.num_programs(1) - 1)
    def _():
        cst_ref[...] = ext_ref[tt:tt + CONV_HIST, :]
        sst_ref[...] = ext2_ref[tt:tt + SC_HIST, :]


def _mixer_prompt(cg, gch, batch, seq, layer, conv_w, conv_b, conv_ln_g, conv_ln_b, sconv_w):
    d_conf = cg.shape[1] // 2
    d_sc = gch.shape[1] // 3
    tt = MIX_T
    nt = seq // tt
    row = lambda b, t: (b * nt + t, 0)
    col = lambda cidx: (lambda b, t: (b * nt + t, cidx))
    par3 = lambda b, t: (layer, 0, 0)
    return pl.pallas_call(
        _mixer_prompt_kernel,
        grid=(batch, nt),
        in_specs=[
            pl.BlockSpec((tt, d_conf), col(0)),
            pl.BlockSpec((tt, d_conf), col(1)),
            pl.BlockSpec((tt, d_sc), col(0)),
            pl.BlockSpec((tt, d_sc), col(1)),
            pl.BlockSpec((tt, d_sc), col(2)),
            pl.BlockSpec((None, CONF_WIDTH, d_conf), par3),
            pl.BlockSpec((None, 1, d_conf), par3),
            pl.BlockSpec((None, 1, d_conf), par3),
            pl.BlockSpec((None, 1, d_conf), par3),
            pl.BlockSpec((None, SC_WIDTH, d_sc), par3),
        ],
        out_specs=[
            pl.BlockSpec((tt, d_conf), row),
            pl.BlockSpec((tt, d_sc), row),
            pl.BlockSpec((None, CONV_HIST, d_conf), lambda b, t: (b, 0, 0)),
            pl.BlockSpec((None, SC_HIST, d_sc), lambda b, t: (b, 0, 0)),
        ],
        out_shape=[
            jax.ShapeDtypeStruct((batch * seq, d_conf), BF16),
            jax.ShapeDtypeStruct((batch * seq, d_sc), BF16),
            jax.ShapeDtypeStruct((batch, CONV_HIST, d_conf), F32),
            jax.ShapeDtypeStruct((batch, SC_HIST, d_sc), F32),
        ],
        scratch_shapes=[pltpu.VMEM((tt + CONV_HIST, d_conf), F32),
                        pltpu.VMEM((tt + SC_HIST, d_sc), F32),
                        pltpu.VMEM((SUBLANES - 1, tt + CONV_HIST - SUBLANES, d_conf), F32)],
        compiler_params=_cparams(("arbitrary", "arbitrary")),
        name="mixer_prompt",
    )(cg, cg, gch, gch, gch, conv_w, conv_b, conv_ln_g, conv_ln_b, sconv_w)


def _mixer_sample_kernel(cg_ref, gch_ref, cs_ref, ss_ref, cw_ref, cb_ref, lg_ref, lb_ref, sw_ref,
                         conf_ref, sc_ref, ncs_ref, nss_ref):
    c = cs_ref.shape[-1]
    c2 = ss_ref.shape[-1]
    hist = CONF_WIDTH - 1
    glu = cg_ref[:, 0:c] * jax.nn.sigmoid(cg_ref[:, c:2 * c])
    acc = glu * cw_ref[hist:hist + 1, :]
    for w in range(hist):
        acc = acc + cs_ref[w] * cw_ref[w:w + 1, :]
    yn = _layer_norm(acc + cb_ref[...], lg_ref[...], lb_ref[...])
    conf_ref[...] = (yn * jax.nn.sigmoid(yn)).astype(conf_ref.dtype)

    p = gch_ref[:, c2:2 * c2] * gch_ref[:, 2 * c2:3 * c2]
    s = p * sw_ref[SC_WIDTH - 1:SC_WIDTH, :]
    for w in range(SC_WIDTH - 1):
        s = s + ss_ref[w] * sw_ref[w:w + 1, :]
    sc_ref[...] = (gch_ref[:, 0:c2] * s).astype(sc_ref.dtype)

    for w in range(hist - 1):
        ncs_ref[w] = cs_ref[w + 1]
    ncs_ref[hist - 1] = glu
    for w in range(SC_WIDTH - 2):
        nss_ref[w] = ss_ref[w + 1]
    nss_ref[SC_WIDTH - 2] = p


def _mixer_sample(cg, gch, cs_t, ss_t, layer, conv_w, conv_b, conv_ln_g, conv_ln_b, sconv_w):
    nb = cg.shape[0]
    d_conf = cg.shape[1] // 2
    d_sc = gch.shape[1] // 3
    full2 = lambda i: (0, 0)
    par3 = lambda i: (layer, 0, 0)
    st4 = lambda i: (layer, 0, 0, 0)
    return pl.pallas_call(
        _mixer_sample_kernel,
        grid=(1,),
        in_specs=[
            pl.BlockSpec(cg.shape, full2),
            pl.BlockSpec(gch.shape, full2),
            pl.BlockSpec((None,) + cs_t.shape[1:], st4),
            pl.BlockSpec((None,) + ss_t.shape[1:], st4),
            pl.BlockSpec((None, CONF_WIDTH, d_conf), par3),
            pl.BlockSpec((None, 1, d_conf), par3),
            pl.BlockSpec((None, 1, d_conf), par3),
            pl.BlockSpec((None, 1, d_conf), par3),
            pl.BlockSpec((None, SC_WIDTH, d_sc), par3),
        ],
        out_specs=[
            pl.BlockSpec((nb, d_conf), full2),
            pl.BlockSpec((nb, d_sc), full2),
            pl.BlockSpec(cs_t.shape[1:], lambda i: (0, 0, 0)),
            pl.BlockSpec(ss_t.shape[1:], lambda i: (0, 0, 0)),
        ],
        out_shape=[
            jax.ShapeDtypeStruct((nb, d_conf), BF16),
            jax.ShapeDtypeStruct((nb, d_sc), BF16),
            jax.ShapeDtypeStruct(cs_t.shape[1:], F32),
            jax.ShapeDtypeStruct(ss_t.shape[1:], F32),
        ],
        compiler_params=_cparams(("arbitrary",)),
        name="mixer_sample",
    )(cg, gch, cs_t, ss_t, conv_w, conv_b, conv_ln_g, conv_ln_b, sconv_w)


def _attn_prompt_kernel(bias_ref, q_ref, k_ref, v_ref, tri_ref, o_ref, kb_ref, vt_ref):
    h = pl.program_id(1)
    i = pl.program_id(2)
    tq = q_ref.shape[0]
    tk = tri_ref.shape[0]
    ratio = tq // tk

    @pl.when(i == 0)
    def _():
        kb_ref[...] = k_ref[...].astype(BF16)
        for kb in range(vt_ref.shape[0]):
            vt_ref[kb] = v_ref[kb * tk:(kb + 1) * tk, :].T.astype(BF16)

    bias = bias_ref[h]
    q = q_ref[...]
    tri2 = tri_ref[...]

    def blocks(kbs, q0s, acc, run, masked):
        zs, splits, csums, avs = [], [], [], []

        def causal(x):
            sq = jnp.where(lax.broadcasted_iota(jnp.int32, (tk, tk), 0)
                           < lax.broadcasted_iota(jnp.int32, (tk, tk), 1), x[:, :tk], 0.0)
            return sq if x.shape[1] == tk else jnp.concatenate([sq, x[:, tk:]], axis=1)

        for kb, q0 in zip(kbs, q0s):
            start = pl.multiple_of(kb * tk, tk)
            kblk = kb_ref[pl.ds(start, tk), :]
            zs.append(lax.dot_general(kblk, q[q0:, :], (((1,), (1,)), ((), ())),
                                      preferred_element_type=F32) + bias)
        for z in zs:
            sp = _softplus2(z)
            if masked:
                sp = causal(sp)
            hi, lo = _split_bf16(sp)
            splits.append(jnp.concatenate([hi, lo], axis=0))
        for w in splits:
            csums.append(jnp.dot(tri2, w, preferred_element_type=F32))
        def from_q0(full, q0, tail):
            return tail if q0 == 0 else jnp.concatenate([full[:, :q0], tail], axis=1)

        for j, q0 in enumerate(q0s):
            csum = csums[j] + run[:, q0:]
            run = from_q0(run, q0, csum[0:1, :])
            a = jnp.exp2(zs[j] - csum)
            if masked:
                a = causal(a)
            avs.append(jnp.dot(vt_ref[kbs[j]], a.astype(BF16), preferred_element_type=F32))
        for q0, av in zip(q0s, avs):
            acc = from_q0(acc, q0, acc[:, q0:] + av)
        return acc, run

    acc = jnp.zeros((q_ref.shape[1], tq), F32)
    run = jnp.zeros((1, tq), F32)
    diag = list(reversed(range(ratio)))
    acc, run = blocks([i * ratio + c for c in diag], [c * tk for c in diag], acc, run, True)

    def body(jj, carry):
        first = i * ratio - 1 - jj * ATT_UNROLL
        return blocks([first - d for d in range(ATT_UNROLL)], [0] * ATT_UNROLL, carry[0], carry[1], False)

    acc, run = lax.fori_loop(0, i * ratio // ATT_UNROLL, body, (acc, run))
    o_ref[...] = acc.T.astype(o_ref.dtype)


def _attn_prompt(q, k, v, bias, batch, seq):
    d_sb = q.shape[1]
    heads = d_sb // SB_HEAD_DIM
    tq, tk = ATT_TQ, ATT_TK
    nq = seq // tq
    assert seq % tq == 0 and (tq // tk) % ATT_UNROLL == 0
    r = lax.broadcasted_iota(jnp.int32, (tk, tk), 0)
    c = lax.broadcasted_iota(jnp.int32, (tk, tk), 1)
    upper = (c >= r).astype(BF16)
    tri = jnp.concatenate([upper, upper], axis=1)
    grid_spec = pltpu.PrefetchScalarGridSpec(
        num_scalar_prefetch=0,
        grid=(batch, heads, nq),
        in_specs=[
            pl.BlockSpec(memory_space=pltpu.SMEM),
            pl.BlockSpec((tq, SB_HEAD_DIM), lambda b, h, i: (b * nq + i, h)),
            pl.BlockSpec((seq, SB_HEAD_DIM), lambda b, h, i: (b, h)),
            pl.BlockSpec((seq, SB_HEAD_DIM), lambda b, h, i: (b, h)),
            pl.BlockSpec((tk, 2 * tk), lambda b, h, i: (0, 0)),
        ],
        out_specs=pl.BlockSpec((tq, SB_HEAD_DIM), lambda b, h, i: (b * nq + i, h)),
        scratch_shapes=[pltpu.VMEM((seq, SB_HEAD_DIM), BF16),
                        pltpu.VMEM((seq // tk, SB_HEAD_DIM, tk), BF16)],
    )
    return pl.pallas_call(
        _attn_prompt_kernel,
        grid_spec=grid_spec,
        out_shape=jax.ShapeDtypeStruct((batch * seq, d_sb), BF16),
        compiler_params=_cparams(("arbitrary", "arbitrary", "arbitrary")),
        name="attn_prompt",
    )(bias, q, k, v, tri)


def _attn_sample_kernel(layer, n_pages, pt_ref, qbd_ref, bias_ref, tri_ref, ck_ref, cv_ref, o_ref,
                        acc_ref, run_ref, kbuf, vbuf, sem):
    b = pl.program_id(0)
    s = pl.program_id(1)
    pps = kbuf.shape[1]
    ns = n_pages // pps
    step = b * ns + s
    slot = step % 2
    heads = acc_ref.shape[1] // kbuf.shape[3]
    page = kbuf.shape[2] // heads

    def page_copies(at_step, sl):
        bb = at_step // ns
        first = bb * n_pages + (n_pages - 1) - (at_step - bb * ns) * pps
        copies = []
        for r in range(pps):
            pg = pt_ref[first - r]
            copies.append(pltpu.make_async_copy(ck_ref.at[layer, pg], kbuf.at[sl, r], sem.at[sl, 0]))
            copies.append(pltpu.make_async_copy(cv_ref.at[layer, pg], vbuf.at[sl, r], sem.at[sl, 1]))
        return copies

    @pl.when(step == 0)
    def _():
        for cp in page_copies(step, slot):
            cp.start()

    @pl.when(step + 1 < pl.num_programs(0) * ns)
    def _():
        for cp in page_copies(step + 1, 1 - slot):
            cp.start()

    for cp in page_copies(step, slot):
        cp.wait()

    @pl.when(s == 0)
    def _():
        acc_ref[...] = jnp.zeros(acc_ref.shape, F32)
        run_ref[...] = jnp.zeros(run_ref.shape, F32)

    def flat_page(buf, r):
        return jnp.concatenate([buf[slot, r, pl.ds(h, page, stride=heads), :] for h in range(heads)],
                               axis=1).astype(BF16)

    vc = jnp.concatenate([flat_page(vbuf, r) for r in range(pps)], axis=0)
    half = pps // 2
    z_halves = [jnp.dot(jnp.concatenate([flat_page(kbuf, r) for r in range(lo, lo + half)], axis=0),
                        qbd_ref[...], preferred_element_type=F32) + bias_ref[...] for lo in (0, half)]
    tri = tri_ref[...]
    run = run_ref[...]
    a_parts = []
    for r in range(pps):
        z = z_halves[r // half][(r % half) * page:(r % half + 1) * page, :]
        hi, lo = _split_bf16(_softplus2(z))
        csum = (jnp.dot(tri, hi, preferred_element_type=F32)
                + jnp.dot(tri, lo, preferred_element_type=F32)) + run
        a_parts.append(jnp.exp2(z - csum).astype(BF16))
        run = csum[0:1, :]
    run_ref[...] = run
    a_all = jnp.concatenate(a_parts, axis=0)
    acc_ref[...] += lax.dot_general(a_all, vc, (((0,), (0,)), ((), ())), preferred_element_type=F32)

    @pl.when(s == pl.num_programs(1) - 1)
    def _():
        top = acc_ref[0:heads, :]
        row = lax.broadcasted_iota(jnp.int32, top.shape, 0)
        col = lax.broadcasted_iota(jnp.int32, top.shape, 1)
        lo_c = row * SB_HEAD_DIM
        keep = (col >= lo_c) & (col < lo_c + SB_HEAD_DIM)
        o_ref[...] = jnp.sum(jnp.where(keep, top, 0.0), axis=0, keepdims=True).astype(o_ref.dtype)


def _attn_sample(q, bias, cache_k, cache_v, page_table, layer):
    nb, d_sb = q.shape
    heads = d_sb // SB_HEAD_DIM
    depth, n_pool, page = cache_k.shape[:3]
    n_pages = page_table.shape[1]
    pps = PAGES_PER_STEP
    assert n_pages % pps == 0 and heads == SUBLANES
    pt = page_table.reshape(-1)
    ck = cache_k.reshape(depth, n_pool, page * heads, SB_HEAD_DIM)
    cv = cache_v.reshape(depth, n_pool, page * heads, SB_HEAD_DIM)
    r = lax.broadcasted_iota(jnp.int32, (d_sb, LANES), 0) // SB_HEAD_DIM
    c = lax.broadcasted_iota(jnp.int32, (d_sb, LANES), 1)
    qbd = jnp.where((r == c)[None], q[:, :, None], jnp.zeros((), q.dtype))
    bias_row = jnp.zeros((1, LANES), F32).at[0, :heads].set(bias)
    rr = lax.broadcasted_iota(jnp.int32, (page, page), 0)
    cc = lax.broadcasted_iota(jnp.int32, (page, page), 1)
    tri = (cc >= rr).astype(BF16)

    grid_spec = pltpu.PrefetchScalarGridSpec(
        num_scalar_prefetch=1,
        grid=(nb, n_pages // pps),
        in_specs=[
            pl.BlockSpec((None, d_sb, LANES), lambda b, s, pt_ref: (b, 0, 0)),
            pl.BlockSpec((1, LANES), lambda b, s, pt_ref: (0, 0)),
            pl.BlockSpec((page, page), lambda b, s, pt_ref: (0, 0)),
            pl.BlockSpec(memory_space=pl.ANY),
            pl.BlockSpec(memory_space=pl.ANY),
        ],
        out_specs=pl.BlockSpec((None, 1, d_sb), lambda b, s, pt_ref: (b, 0, 0)),
        scratch_shapes=[
            pltpu.VMEM((LANES, d_sb), F32),
            pltpu.VMEM((1, LANES), F32),
            pltpu.VMEM((2, pps, page * heads, SB_HEAD_DIM), cache_k.dtype),
            pltpu.VMEM((2, pps, page * heads, SB_HEAD_DIM), cache_v.dtype),
            pltpu.SemaphoreType.DMA((2, 2)),
        ],
    )
    out = pl.pallas_call(
        functools.partial(_attn_sample_kernel, layer, n_pages),
        grid_spec=grid_spec,
        out_shape=jax.ShapeDtypeStruct((nb, 1, d_sb), BF16),
        compiler_params=_cparams(("arbitrary", "arbitrary")),
        name="attn_sample",
    )(pt, qbd, bias_row, tri, ck, cv)
    return out.reshape(nb, d_sb)


def _route(logits):
    lane = lax.broadcasted_iota(jnp.int32, logits.shape, 1).astype(F32)
    neg = -jnp.inf
    far = float(LANES)
    is_group = lane < N_GROUPS
    lg = jnp.where(is_group, logits, neg)
    gmax = jnp.max(lg, axis=1, keepdims=True)
    gidx = jnp.min(jnp.where(lg == gmax, lane, far), axis=1, keepdims=True)
    den = jnp.sum(jnp.where(is_group, jnp.exp(logits - gmax), 0.0), axis=1, keepdims=True)
    ggate = 1.0 / den
    first = N_GROUPS + gidx * EXPERTS_PER_GROUP
    le = jnp.where((lane >= first) & (lane < first + EXPERTS_PER_GROUP), logits, neg)
    v1 = jnp.max(le, axis=1, keepdims=True)
    i1 = jnp.min(jnp.where(le == v1, lane, far), axis=1, keepdims=True)
    le2 = jnp.where(lane == i1, neg, le)
    v2 = jnp.max(le2, axis=1, keepdims=True)
    i2 = jnp.min(jnp.where(le2 == v2, lane, far), axis=1, keepdims=True)
    e21 = jnp.exp(v2 - v1)
    p1 = 1.0 / (1.0 + e21)
    p2 = e21 * p1
    return lane, gidx, (i1 - N_GROUPS, ggate * p1), (i2 - N_GROUPS, ggate * p2)


def _router_logits(x1, wr_ref, br_ref):
    hi, lo = _split_bf16(x1)
    both = jnp.dot(hi, wr_ref[...], preferred_element_type=F32)
    logits = (both[:, :LANES] + both[:, LANES:]
              + jnp.dot(lo, wr_ref[:, :LANES], preferred_element_type=F32)) + br_ref[...]
    return hi, logits


def _disp_tile(n_tokens):
    return min(DISP_TM, n_tokens)


def _moe_layout(token_sets):
    tiles = sum(n // _disp_tile(n) for n in token_sets)
    max_rows = sum(token_sets) + tiles * (ROW_ALIGN - 1)
    cap = pl.cdiv(max_rows + DISP_TM, EXP_TM) * EXP_TM
    t_max = pl.cdiv(max_rows, EXP_TM) + N_GROUPS
    return cap, t_max


def _sort_rows(tm):
    return pl.cdiv(tm + N_GROUPS * (ROW_ALIGN - 1), LANES) * LANES


def _dispatch_kernel(alpha, cap, conf_ref, att_ref, sc_ref, x_ref, w_ref, g_ref, b_ref, wr_ref,
                     br_ref, ltri_ref, cnt0_ref, rows_in_ref, x1_ref, info_ref, tstart_ref, xs_ref,
                     sbuf, cnt_ref, sem):
    del rows_in_ref
    i = pl.program_id(0)
    nt = pl.num_programs(0)
    slot = i % 2
    tm = x_ref.shape[0]
    sort_rows = sbuf.shape[1] - tm

    @pl.when(i == 0)
    def _():
        for g in range(N_GROUPS):
            cnt_ref[g] = cnt0_ref[g]
        sbuf[...] = jnp.zeros(sbuf.shape, BF16)

    cat = jnp.concatenate([conf_ref[...], att_ref[...], sc_ref[...]], axis=1)
    mix = jnp.dot(cat, w_ref[...], preferred_element_type=F32)
    x1 = _layer_norm(alpha * x_ref[...] + mix, g_ref[...], b_ref[...])
    x1_ref[...] = x1
    hi, logits = _router_logits(x1, wr_ref, br_ref)
    lane, gidx, (e1, w1), (e2, w2) = _route(logits)

    first = gidx * EXPERTS_PER_GROUP
    l1, l2 = e1 - first, e2 - first
    w1h = w1.astype(BF16).astype(F32)
    w2h = w2.astype(BF16).astype(F32)
    gates = jnp.where(lane == l1, w1h, jnp.where(lane == l2, w2h, 0.0))
    gates = jnp.where(lane == l1 + EXPERTS_PER_GROUP, w1 - w1h,
                      jnp.where(lane == l2 + EXPERTS_PER_GROUP, w2 - w2h, gates))
    xrow = jnp.concatenate([hi, gates.astype(BF16)], axis=1)

    onehot = jnp.where(lane == gidx, 1.0, 0.0)
    before = jnp.dot(ltri_ref[...], onehot.astype(BF16), preferred_element_type=F32)
    rank = jnp.sum(jnp.where(lane == gidx, before, 0.0), axis=1, keepdims=True)
    lens = (before[tm - 1:tm, :] + onehot[tm - 1:tm, :]).astype(jnp.int32)
    info_ref[...] = jnp.where(lane == 0.0, gidx, jnp.where(lane == 1.0, rank, 0.0))

    lbase = jnp.int32(0)
    lbases, pads = [], []
    dest = rank
    for g in range(N_GROUPS):
        pad = ((lens[0, g] + (ROW_ALIGN - 1)) // ROW_ALIGN) * ROW_ALIGN
        lbases.append(lbase)
        pads.append(pad)
        dest = dest + jnp.where(gidx == float(g), lbase.astype(F32), 0.0)
        lbase = lbase + pad
    col = lax.broadcasted_iota(jnp.int32, (tm, sort_rows), 1).astype(F32)
    onehot_dest = jnp.where(col == dest, 1.0, 0.0).astype(BF16)
    srt = lax.dot_general(onehot_dest, xrow, (((0,), (0,)), ((), ())), preferred_element_type=F32)
    sbuf[slot, 0:sort_rows, :] = srt.astype(BF16)

    def copy(sl, g, src, dst):
        return pltpu.make_async_copy(
            sbuf.at[sl, pl.ds(pl.multiple_of(src, ROW_ALIGN), tm)],
            xs_ref.at[pl.ds(pl.multiple_of(dst, ROW_ALIGN), tm)], sem.at[sl, g])

    @pl.when(i > 0)
    def _():
        for g in range(N_GROUPS):
            copy(1 - slot, g, 0, 0).wait()

    for g in range(N_GROUPS):
        start = cnt_ref[g]
        copy(slot, g, lbases[g], g * cap + start).start()
        tstart_ref[i, g] = start
        cnt_ref[g] = start + pads[g]

    @pl.when(i == nt - 1)
    def _():
        for g in range(N_GROUPS):
            copy(slot, g, 0, 0).wait()
            tstart_ref[nt, g] = cnt_ref[g]


def _dispatch(conf, att, sc, x, w_out_b, layer, alpha, ln_g, ln_b, wr, br, rows_buf, cnt0, cap):
    m, d_model = x.shape
    tm = _disp_tile(m)
    nt = m // tm
    assert m % tm == 0 and tm % ROW_ALIGN == 0
    d_mix = w_out_b.shape[1]
    width = d_model + LANES
    rr = lax.broadcasted_iota(jnp.int32, (tm, tm), 0)
    cc = lax.broadcasted_iota(jnp.int32, (tm, tm), 1)
    ltri = (cc < rr).astype(BF16)
    row = lambda i: (i, 0)
    par3 = lambda i: (layer, 0, 0)
    return pl.pallas_call(
        functools.partial(_dispatch_kernel, alpha, cap),
        grid=(nt,),
        in_specs=[
            pl.BlockSpec((tm, conf.shape[1]), row),
            pl.BlockSpec((tm, att.shape[1]), row),
            pl.BlockSpec((tm, sc.shape[1]), row),
            pl.BlockSpec((tm, d_model), row),
            pl.BlockSpec((None, d_mix, d_model), par3),
            pl.BlockSpec((None, 1, d_model), par3),
            pl.BlockSpec((None, 1, d_model), par3),
            pl.BlockSpec((None, d_model, 2 * LANES), par3),
            pl.BlockSpec((None, 1, LANES), par3),
            pl.BlockSpec((tm, tm), lambda i: (0, 0)),
            pl.BlockSpec(memory_space=pltpu.SMEM),
            pl.BlockSpec(memory_space=pl.ANY),
        ],
        out_specs=[
            pl.BlockSpec((tm, d_model), row),
            pl.BlockSpec((tm, LANES), row),
            pl.BlockSpec(memory_space=pltpu.SMEM),
            pl.BlockSpec(memory_space=pl.ANY),
        ],
        input_output_aliases={11: 3},
        out_shape=[
            jax.ShapeDtypeStruct((m, d_model), F32),
            jax.ShapeDtypeStruct((m, LANES), F32),
            jax.ShapeDtypeStruct((nt + 1, N_GROUPS), jnp.int32),
            jax.ShapeDtypeStruct(rows_buf.shape, rows_buf.dtype),
        ],
        scratch_shapes=[
            pltpu.VMEM((2, _sort_rows(tm) + tm, width), BF16),
            pltpu.SMEM((N_GROUPS,), jnp.int32),
            pltpu.SemaphoreType.DMA((2, N_GROUPS)),
        ],
        compiler_params=_cparams(("arbitrary",)),
        name="dispatch",
    )(conf, att, sc, x, w_out_b, ln_g, ln_b, wr, br, ltri, cnt0, rows_buf)


def _expert_kernel(cpe, tg_ref, tb_ref, tn_ref, tot_ref, xs_ref, wg_ref, wu_ref, wd_ref, ys_ref,
                   acc_ref, wgb_ref, wub_ref, wdb_ref):
    t = pl.program_id(0)
    c = pl.program_id(1)
    nc = EXPERTS_PER_GROUP * cpe
    d_model = wg_ref.shape[0]

    @pl.when(t < tot_ref[0])
    def _():
        @pl.when(c == 0)
        def _():
            acc_ref[...] = jnp.zeros(acc_ref.shape, F32)

        wgb_ref[...] = wg_ref[...].astype(BF16)
        wub_ref[...] = wu_ref[...].astype(BF16)
        wdb_ref[...] = wd_ref[...].astype(BF16)
        e_local = c // cpe

        def sub(r0, rows):
            xb = xs_ref[pl.ds(r0, rows), 0:d_model]
            gates = xs_ref[pl.ds(r0, rows), d_model:d_model + LANES].astype(F32)
            lane = lax.broadcasted_iota(jnp.int32, gates.shape, 1)
            mine = (lane == e_local) | (lane == e_local + EXPERTS_PER_GROUP)
            ce = jnp.sum(jnp.where(mine, gates, 0.0), axis=1, keepdims=True)
            gate = jnp.dot(xb, wgb_ref[...], preferred_element_type=F32)
            up = jnp.dot(xb, wub_ref[...], preferred_element_type=F32)
            hid = (gate * jax.nn.sigmoid(gate)) * up * ce
            acc_ref[pl.ds(r0, rows), :] += jnp.dot(hid.astype(BF16), wdb_ref[...],
                                                   preferred_element_type=F32)

        nsub = tn_ref[t]

        def pair(s, carry):
            sub(pl.multiple_of(s * (2 * EXP_SUB), 2 * EXP_SUB), 2 * EXP_SUB)
            return carry

        lax.fori_loop(0, lax.shift_right_logical(nsub, 1), pair, 0)

        @pl.when((nsub & 1) == 1)
        def _():
            sub(pl.multiple_of((nsub - 1) * EXP_SUB, EXP_SUB), EXP_SUB)

        @pl.when(c == nc - 1)
        def _():
            ys_ref[:, 0:d_model] = acc_ref[...].astype(ys_ref.dtype)
            ys_ref[:, d_model:d_model + LANES] = xs_ref[:, d_model:d_model + LANES]


def _experts(xs, tile_g, tile_blk, tile_nsub, total, wg, wu, wd, layer, cap):
    width = xs.shape[1]
    d_model = width - LANES
    d_exp = wg.shape[-1]
    t_max = tile_g.shape[0]
    cpe = d_exp // EXP_HC
    nc = EXPERTS_PER_GROUP * cpe
    blocks_per_group = cap // EXP_TM

    def chunk(t, c, tot):
        return jnp.where(t < tot[0], c, nc - 1)

    def rows(t, c, tg, tb, tn, tot):
        return (tg[t] * blocks_per_group + tb[t], 0)

    def w_in_map(t, c, tg, tb, tn, tot):
        ce = chunk(t, c, tot)
        return (layer, tg[t] * EXPERTS_PER_GROUP + ce // cpe, 0, ce % cpe)

    def w_out_map(t, c, tg, tb, tn, tot):
        ce = chunk(t, c, tot)
        return (layer, tg[t] * EXPERTS_PER_GROUP + ce // cpe, ce % cpe, 0)

    grid_spec = pltpu.PrefetchScalarGridSpec(
        num_scalar_prefetch=4,
        grid=(t_max, nc),
        in_specs=[
            pl.BlockSpec((EXP_TM, width), rows),
            pl.BlockSpec((None, None, d_model, EXP_HC), w_in_map),
            pl.BlockSpec((None, None, d_model, EXP_HC), w_in_map),
            pl.BlockSpec((None, None, EXP_HC, d_model), w_out_map),
        ],
        out_specs=pl.BlockSpec((EXP_TM, width), rows),
        scratch_shapes=[
            pltpu.VMEM((EXP_TM, d_model), F32),
            pltpu.VMEM((d_model, EXP_HC), BF16),
            pltpu.VMEM((d_model, EXP_HC), BF16),
            pltpu.VMEM((EXP_HC, d_model), BF16),
        ],
    )
    return pl.pallas_call(
        functools.partial(_expert_kernel, cpe),
        grid_spec=grid_spec,
        out_shape=jax.ShapeDtypeStruct(xs.shape, xs.dtype),
        input_output_aliases={4: 0},
        compiler_params=_cparams(("arbitrary", "arbitrary")),
        name="experts",
    )(tile_g, tile_blk, tile_nsub, total, xs, wg, wu, wd)


def _combine_kernel(alpha, ts_ref, back_ref, x1_ref, info_ref, *refs):
    wins = refs[:N_GROUPS]
    g_ref, b_ref, o_ref = refs[N_GROUPS:]
    i = pl.program_id(0)
    tm = x1_ref.shape[0]
    info = info_ref[...]
    gidx = info[:, 0:1]
    rank = info[:, 1:2]
    col = lax.broadcasted_iota(jnp.int32, (tm, tm), 1).astype(F32)
    y = jnp.zeros(x1_ref.shape, F32)
    for g in range(N_GROUPS):
        row_in_window = rank + back_ref[i * N_GROUPS + g].astype(F32)
        pick = jnp.where((gidx == float(g)) & (col == row_in_window), 1.0, 0.0).astype(BF16)
        y = y + jnp.dot(pick, wins[g][...], preferred_element_type=F32)
    o_ref[...] = _layer_norm(alpha * x1_ref[...] + y, g_ref[...], b_ref[...])


def _combine(x1, info, tstart, ends, ys, layer, alpha, ln_g, ln_b, cap):
    m, d_model = x1.shape
    tm = _disp_tile(m)
    last = _tiles_per_group(ends) * EXP_TM - tm
    ts = jnp.minimum(tstart, last[None, :]).reshape(-1)
    back = tstart.reshape(-1) - ts

    def window(g):
        def index_map(i, ts_ref, back_ref):
            return (pl.multiple_of(g * cap + ts_ref[i * N_GROUPS + g], ROW_ALIGN), 0)
        return pl.BlockSpec((pl.Element(tm), pl.Element(d_model)), index_map)

    row = lambda i, ts_ref, back_ref: (i, 0)
    par3 = lambda i, ts_ref, back_ref: (layer, 0, 0)
    grid_spec = pltpu.PrefetchScalarGridSpec(
        num_scalar_prefetch=2,
        grid=(m // tm,),
        in_specs=[pl.BlockSpec((tm, d_model), row), pl.BlockSpec((tm, LANES), row)]
        + [window(g) for g in range(N_GROUPS)]
        + [pl.BlockSpec((None, 1, d_model), par3), pl.BlockSpec((None, 1, d_model), par3)],
        out_specs=pl.BlockSpec((tm, d_model), row),
    )
    return pl.pallas_call(
        functools.partial(_combine_kernel, alpha),
        grid_spec=grid_spec,
        out_shape=jax.ShapeDtypeStruct((m, d_model), F32),
        compiler_params=_cparams(("arbitrary",)),
        name="combine",
    )(ts, back, x1, info, *([ys] * N_GROUPS), ln_g, ln_b)


def _tiles_per_group(ends):
    return jnp.maximum((ends + EXP_TM - 1) // EXP_TM, 1)


def _expert_tiles(ends, t_max):
    per_group = _tiles_per_group(ends)
    cum = jnp.cumsum(per_group)
    total = cum[-1:]
    t = jnp.minimum(jnp.arange(t_max, dtype=jnp.int32), total[0] - 1)
    tile_g = jnp.sum((t[:, None] >= cum[None, :]).astype(jnp.int32), axis=1)
    tile_blk = t - (cum - per_group)[tile_g]
    valid = jnp.clip(ends[tile_g] - tile_blk * EXP_TM, 0, EXP_TM)
    tile_nsub = (valid + EXP_SUB - 1) // EXP_SUB
    return tile_g, tile_blk, tile_nsub.astype(jnp.int32), total.astype(jnp.int32)


def kernel(x_prompt, x_sample, cache_k, cache_v, state_conv, state_sconv, page_table, w_in, conv_w, conv_b, conv_ln_g, conv_ln_b, sconv_w, sb_bias, w_out, ln1_g, ln1_b, router_group_w, router_group_b, router_expert_w, router_expert_b, expert_w_gate, expert_w_up, expert_w_down, ln2_g, ln2_b):
    batch, seq, d_model = x_prompt.shape
    nb = x_sample.shape[0]
    depth = w_in.shape[0]
    d_conf = conv_w.shape[-1]
    d_sc = sconv_w.shape[-1]
    heads = sb_bias.shape[-1]
    d_sb = heads * SB_HEAD_DIM
    alpha = (2 * depth) ** 0.25

    row3 = lambda a: a.reshape(depth, 1, a.shape[-1])
    conv_b3, conv_ln_g3, conv_ln_b3 = row3(conv_b), row3(conv_ln_g), row3(conv_ln_b)
    ln1_g3, ln1_b3, ln2_g3, ln2_b3 = row3(ln1_g), row3(ln1_b), row3(ln2_g), row3(ln2_b)

    n_log = N_GROUPS + N_EXPERTS
    wr = jnp.concatenate(
        [router_group_w, router_expert_w.transpose(0, 2, 1, 3).reshape(depth, d_model, N_EXPERTS)], axis=-1)
    wr = jnp.pad(wr, ((0, 0), (0, 0), (0, LANES - n_log)))
    wr_hi = wr.astype(BF16)
    wr_parts = jnp.concatenate([wr_hi, (wr - wr_hi.astype(F32)).astype(BF16)], axis=-1)
    br = jnp.concatenate([router_group_b, router_expert_b.reshape(depth, N_EXPERTS)], axis=-1)
    br = jnp.pad(br, ((0, 0), (0, LANES - n_log))).reshape(depth, 1, LANES)

    w_in_b = w_in.astype(BF16)
    w_out_b = w_out.astype(BF16)
    experts = (expert_w_gate, expert_w_up, expert_w_down)
    cap, t_max = _moe_layout((batch * seq, nb))
    rows = jnp.zeros((N_GROUPS * cap, d_model + LANES), BF16)
    no_rows = jnp.zeros((N_GROUPS,), jnp.int32)

    cs_t = state_conv.transpose(0, 2, 1, 3)
    ss_t = state_sconv.transpose(0, 2, 1, 3)

    yp = x_prompt.reshape(batch * seq, d_model)
    ys = x_sample.reshape(nb, d_model)
    kp, vp, cp, sp, kd, vd, cd, sd = [], [], [], [], [], [], [], []
    for l in range(depth):
        router = (w_out_b, l, alpha, ln1_g3, ln1_b3, wr_parts, br)
        cg, q, k, v, gch = _inproj(yp, w_in_b, l, d_conf, d_sb, d_sc)
        conf, sc, cst, sst = _mixer_prompt(cg, gch, batch, seq, l, conv_w, conv_b3, conv_ln_g3, conv_ln_b3,
                                           sconv_w)
        att = _attn_prompt(q, k, v, sb_bias[l] * LOG2_E, batch, seq)
        x1p, info_p, tstart_p, rows = _dispatch(conf, att, sc, yp, *router, rows, no_rows, cap)
        kp.append(k.reshape(batch, seq, heads, SB_HEAD_DIM))
        vp.append(v.reshape(batch, seq, heads, SB_HEAD_DIM))
        cp.append(cst[:, CONV_HIST - (CONF_WIDTH - 1):, :])
        sp.append(sst[:, SC_HIST - (SC_WIDTH - 1):, :])

        cg, q, k, v, gch = _inproj(ys, w_in_b, l, d_conf, d_sb, d_sc)
        conf, sc, ncs, nss = _mixer_sample(cg, gch, cs_t, ss_t, l, conv_w, conv_b3, conv_ln_g3, conv_ln_b3,
                                           sconv_w)
        att = _attn_sample(q, sb_bias[l] * LOG2_E, cache_k, cache_v, page_table, l)
        x1s, info_s, tstart_s, rows = _dispatch(conf, att, sc, ys, *router, rows, tstart_p[-1], cap)
        kd.append(k.reshape(nb, 1, heads, SB_HEAD_DIM))
        vd.append(v.reshape(nb, 1, heads, SB_HEAD_DIM))
        cd.append(ncs.transpose(1, 0, 2))
        sd.append(nss.transpose(1, 0, 2))

        ends = tstart_s[-1]
        rows = _experts(rows, *_expert_tiles(ends, t_max), *experts, l, cap)
        yp = _combine(x1p, info_p, tstart_p, ends, rows, l, alpha, ln2_g3, ln2_b3, cap)
        ys = _combine(x1s, info_s, tstart_s, ends, rows, l, alpha, ln2_g3, ln2_b3, cap)

    return (yp.reshape(batch, seq, d_model), ys.reshape(nb, 1, d_model),
            jnp.stack(kp), jnp.stack(vp), jnp.stack(cp), jnp.stack(sp),
            jnp.stack(kd), jnp.stack(vd), jnp.stack(cd), jnp.stack(sd))
```

```python
import functools
import math

import jax
import jax.numpy as jnp
from jax import lax
from jax.experimental import pallas as pl
from jax.experimental.pallas import tpu as pltpu

F32 = jnp.float32
BF16 = jnp.bfloat16

SB_HEAD_DIM = 128
CONF_WIDTH = 31
SC_WIDTH = 3
N_GROUPS = 4
EXPERTS_PER_GROUP = 4
N_EXPERTS = N_GROUPS * EXPERTS_PER_GROUP
LN_EPS = 1e-5
LOG2_E = math.log2(math.e)

LANES = 128
SUBLANES = 8
VMEM_LIMIT_MB = 56

INPROJ_TM = 1024
INPROJ_TN = 512
MIX_T = 512
MIX_RC = 32
CONV_HIST = 32
SC_HIST = 8
ATT_TQ = 1024
ATT_TK = 256
ATT_UNROLL = 4
PAGES_PER_STEP = 16
DISP_TM = 256
ROW_ALIGN = 16
EXP_TM = 1280
EXP_SUB = 256
EXP_HC = 256


def _cparams(sem):
    return pltpu.CompilerParams(dimension_semantics=sem,
                                vmem_limit_bytes=VMEM_LIMIT_MB * 1024 * 1024)


def _layer_norm(y, g, b):
    mu = jnp.mean(y, axis=-1, keepdims=True)
    d = y - mu
    var = jnp.mean(d * d, axis=-1, keepdims=True)
    return d * lax.rsqrt(var + LN_EPS) * g + b


def _softplus2(z):
    return jnp.maximum(z, 0.0) + jnp.log2(1.0 + jnp.exp2(-jnp.abs(z)))


def _split_bf16(x):
    hi = x.astype(BF16)
    lo = (x - hi.astype(F32)).astype(BF16)
    return hi, lo


def _inproj_kernel(bounds, scale, x_ref, w_ref, cg_ref, q_ref, k_ref, v_ref, gch_ref, xb_ref):
    j = pl.program_id(1)

    @pl.when(j == 0)
    def _():
        xb_ref[...] = x_ref[...].astype(BF16)

    acc = jnp.dot(xb_ref[...], w_ref[...], preferred_element_type=F32)
    b_cg, b_q, b_k, b_v = bounds

    @pl.when(j < b_cg)
    def _():
        cg_ref[...] = acc

    @pl.when((j >= b_cg) & (j < b_q))
    def _():
        q_ref[...] = (acc * scale).astype(BF16)

    @pl.when((j >= b_q) & (j < b_k))
    def _():
        k_ref[...] = acc

    @pl.when((j >= b_k) & (j < b_v))
    def _():
        v_ref[...] = acc

    @pl.when(j >= b_v)
    def _():
        gch_ref[...] = acc


def _inproj(x, w_in, layer, d_conf, d_sb, d_sc):
    m, d_model = x.shape
    p_in = w_in.shape[-1]
    tm = min(INPROJ_TM, m)
    tn = INPROJ_TN
    n_cg, n_sb, n_gch = 2 * d_conf // tn, d_sb // tn, 3 * d_sc // tn
    bounds = (n_cg, n_cg + n_sb, n_cg + 2 * n_sb, n_cg + 3 * n_sb)
    assert bounds[-1] + n_gch == p_in // tn and m % tm == 0

    def out_map(lo, n):
        return lambda i, j: (i, jnp.clip(j - lo, 0, n - 1))

    kernel = functools.partial(_inproj_kernel, bounds, SB_HEAD_DIM ** -0.5 * LOG2_E)
    return pl.pallas_call(
        kernel,
        grid=(m // tm, p_in // tn),
        in_specs=[
            pl.BlockSpec((tm, d_model), lambda i, j: (i, 0)),
            pl.BlockSpec((None, d_model, tn), lambda i, j: (layer, 0, j)),
        ],
        out_specs=[
            pl.BlockSpec((tm, tn), out_map(0, n_cg)),
            pl.BlockSpec((tm, tn), out_map(bounds[0], n_sb)),
            pl.BlockSpec((tm, tn), out_map(bounds[1], n_sb)),
            pl.BlockSpec((tm, tn), out_map(bounds[2], n_sb)),
            pl.BlockSpec((tm, tn), out_map(bounds[3], n_gch)),
        ],
        out_shape=[
            jax.ShapeDtypeStruct((m, 2 * d_conf), F32),
            jax.ShapeDtypeStruct((m, d_sb), BF16),
            jax.ShapeDtypeStruct((m, d_sb), F32),
            jax.ShapeDtypeStruct((m, d_sb), F32),
            jax.ShapeDtypeStruct((m, 3 * d_sc), F32),
        ],
        scratch_shapes=[pltpu.VMEM((tm, d_model), BF16)],
        compiler_params=_cparams(("arbitrary", "arbitrary")),
        name="inproj",
    )(x, w_in)


def _mixer_prompt_kernel(val_ref, gate_ref, gb_ref, gc_ref, h_ref, cw_ref, cb_ref, lg_ref, lb_ref, sw_ref,
                         conf_ref, sc_ref, cst_ref, sst_ref, ext_ref, ext2_ref, shift_ref):
    t = pl.program_id(1)
    tt, c = val_ref.shape

    @pl.when(t == 0)
    def _():
        ext_ref[0:CONV_HIST, :] = jnp.zeros((CONV_HIST, c), F32)
        ext2_ref[0:SC_HIST, :] = jnp.zeros((SC_HIST, ext2_ref.shape[1]), F32)

    ext_ref[CONV_HIST:CONV_HIST + tt, :] = val_ref[...] * jax.nn.sigmoid(gate_ref[...])
    ext2_ref[SC_HIST:SC_HIST + tt, :] = gc_ref[...] * h_ref[...]

    shifted_rows = shift_ref.shape[1]
    for b in range(1, SUBLANES):
        shift_ref[b - 1] = ext_ref[b:b + shifted_rows, :]

    def tap_rows(start, rows):
        b, base = start % SUBLANES, start - start % SUBLANES
        if b == 0:
            return ext_ref[base:base + rows, :]
        return shift_ref[b - 1, base:base + rows, :]

    cb = cb_ref[...]
    lg = lg_ref[...]
    lb = lb_ref[...]
    conv_off = CONV_HIST - (CONF_WIDTH - 1)
    sc_off = SC_HIST - (SC_WIDTH - 1)
    for ch in range(tt // MIX_RC):
        r0 = ch * MIX_RC
        acc = jnp.zeros((MIX_RC, c), F32)
        for w in range(CONF_WIDTH):
            acc = acc + tap_rows(r0 + conv_off + w, MIX_RC) * cw_ref[w:w + 1, :]
        yn = _layer_norm(acc + cb, lg, lb)
        conf_ref[r0:r0 + MIX_RC, :] = (yn * jax.nn.sigmoid(yn)).astype(conf_ref.dtype)
        s = jnp.zeros((MIX_RC, ext2_ref.shape[1]), F32)
        for w in range(SC_WIDTH):
            s = s + ext2_ref[r0 + sc_off + w:r0 + sc_off + w + MIX_RC, :] * sw_ref[w:w + 1, :]
        sc_ref[r0:r0 + MIX_RC, :] = (gb_ref[r0:r0 + MIX_RC, :] * s).astype(sc_ref.dtype)

    ext_ref[0:CONV_HIST, :] = ext_ref[tt:tt + CONV_HIST, :]
    ext2_ref[0:SC_HIST, :] = ext2_ref[tt:tt + SC_HIST, :]

    @pl.when(t == pl.num_programs(1) - 1)
    def _():
        cst_ref[...] = ext_ref[tt:tt + CONV_HIST, :]
        sst_ref[...] = ext2_ref[tt:tt + SC_HIST, :]


def _mixer_prompt(cg, gch, batch, seq, layer, conv_w, conv_b, conv_ln_g, conv_ln_b, sconv_w):
    d_conf = cg.shape[1] // 2
    d_sc = gch.shape[1] // 3
    tt = MIX_T
    nt = seq // tt
    row = lambda b, t: (b * nt + t, 0)
    col = lambda cidx: (lambda b, t: (b * nt + t, cidx))
    par3 = lambda b, t: (layer, 0, 0)
    return pl.pallas_call(
        _mixer_prompt_kernel,
        grid=(batch, nt),
        in_specs=[
            pl.BlockSpec((tt, d_conf), col(0)),
            pl.BlockSpec((tt, d_conf), col(1)),
            pl.BlockSpec((tt, d_sc), col(0)),
            pl.BlockSpec((tt, d_sc), col(1)),
            pl.BlockSpec((tt, d_sc), col(2)),
            pl.BlockSpec((None, CONF_WIDTH, d_conf), par3),
            pl.BlockSpec((None, 1, d_conf), par3),
            pl.BlockSpec((None, 1, d_conf), par3),
            pl.BlockSpec((None, 1, d_conf), par3),
            pl.BlockSpec((None, SC_WIDTH, d_sc), par3),
        ],
        out_specs=[
            pl.BlockSpec((tt, d_conf), row),
            pl.BlockSpec((tt, d_sc), row),
            pl.BlockSpec((None, CONV_HIST, d_conf), lambda b, t: (b, 0, 0)),
            pl.BlockSpec((None, SC_HIST, d_sc), lambda b, t: (b, 0, 0)),
        ],
        out_shape=[
            jax.ShapeDtypeStruct((batch * seq, d_conf), BF16),
            jax.ShapeDtypeStruct((batch * seq, d_sc), BF16),
            jax.ShapeDtypeStruct((batch, CONV_HIST, d_conf), F32),
            jax.ShapeDtypeStruct((batch, SC_HIST, d_sc), F32),
        ],
        scratch_shapes=[pltpu.VMEM((tt + CONV_HIST, d_conf), F32),
                        pltpu.VMEM((tt + SC_HIST, d_sc), F32),
                        pltpu.VMEM((SUBLANES - 1, tt + CONV_HIST - SUBLANES, d_conf), F32)],
        compiler_params=_cparams(("arbitrary", "arbitrary")),
        name="mixer_prompt",
    )(cg, cg, gch, gch, gch, conv_w, conv_b, conv_ln_g, conv_ln_b, sconv_w)


def _mixer_sample_kernel(cg_ref, gch_ref, cs_ref, ss_ref, cw_ref, cb_ref, lg_ref, lb_ref, sw_ref,
                         conf_ref, sc_ref, ncs_ref, nss_ref):
    c = cs_ref.shape[-1]
    c2 = ss_ref.shape[-1]
    hist = CONF_WIDTH - 1
    glu = cg_ref[:, 0:c] * jax.nn.sigmoid(cg_ref[:, c:2 * c])
    acc = glu * cw_ref[hist:hist + 1, :]
    for w in range(hist):
        acc = acc + cs_ref[w] * cw_ref[w:w + 1, :]
    yn = _layer_norm(acc + cb_ref[...], lg_ref[...], lb_ref[...])
    conf_ref[...] = (yn * jax.nn.sigmoid(yn)).astype(conf_ref.dtype)

    p = gch_ref[:, c2:2 * c2] * gch_ref[:, 2 * c2:3 * c2]
    s = p * sw_ref[SC_WIDTH - 1:SC_WIDTH, :]
    for w in range(SC_WIDTH - 1):
        s = s + ss_ref[w] * sw_ref[w:w + 1, :]
    sc_ref[...] = (gch_ref[:, 0:c2] * s).astype(sc_ref.dtype)

    for w in range(hist - 1):
        ncs_ref[w] = cs_ref[w + 1]
    ncs_ref[hist - 1] = glu
    for w in range(SC_WIDTH - 2):
        nss_ref[w] = ss_ref[w + 1]
    nss_ref[SC_WIDTH - 2] = p


def _mixer_sample(cg, gch, cs_t, ss_t, layer, conv_w, conv_b, conv_ln_g, conv_ln_b, sconv_w):
    nb = cg.shape[0]
    d_conf = cg.shape[1] // 2
    d_sc = gch.shape[1] // 3
    full2 = lambda i: (0, 0)
    par3 = lambda i: (layer, 0, 0)
    st4 = lambda i: (layer, 0, 0, 0)
    return pl.pallas_call(
        _mixer_sample_kernel,
        grid=(1,),
        in_specs=[
            pl.BlockSpec(cg.shape, full2),
            pl.BlockSpec(gch.shape, full2),
            pl.BlockSpec((None,) + cs_t.shape[1:], st4),
            pl.BlockSpec((None,) + ss_t.shape[1:], st4),
            pl.BlockSpec((None, CONF_WIDTH, d_conf), par3),
            pl.BlockSpec((None, 1, d_conf), par3),
            pl.BlockSpec((None, 1, d_conf), par3),
            pl.BlockSpec((None, 1, d_conf), par3),
            pl.BlockSpec((None, SC_WIDTH, d_sc), par3),
        ],
        out_specs=[
            pl.BlockSpec((nb, d_conf), full2),
            pl.BlockSpec((nb, d_sc), full2),
            pl.BlockSpec(cs_t.shape[1:], lambda i: (0, 0, 0)),
            pl.BlockSpec(ss_t.shape[1:], lambda i: (0, 0, 0)),
        ],
        out_shape=[
            jax.ShapeDtypeStruct((nb, d_conf), BF16),
            jax.ShapeDtypeStruct((nb, d_sc), BF16),
            jax.ShapeDtypeStruct(cs_t.shape[1:], F32),
            jax.ShapeDtypeStruct(ss_t.shape[1:], F32),
        ],
        compiler_params=_cparams(("arbitrary",)),
        name="mixer_sample",
    )(cg, gch, cs_t, ss_t, conv_w, conv_b, conv_ln_g, conv_ln_b, sconv_w)


def _attn_prompt_kernel(bias_ref, q_ref, k_ref, v_ref, tri_ref, o_ref, kb_ref, vt_ref):
    h = pl.program_id(1)
    i = pl.program_id(2)
    tq = q_ref.shape[0]
    tk = tri_ref.shape[0]
    ratio = tq // tk

    @pl.when(i == 0)
    def _():
        kb_ref[...] = k_ref[...].astype(BF16)
        for kb in range(vt_ref.shape[0]):
            vt_ref[kb] = v_ref[kb * tk:(kb + 1) * tk, :].T.astype(BF16)

    bias = bias_ref[h]
    q = q_ref[...]
    tri2 = tri_ref[...]

    def blocks(kbs, q0s, acc, run, masked):
        zs, splits, csums, avs = [], [], [], []

        def causal(x):
            sq = jnp.where(lax.broadcasted_iota(jnp.int32, (tk, tk), 0)
                           < lax.broadcasted_iota(jnp.int32, (tk, tk), 1), x[:, :tk], 0.0)
            return sq if x.shape[1] == tk else jnp.concatenate([sq, x[:, tk:]], axis=1)

        for kb, q0 in zip(kbs, q0s):
            start = pl.multiple_of(kb * tk, tk)
            kblk = kb_ref[pl.ds(start, tk), :]
            zs.append(lax.dot_general(kblk, q[q0:, :], (((1,), (1,)), ((), ())),
                                      preferred_element_type=F32) + bias)
        for z in zs:
            sp = _softplus2(z)
            if masked:
                sp = causal(sp)
            hi, lo = _split_bf16(sp)
            splits.append(jnp.concatenate([hi, lo], axis=0))
        for w in splits:
            csums.append(jnp.dot(tri2, w, preferred_element_type=F32))
        def from_q0(full, q0, tail):
            return tail if q0 == 0 else jnp.concatenate([full[:, :q0], tail], axis=1)

        for j, q0 in enumerate(q0s):
            csum = csums[j] + run[:, q0:]
            run = from_q0(run, q0, csum[0:1, :])
            a = jnp.exp2(zs[j] - csum)
            if masked:
                a = causal(a)
            avs.append(jnp.dot(vt_ref[kbs[j]], a.astype(BF16), preferred_element_type=F32))
        for q0, av in zip(q0s, avs):
            acc = from_q0(acc, q0, acc[:, q0:] + av)
        return acc, run

    acc = jnp.zeros((q_ref.shape[1], tq), F32)
    run = jnp.zeros((1, tq), F32)
    diag = list(reversed(range(ratio)))
    acc, run = blocks([i * ratio + c for c in diag], [c * tk for c in diag], acc, run, True)

    def body(jj, carry):
        first = i * ratio - 1 - jj * ATT_UNROLL
        return blocks([first - d for d in range(ATT_UNROLL)], [0] * ATT_UNROLL, carry[0], carry[1], False)

    acc, run = lax.fori_loop(0, i * ratio // ATT_UNROLL, body, (acc, run))
    o_ref[...] = acc.T.astype(o_ref.dtype)


def _attn_prompt(q, k, v, bias, batch, seq):
    d_sb = q.shape[1]
    heads = d_sb // SB_HEAD_DIM
    tq, tk = ATT_TQ, ATT_TK
    nq = seq // tq
    assert seq % tq == 0 and (tq // tk) % ATT_UNROLL == 0
    r = lax.broadcasted_iota(jnp.int32, (tk, tk), 0)
    c = lax.broadcasted_iota(jnp.int32, (tk, tk), 1)
    upper = (c >= r).astype(BF16)
    tri = jnp.concatenate([upper, upper], axis=1)
    grid_spec = pltpu.PrefetchScalarGridSpec(
        num_scalar_prefetch=0,
        grid=(batch, heads, nq),
        in_specs=[
            pl.BlockSpec(memory_space=pltpu.SMEM),
            pl.BlockSpec((tq, SB_HEAD_DIM), lambda b, h, i: (b * nq + i, h)),
            pl.BlockSpec((seq, SB_HEAD_DIM), lambda b, h, i: (b, h)),
            pl.BlockSpec((seq, SB_HEAD_DIM), lambda b, h, i: (b, h)),
            pl.BlockSpec((tk, 2 * tk), lambda b, h, i: (0, 0)),
        ],
        out_specs=pl.BlockSpec((tq, SB_HEAD_DIM), lambda b, h, i: (b * nq + i, h)),
        scratch_shapes=[pltpu.VMEM((seq, SB_HEAD_DIM), BF16),
                        pltpu.VMEM((seq // tk, SB_HEAD_DIM, tk), BF16)],
    )
    return pl.pallas_call(
        _attn_prompt_kernel,
        grid_spec=grid_spec,
        out_shape=jax.ShapeDtypeStruct((batch * seq, d_sb), BF16),
        compiler_params=_cparams(("arbitrary", "arbitrary", "arbitrary")),
        name="attn_prompt",
    )(bias, q, k, v, tri)


def _attn_sample_kernel(layer, n_pages, pt_ref, qbd_ref, bias_ref, tri_ref, ck_ref, cv_ref, o_ref,
                        acc_ref, run_ref, kbuf, vbuf, sem):
    b = pl.program_id(0)
    s = pl.program_id(1)
    pps = kbuf.shape[1]
    ns = n_pages // pps
    step = b * ns + s
    slot = step % 2
    heads = acc_ref.shape[1] // kbuf.shape[3]
    page = kbuf.shape[2] // heads

    def page_copies(at_step, sl):
        bb = at_step // ns
        first = bb * n_pages + (n_pages - 1) - (at_step - bb * ns) * pps
        copies = []
        for r in range(pps):
            pg = pt_ref[first - r]
            copies.append(pltpu.make_async_copy(ck_ref.at[layer, pg], kbuf.at[sl, r], sem.at[sl, 0]))
            copies.append(pltpu.make_async_copy(cv_ref.at[layer, pg], vbuf.at[sl, r], sem.at[sl, 1]))
        return copies

    @pl.when(step == 0)
    def _():
        for n, cp in enumerate(page_copies(step, slot)):
            cp.start(priority=n % 2)

    @pl.when(step + 1 < pl.num_programs(0) * ns)
    def _():
        for n, cp in enumerate(page_copies(step + 1, 1 - slot)):
            cp.start(priority=n % 2)

    for cp in page_copies(step, slot):
        cp.wait()

    @pl.when(s == 0)
    def _():
        acc_ref[...] = jnp.zeros(acc_ref.shape, F32)
        run_ref[...] = jnp.zeros(run_ref.shape, F32)

    def flat_page(buf, r):
        return jnp.concatenate([buf[slot, r, pl.ds(h, page, stride=heads), :] for h in range(heads)],
                               axis=1).astype(BF16)

    vc = jnp.concatenate([flat_page(vbuf, r) for r in range(pps)], axis=0)
    half = pps // 2
    z_halves = [jnp.dot(jnp.concatenate([flat_page(kbuf, r) for r in range(lo, lo + half)], axis=0),
                        qbd_ref[...], preferred_element_type=F32) + bias_ref[...] for lo in (0, half)]
    tri = tri_ref[...]
    run = run_ref[...]
    a_parts = []
    for r in range(pps):
        z = z_halves[r // half][(r % half) * page:(r % half + 1) * page, :]
        hi, lo = _split_bf16(_softplus2(z))
        csum = (jnp.dot(tri, hi, preferred_element_type=F32)
                + jnp.dot(tri, lo, preferred_element_type=F32)) + run
        a_parts.append(jnp.exp2(z - csum).astype(BF16))
        run = csum[0:1, :]
    run_ref[...] = run
    a_all = jnp.concatenate(a_parts, axis=0)
    acc_ref[...] += lax.dot_general(a_all, vc, (((0,), (0,)), ((), ())), preferred_element_type=F32)

    @pl.when(s == pl.num_programs(1) - 1)
    def _():
        top = acc_ref[0:heads, :]
        row = lax.broadcasted_iota(jnp.int32, top.shape, 0)
        col = lax.broadcasted_iota(jnp.int32, top.shape, 1)
        lo_c = row * SB_HEAD_DIM
        keep = (col >= lo_c) & (col < lo_c + SB_HEAD_DIM)
        o_ref[...] = jnp.sum(jnp.where(keep, top, 0.0), axis=0, keepdims=True).astype(o_ref.dtype)


def _attn_sample(q, bias, cache_k, cache_v, page_table, layer):
    nb, d_sb = q.shape
    heads = d_sb // SB_HEAD_DIM
    depth, n_pool, page = cache_k.shape[:3]
    n_pages = page_table.shape[1]
    pps = PAGES_PER_STEP
    assert n_pages % pps == 0 and heads == SUBLANES
    pt = page_table.reshape(-1)
    ck = cache_k.reshape(depth, n_pool, page * heads, SB_HEAD_DIM)
    cv = cache_v.reshape(depth, n_pool, page * heads, SB_HEAD_DIM)
    r = lax.broadcasted_iota(jnp.int32, (d_sb, LANES), 0) // SB_HEAD_DIM
    c = lax.broadcasted_iota(jnp.int32, (d_sb, LANES), 1)
    qbd = jnp.where((r == c)[None], q[:, :, None], jnp.zeros((), q.dtype))
    bias_row = jnp.zeros((1, LANES), F32).at[0, :heads].set(bias)
    rr = lax.broadcasted_iota(jnp.int32, (page, page), 0)
    cc = lax.broadcasted_iota(jnp.int32, (page, page), 1)
    tri = (cc >= rr).astype(BF16)

    grid_spec = pltpu.PrefetchScalarGridSpec(
        num_scalar_prefetch=1,
        grid=(nb, n_pages // pps),
        in_specs=[
            pl.BlockSpec((None, d_sb, LANES), lambda b, s, pt_ref: (b, 0, 0)),
            pl.BlockSpec((1, LANES), lambda b, s, pt_ref: (0, 0)),
            pl.BlockSpec((page, page), lambda b, s, pt_ref: (0, 0)),
            pl.BlockSpec(memory_space=pl.ANY),
            pl.BlockSpec(memory_space=pl.ANY),
        ],
        out_specs=pl.BlockSpec((None, 1, d_sb), lambda b, s, pt_ref: (b, 0, 0)),
        scratch_shapes=[
            pltpu.VMEM((LANES, d_sb), F32),
            pltpu.VMEM((1, LANES), F32),
            pltpu.VMEM((2, pps, page * heads, SB_HEAD_DIM), cache_k.dtype),
            pltpu.VMEM((2, pps, page * heads, SB_HEAD_DIM), cache_v.dtype),
            pltpu.SemaphoreType.DMA((2, 2)),
        ],
    )
    out = pl.pallas_call(
        functools.partial(_attn_sample_kernel, layer, n_pages),
        grid_spec=grid_spec,
        out_shape=jax.ShapeDtypeStruct((nb, 1, d_sb), BF16),
        compiler_params=_cparams(("arbitrary", "arbitrary")),
        name="attn_sample",
    )(pt, qbd, bias_row, tri, ck, cv)
    return out.reshape(nb, d_sb)


def _route(logits):
    lane = lax.broadcasted_iota(jnp.int32, logits.shape, 1).astype(F32)
    neg = -jnp.inf
    far = float(LANES)
    is_group = lane < N_GROUPS
    lg = jnp.where(is_group, logits, neg)
    gmax = jnp.max(lg, axis=1, keepdims=True)
    gidx = jnp.min(jnp.where(lg == gmax, lane, far), axis=1, keepdims=True)
    den = jnp.sum(jnp.where(is_group, jnp.exp(logits - gmax), 0.0), axis=1, keepdims=True)
    ggate = 1.0 / den
    first = N_GROUPS + gidx * EXPERTS_PER_GROUP
    le = jnp.where((lane >= first) & (lane < first + EXPERTS_PER_GROUP), logits, neg)
    v1 = jnp.max(le, axis=1, keepdims=True)
    i1 = jnp.min(jnp.where(le == v1, lane, far), axis=1, keepdims=True)
    le2 = jnp.where(lane == i1, neg, le)
    v2 = jnp.max(le2, axis=1, keepdims=True)
    i2 = jnp.min(jnp.where(le2 == v2, lane, far), axis=1, keepdims=True)
    e21 = jnp.exp(v2 - v1)
    p1 = 1.0 / (1.0 + e21)
    p2 = e21 * p1
    return lane, gidx, (i1 - N_GROUPS, ggate * p1), (i2 - N_GROUPS, ggate * p2)


def _router_logits(x1, wr_ref, br_ref):
    hi, lo = _split_bf16(x1)
    both = jnp.dot(hi, wr_ref[...], preferred_element_type=F32)
    logits = (both[:, :LANES] + both[:, LANES:]
              + jnp.dot(lo, wr_ref[:, :LANES], preferred_element_type=F32)) + br_ref[...]
    return hi, logits


def _disp_tile(n_tokens):
    return min(DISP_TM, n_tokens)


def _moe_layout(token_sets):
    tiles = sum(n // _disp_tile(n) for n in token_sets)
    max_rows = sum(token_sets) + tiles * (ROW_ALIGN - 1)
    cap = pl.cdiv(max_rows + DISP_TM, EXP_TM) * EXP_TM
    t_max = pl.cdiv(max_rows, EXP_TM) + N_GROUPS
    return cap, t_max


def _sort_rows(tm):
    return pl.cdiv(tm + N_GROUPS * (ROW_ALIGN - 1), LANES) * LANES


def _dispatch_kernel(alpha, cap, conf_ref, att_ref, sc_ref, x_ref, w_ref, g_ref, b_ref, wr_ref,
                     br_ref, ltri_ref, cnt0_ref, rows_in_ref, x1_ref, info_ref, tstart_ref, xs_ref,
                     sbuf, cnt_ref, sem):
    del rows_in_ref
    i = pl.program_id(0)
    nt = pl.num_programs(0)
    slot = i % 2
    tm = x_ref.shape[0]
    sort_rows = sbuf.shape[1] - tm

    @pl.when(i == 0)
    def _():
        for g in range(N_GROUPS):
            cnt_ref[g] = cnt0_ref[g]
        sbuf[...] = jnp.zeros(sbuf.shape, BF16)

    cat = jnp.concatenate([conf_ref[...], att_ref[...], sc_ref[...]], axis=1)
    mix = jnp.dot(cat, w_ref[...], preferred_element_type=F32)
    x1 = _layer_norm(alpha * x_ref[...] + mix, g_ref[...], b_ref[...])
    x1_ref[...] = x1
    hi, logits = _router_logits(x1, wr_ref, br_ref)
    lane, gidx, (e1, w1), (e2, w2) = _route(logits)

    first = gidx * EXPERTS_PER_GROUP
    l1, l2 = e1 - first, e2 - first
    w1h = w1.astype(BF16).astype(F32)
    w2h = w2.astype(BF16).astype(F32)
    gates = jnp.where(lane == l1, w1h, jnp.where(lane == l2, w2h, 0.0))
    gates = jnp.where(lane == l1 + EXPERTS_PER_GROUP, w1 - w1h,
                      jnp.where(lane == l2 + EXPERTS_PER_GROUP, w2 - w2h, gates))
    xrow = jnp.concatenate([hi, gates.astype(BF16)], axis=1)

    onehot = jnp.where(lane == gidx, 1.0, 0.0)
    before = jnp.dot(ltri_ref[...], onehot.astype(BF16), preferred_element_type=F32)
    rank = jnp.sum(jnp.where(lane == gidx, before, 0.0), axis=1, keepdims=True)
    lens = (before[tm - 1:tm, :] + onehot[tm - 1:tm, :]).astype(jnp.int32)
    info_ref[...] = jnp.where(lane == 0.0, gidx, jnp.where(lane == 1.0, rank, 0.0))

    lbase = jnp.int32(0)
    lbases, pads = [], []
    dest = rank
    for g in range(N_GROUPS):
        pad = ((lens[0, g] + (ROW_ALIGN - 1)) // ROW_ALIGN) * ROW_ALIGN
        lbases.append(lbase)
        pads.append(pad)
        dest = dest + jnp.where(gidx == float(g), lbase.astype(F32), 0.0)
        lbase = lbase + pad
    col = lax.broadcasted_iota(jnp.int32, (tm, sort_rows), 1).astype(F32)
    onehot_dest = jnp.where(col == dest, 1.0, 0.0).astype(BF16)
    srt = lax.dot_general(onehot_dest, xrow, (((0,), (0,)), ((), ())), preferred_element_type=F32)
    sbuf[slot, 0:sort_rows, :] = srt.astype(BF16)

    def copy(sl, g, src, dst):
        return pltpu.make_async_copy(
            sbuf.at[sl, pl.ds(pl.multiple_of(src, ROW_ALIGN), tm)],
            xs_ref.at[pl.ds(pl.multiple_of(dst, ROW_ALIGN), tm)], sem.at[sl, g])

    @pl.when(i > 0)
    def _():
        for g in range(N_GROUPS):
            copy(1 - slot, g, 0, 0).wait()

    for g in range(N_GROUPS):
        start = cnt_ref[g]
        copy(slot, g, lbases[g], g * cap + start).start()
        tstart_ref[i, g] = start
        cnt_ref[g] = start + pads[g]

    @pl.when(i == nt - 1)
    def _():
        for g in range(N_GROUPS):
            copy(slot, g, 0, 0).wait()
            tstart_ref[nt, g] = cnt_ref[g]


def _dispatch(conf, att, sc, x, w_out_b, layer, alpha, ln_g, ln_b, wr, br, rows_buf, cnt0, cap):
    m, d_model = x.shape
    tm = _disp_tile(m)
    nt = m // tm
    assert m % tm == 0 and tm % ROW_ALIGN == 0
    d_mix = w_out_b.shape[1]
    width = d_model + LANES
    rr = lax.broadcasted_iota(jnp.int32, (tm, tm), 0)
    cc = lax.broadcasted_iota(jnp.int32, (tm, tm), 1)
    ltri = (cc < rr).astype(BF16)
    row = lambda i: (i, 0)
    par3 = lambda i: (layer, 0, 0)
    return pl.pallas_call(
        functools.partial(_dispatch_kernel, alpha, cap),
        grid=(nt,),
        in_specs=[
            pl.BlockSpec((tm, conf.shape[1]), row),
            pl.BlockSpec((tm, att.shape[1]), row),
            pl.BlockSpec((tm, sc.shape[1]), row),
            pl.BlockSpec((tm, d_model), row),
            pl.BlockSpec((None, d_mix, d_model), par3),
            pl.BlockSpec((None, 1, d_model), par3),
            pl.BlockSpec((None, 1, d_model), par3),
            pl.BlockSpec((None, d_model, 2 * LANES), par3),
            pl.BlockSpec((None, 1, LANES), par3),
            pl.BlockSpec((tm, tm), lambda i: (0, 0)),
            pl.BlockSpec(memory_space=pltpu.SMEM),
            pl.BlockSpec(memory_space=pl.ANY),
        ],
        out_specs=[
            pl.BlockSpec((tm, d_model), row),
            pl.BlockSpec((tm, LANES), row),
            pl.BlockSpec(memory_space=pltpu.SMEM),
            pl.BlockSpec(memory_space=pl.ANY),
        ],
        input_output_aliases={11: 3},
        out_shape=[
            jax.ShapeDtypeStruct((m, d_model), F32),
            jax.ShapeDtypeStruct((m, LANES), F32),
            jax.ShapeDtypeStruct((nt + 1, N_GROUPS), jnp.int32),
            jax.ShapeDtypeStruct(rows_buf.shape, rows_buf.dtype),
        ],
        scratch_shapes=[
            pltpu.VMEM((2, _sort_rows(tm) + tm, width), BF16),
            pltpu.SMEM((N_GROUPS,), jnp.int32),
            pltpu.SemaphoreType.DMA((2, N_GROUPS)),
        ],
        compiler_params=_cparams(("arbitrary",)),
        name="dispatch",
    )(conf, att, sc, x, w_out_b, ln_g, ln_b, wr, br, ltri, cnt0, rows_buf)


def _expert_kernel(cpe, tg_ref, tb_ref, tn_ref, tot_ref, xs_ref, wg_ref, wu_ref, wd_ref, ys_ref,
                   acc_ref, wgb_ref, wub_ref, wdb_ref):
    t = pl.program_id(0)
    c = pl.program_id(1)
    nc = EXPERTS_PER_GROUP * cpe
    d_model = wg_ref.shape[0]

    @pl.when(t < tot_ref[0])
    def _():
        @pl.when(c == 0)
        def _():
            acc_ref[...] = jnp.zeros(acc_ref.shape, F32)

        wgb_ref[...] = wg_ref[...].astype(BF16)
        wub_ref[...] = wu_ref[...].astype(BF16)
        wdb_ref[...] = wd_ref[...].astype(BF16)
        e_local = c // cpe

        def sub(r0, rows):
            xb = xs_ref[pl.ds(r0, rows), 0:d_model]
            gates = xs_ref[pl.ds(r0, rows), d_model:d_model + LANES].astype(F32)
            lane = lax.broadcasted_iota(jnp.int32, gates.shape, 1)
            mine = (lane == e_local) | (lane == e_local + EXPERTS_PER_GROUP)
            ce = jnp.sum(jnp.where(mine, gates, 0.0), axis=1, keepdims=True)
            gate = jnp.dot(xb, wgb_ref[...], preferred_element_type=F32)
            up = jnp.dot(xb, wub_ref[...], preferred_element_type=F32)
            hid = (gate * jax.nn.sigmoid(gate)) * up * ce
            acc_ref[pl.ds(r0, rows), :] += jnp.dot(hid.astype(BF16), wdb_ref[...],
                                                   preferred_element_type=F32)

        nsub = tn_ref[t]

        def pair(s, carry):
            sub(pl.multiple_of(s * (2 * EXP_SUB), 2 * EXP_SUB), 2 * EXP_SUB)
            return carry

        lax.fori_loop(0, lax.shift_right_logical(nsub, 1), pair, 0)

        @pl.when((nsub & 1) == 1)
        def _():
            sub(pl.multiple_of((nsub - 1) * EXP_SUB, EXP_SUB), EXP_SUB)

        @pl.when(c == nc - 1)
        def _():
            ys_ref[:, 0:d_model] = acc_ref[...].astype(ys_ref.dtype)
            ys_ref[:, d_model:d_model + LANES] = xs_ref[:, d_model:d_model + LANES]


def _experts(xs, tile_g, tile_blk, tile_nsub, total, wg, wu, wd, layer, cap):
    width = xs.shape[1]
    d_model = width - LANES
    d_exp = wg.shape[-1]
    t_max = tile_g.shape[0]
    cpe = d_exp // EXP_HC
    nc = EXPERTS_PER_GROUP * cpe
    blocks_per_group = cap // EXP_TM

    def chunk(t, c, tot):
        return jnp.where(t < tot[0], c, nc - 1)

    def rows(t, c, tg, tb, tn, tot):
        return (tg[t] * blocks_per_group + tb[t], 0)

    def w_in_map(t, c, tg, tb, tn, tot):
        ce = chunk(t, c, tot)
        return (layer, tg[t] * EXPERTS_PER_GROUP + ce // cpe, 0, ce % cpe)

    def w_out_map(t, c, tg, tb, tn, tot):
        ce = chunk(t, c, tot)
        return (layer, tg[t] * EXPERTS_PER_GROUP + ce // cpe, ce % cpe, 0)

    grid_spec = pltpu.PrefetchScalarGridSpec(
        num_scalar_prefetch=4,
        grid=(t_max, nc),
        in_specs=[
            pl.BlockSpec((EXP_TM, width), rows),
            pl.BlockSpec((None, None, d_model, EXP_HC), w_in_map),
            pl.BlockSpec((None, None, d_model, EXP_HC), w_in_map),
            pl.BlockSpec((None, None, EXP_HC, d_model), w_out_map),
        ],
        out_specs=pl.BlockSpec((EXP_TM, width), rows),
        scratch_shapes=[
            pltpu.VMEM((EXP_TM, d_model), F32),
            pltpu.VMEM((d_model, EXP_HC), BF16),
            pltpu.VMEM((d_model, EXP_HC), BF16),
            pltpu.VMEM((EXP_HC, d_model), BF16),
        ],
    )
    return pl.pallas_call(
        functools.partial(_expert_kernel, cpe),
        grid_spec=grid_spec,
        out_shape=jax.ShapeDtypeStruct(xs.shape, xs.dtype),
        input_output_aliases={4: 0},
        compiler_params=_cparams(("arbitrary", "arbitrary")),
        name="experts",
    )(tile_g, tile_blk, tile_nsub, total, xs, wg, wu, wd)


def _combine_kernel(alpha, ts_ref, back_ref, x1_ref, info_ref, *refs):
    wins = refs[:N_GROUPS]
    g_ref, b_ref, o_ref = refs[N_GROUPS:]
    i = pl.program_id(0)
    tm = x1_ref.shape[0]
    info = info_ref[...]
    gidx = info[:, 0:1]
    rank = info[:, 1:2]
    col = lax.broadcasted_iota(jnp.int32, (tm, tm), 1).astype(F32)
    y = jnp.zeros(x1_ref.shape, F32)
    for g in range(N_GROUPS):
        row_in_window = rank + back_ref[i * N_GROUPS + g].astype(F32)
        pick = jnp.where((gidx == float(g)) & (col == row_in_window), 1.0, 0.0).astype(BF16)
        y = y + jnp.dot(pick, wins[g][...], preferred_element_type=F32)
    o_ref[...] = _layer_norm(alpha * x1_ref[...] + y, g_ref[...], b_ref[...])


def _combine(x1, info, tstart, ends, ys, layer, alpha, ln_g, ln_b, cap):
    m, d_model = x1.shape
    tm = _disp_tile(m)
    last = _tiles_per_group(ends) * EXP_TM - tm
    ts = jnp.minimum(tstart, last[None, :]).reshape(-1)
    back = tstart.reshape(-1) - ts

    def window(g):
        def index_map(i, ts_ref, back_ref):
            return (pl.multiple_of(g * cap + ts_ref[i * N_GROUPS + g], ROW_ALIGN), 0)
        return pl.BlockSpec((pl.Element(tm), pl.Element(d_model)), index_map)

    row = lambda i, ts_ref, back_ref: (i, 0)
    par3 = lambda i, ts_ref, back_ref: (layer, 0, 0)
    grid_spec = pltpu.PrefetchScalarGridSpec(
        num_scalar_prefetch=2,
        grid=(m // tm,),
        in_specs=[pl.BlockSpec((tm, d_model), row), pl.BlockSpec((tm, LANES), row)]
        + [window(g) for g in range(N_GROUPS)]
        + [pl.BlockSpec((None, 1, d_model), par3), pl.BlockSpec((None, 1, d_model), par3)],
        out_specs=pl.BlockSpec((tm, d_model), row),
    )
    return pl.pallas_call(
        functools.partial(_combine_kernel, alpha),
        grid_spec=grid_spec,
        out_shape=jax.ShapeDtypeStruct((m, d_model), F32),
        compiler_params=_cparams(("arbitrary",)),
        name="combine",
    )(ts, back, x1, info, *([ys] * N_GROUPS), ln_g, ln_b)


def _tiles_per_group(ends):
    return jnp.maximum((ends + EXP_TM - 1) // EXP_TM, 1)


def _expert_tiles(ends, t_max):
    per_group = _tiles_per_group(ends)
    cum = jnp.cumsum(per_group)
    total = cum[-1:]
    t = jnp.minimum(jnp.arange(t_max, dtype=jnp.int32), total[0] - 1)
    tile_g = jnp.sum((t[:, None] >= cum[None, :]).astype(jnp.int32), axis=1)
    tile_blk = t - (cum - per_group)[tile_g]
    valid = jnp.clip(ends[tile_g] - tile_blk * EXP_TM, 0, EXP_TM)
    tile_nsub = (valid + EXP_SUB - 1) // EXP_SUB
    return tile_g, tile_blk, tile_nsub.astype(jnp.int32), total.astype(jnp.int32)


def kernel(x_prompt, x_sample, cache_k, cache_v, state_conv, state_sconv, page_table, w_in, conv_w, conv_b, conv_ln_g, conv_ln_b, sconv_w, sb_bias, w_out, ln1_g, ln1_b, router_group_w, router_group_b, router_expert_w, router_expert_b, expert_w_gate, expert_w_up, expert_w_down, ln2_g, ln2_b):
    batch, seq, d_model = x_prompt.shape
    nb = x_sample.shape[0]
    depth = w_in.shape[0]
    d_conf = conv_w.shape[-1]
    d_sc = sconv_w.shape[-1]
    heads = sb_bias.shape[-1]
    d_sb = heads * SB_HEAD_DIM
    alpha = (2 * depth) ** 0.25

    row3 = lambda a: a.reshape(depth, 1, a.shape[-1])
    conv_b3, conv_ln_g3, conv_ln_b3 = row3(conv_b), row3(conv_ln_g), row3(conv_ln_b)
    ln1_g3, ln1_b3, ln2_g3, ln2_b3 = row3(ln1_g), row3(ln1_b), row3(ln2_g), row3(ln2_b)

    n_log = N_GROUPS + N_EXPERTS
    wr = jnp.concatenate(
        [router_group_w, router_expert_w.transpose(0, 2, 1, 3).reshape(depth, d_model, N_EXPERTS)], axis=-1)
    wr = jnp.pad(wr, ((0, 0), (0, 0), (0, LANES - n_log)))
    wr_hi = wr.astype(BF16)
    wr_parts = jnp.concatenate([wr_hi, (wr - wr_hi.astype(F32)).astype(BF16)], axis=-1)
    br = jnp.concatenate([router_group_b, router_expert_b.reshape(depth, N_EXPERTS)], axis=-1)
    br = jnp.pad(br, ((0, 0), (0, LANES - n_log))).reshape(depth, 1, LANES)

    w_in_b = w_in.astype(BF16)
    w_out_b = w_out.astype(BF16)
    experts = (expert_w_gate, expert_w_up, expert_w_down)
    cap, t_max = _moe_layout((batch * seq, nb))
    rows = jnp.zeros((N_GROUPS * cap, d_model + LANES), BF16)
    no_rows = jnp.zeros((N_GROUPS,), jnp.int32)

    cs_t = state_conv.transpose(0, 2, 1, 3)
    ss_t = state_sconv.transpose(0, 2, 1, 3)

    yp = x_prompt.reshape(batch * seq, d_model)
    ys = x_sample.reshape(nb, d_model)
    kp, vp, cp, sp, kd, vd, cd, sd = [], [], [], [], [], [], [], []
    for l in range(depth):
        router = (w_out_b, l, alpha, ln1_g3, ln1_b3, wr_parts, br)
        cg, q, k, v, gch = _inproj(yp, w_in_b, l, d_conf, d_sb, d_sc)
        conf, sc, cst, sst = _mixer_prompt(cg, gch, batch, seq, l, conv_w, conv_b3, conv_ln_g3, conv_ln_b3,
                                           sconv_w)
        att = _attn_prompt(q, k, v, sb_bias[l] * LOG2_E, batch, seq)
        x1p, info_p, tstart_p, rows = _dispatch(conf, att, sc, yp, *router, rows, no_rows, cap)
        kp.append(k.reshape(batch, seq, heads, SB_HEAD_DIM))
        vp.append(v.reshape(batch, seq, heads, SB_HEAD_DIM))
        cp.append(cst[:, CONV_HIST - (CONF_WIDTH - 1):, :])
        sp.append(sst[:, SC_HIST - (SC_WIDTH - 1):, :])

        cg, q, k, v, gch = _inproj(ys, w_in_b, l, d_conf, d_sb, d_sc)
        conf, sc, ncs, nss = _mixer_sample(cg, gch, cs_t, ss_t, l, conv_w, conv_b3, conv_ln_g3, conv_ln_b3,
                                           sconv_w)
        att = _attn_sample(q, sb_bias[l] * LOG2_E, cache_k, cache_v, page_table, l)
        x1s, info_s, tstart_s, rows = _dispatch(conf, att, sc, ys, *router, rows, tstart_p[-1], cap)
        kd.append(k.reshape(nb, 1, heads, SB_HEAD_DIM))
        vd.append(v.reshape(nb, 1, heads, SB_HEAD_DIM))
        cd.append(ncs.transpose(1, 0, 2))
        sd.append(nss.transpose(1, 0, 2))

        ends = tstart_s[-1]
        rows = _experts(rows, *_expert_tiles(ends, t_max), *experts, l, cap)
        yp = _combine(x1p, info_p, tstart_p, ends, rows, l, alpha, ln2_g3, ln2_b3, cap)
        ys = _combine(x1s, info_s, tstart_s, ends, rows, l, alpha, ln2_g3, ln2_b3, cap)

    return (yp.reshape(batch, seq, d_model), ys.reshape(nb, 1, d_model),
            jnp.stack(kp), jnp.stack(vp), jnp.stack(cp), jnp.stack(sp),
            jnp.stack(kd), jnp.stack(vd), jnp.stack(cd), jnp.stack(sd))
```
